```python
import jax, jax.numpy as jnp
from jax import lax
import numpy as np

D_MODEL = 1024
BATCH = 8
SEQ = 2048
DEPTH = 2
DEC_BATCH = 128
DEC_SEQ = 1
PAST_LEN = 16384
PAGE_SIZE = 128

N_MIXERS = 2
N_RWKV_LAYERS = (DEPTH + N_MIXERS - 1) // N_MIXERS
N_POOL_LAYERS = DEPTH // N_MIXERS
N_META = 16
RMS_EPS = 1e-6
RWKV_HEAD = 64
RWKV_HEADS = D_MODEL // RWKV_HEAD
DECAY_LORA = 64
ICLR_LORA = 64
GATE_LORA = 160
GN_EPS = 64e-5
POOL_WINDOWS = (2, 4, 8, 16)
POOL_GROUPS = len(POOL_WINDOWS)
POOL_GROUP_DIM = D_MODEL // POOL_GROUPS
POOL_BUF = max(POOL_WINDOWS) - 1
PEER_HEADS = 8
PEER_NKEYS = 128
PEER_EXPERTS = PEER_NKEYS * PEER_NKEYS
PEER_QDIM = 256
PEER_HALF = PEER_QDIM // 2
PEER_TOPK = 16
PEER_BLOCK = 256

kernel_name = "rwkv7_pool_peer_hybrid_step"

F32 = jnp.float32


def rmsnorm(x, g):
    xf = x.astype(F32)
    xf = xf * lax.rsqrt(jnp.mean(xf * xf, axis=-1, keepdims=True) + RMS_EPS)
    return (xf * g.astype(F32)).astype(x.dtype)


def wkv_step(S, inp):
    r, dec, k, v, aa, bb = inp
    sa = jnp.einsum('bhvk,bhk->bhv', S, aa)
    S = S * dec[:, :, None, :] + sa[..., None] * bb[:, :, None, :] + v[..., None] * k[:, :, None, :]
    y = jnp.einsum('bhvk,bhk->bhv', S, r)
    return S, y


def rwkv7_time_mix(h, shift_prev, S0, mix, w_rkv, dw1, dw2, db, aw1, aw2, ab,
                   gw1, gw2, k_k, k_a, r_k, ln_g, ln_b, w_o):
    B, T, D = h.shape
    dt = h.dtype
    prev = jnp.concatenate([shift_prev[:, None, :].astype(dt), h[:, :-1]], axis=1)
    xx = prev - h
    xs = h[None] + xx[None] * mix[:, None, None, :]
    rkv = jnp.einsum('cbtd,cde->cbte', xs[:3], w_rkv)
    r, k, v = rkv[0], rkv[1], rkv[2]
    xw, xa, xg = xs[3], xs[4], xs[5]
    w = -jax.nn.softplus(-(db + jnp.tanh(xw @ dw1) @ dw2).astype(F32)) - 0.5
    decay = jnp.exp(-jnp.exp(w))
    a = jax.nn.sigmoid((ab + (xa @ aw1) @ aw2).astype(F32))
    g = jax.nn.sigmoid(xg @ gw1) @ gw2

    def heads(t):
        return t.reshape(B, T, RWKV_HEADS, RWKV_HEAD)

    kf = k.astype(F32)
    kk = heads(kf * k_k.astype(F32))
    kk = kk / jnp.maximum(jnp.sqrt(jnp.sum(kk * kk, axis=-1, keepdims=True)), 1e-12)
    kf = kf * (1.0 + (a - 1.0) * k_a.astype(F32))
    r4, k4, v4, a4, d4 = heads(r.astype(F32)), heads(kf), heads(v.astype(F32)), heads(a), heads(decay)

    def tmajor(t):
        return jnp.swapaxes(t, 0, 1)

    xs_scan = (tmajor(r4), tmajor(d4), tmajor(k4), tmajor(v4), tmajor(-kk), tmajor(kk * a4))
    S_T, y = lax.scan(wkv_step, S0.astype(F32), xs_scan)
    y = tmajor(y)
    mu = jnp.mean(y, axis=-1, keepdims=True)
    var = jnp.mean(jnp.square(y - mu), axis=-1, keepdims=True)
    y = ((y - mu) * lax.rsqrt(var + GN_EPS)).reshape(B, T, D) * ln_g.astype(F32) + ln_b.astype(F32)
    bonus = jnp.sum(r4 * k4 * r_k.astype(F32), axis=-1, keepdims=True) * v4
    out = ((y + bonus.reshape(B, T, D)).astype(dt) * g) @ w_o
    return out, S_T.astype(S0.dtype), h[:, -1]


def pool_mix(h, prev, n_prev, pool_w, pool_scale):
    B, T, D = h.shape
    ext = jnp.concatenate([prev.astype(h.dtype), h], axis=1)
    cs = jnp.cumsum(ext.astype(F32), axis=1)
    cs = jnp.concatenate([jnp.zeros((B, 1, D), F32), cs], axis=1)
    pos = jnp.arange(T, dtype=F32)
    hf = h.astype(F32)
    groups = []
    for gi, w in enumerate(POOL_WINDOWS):
        sl = slice(gi * POOL_GROUP_DIM, (gi + 1) * POOL_GROUP_DIM)
        s = cs[:, POOL_BUF + 1:POOL_BUF + 1 + T, sl] - cs[:, POOL_BUF + 1 - w:POOL_BUF + 1 - w + T, sl]
        cnt = jnp.minimum(float(w), pos + 1.0 + n_prev)
        groups.append(s / cnt[None, :, None] - hf[..., sl])
    mixed = jnp.stack(groups, axis=2).astype(h.dtype)
    out = jnp.einsum('btgc,gce->btge', mixed, pool_w).reshape(B, T, D) * pool_scale
    return out, ext[:, -POOL_BUF:]


def peer_ffn(h, wq, keys, U, V):
    B, T, D = h.shape
    n = B * T
    nb = -(-n // PEER_BLOCK)
    flat = jnp.pad(h.reshape(n, D), ((0, nb * PEER_BLOCK - n), (0, 0))).reshape(nb, PEER_BLOCK, D)

    def block(hb):
        q = (hb @ wq).reshape(PEER_BLOCK, PEER_HEADS, 2, PEER_HALF)
        s = jnp.einsum('thcd,hcnd->thcn', q, keys).astype(F32)
        v1, i1 = lax.top_k(s[:, :, 0], PEER_TOPK)
        v2, i2 = lax.top_k(s[:, :, 1], PEER_TOPK)
        cand = (v1[..., :, None] + v2[..., None, :]).reshape(PEER_BLOCK, PEER_HEADS, PEER_TOPK * PEER_TOPK)
        sv, ci = lax.top_k(cand, PEER_TOPK)
        e = (jnp.take_along_axis(i1, ci // PEER_TOPK, axis=-1) * PEER_NKEYS
             + jnp.take_along_axis(i2, ci % PEER_TOPK, axis=-1))
        gate = jax.nn.softmax(sv, axis=-1)
        act = jax.nn.gelu(jnp.einsum('thkd,td->thk', U[e], hb).astype(F32), approximate=False)
        return jnp.einsum('thk,thkd->td', (gate * act).astype(hb.dtype), V[e])

    out = lax.map(block, flat)
    return out.reshape(nb * PEER_BLOCK, D)[:n].reshape(B, T, D)


def trunk(x, wkv0, shift0, pool0, n_prev, P):
    new_wkv, new_shift, new_pool = [], [], []
    for i in range(DEPTH):
        j = i // N_MIXERS
        h = rmsnorm(x, P['norm_mix'][i])
        if i % N_MIXERS == 0:
            out, S, sh = rwkv7_time_mix(
                h, shift0[j], wkv0[j], P['rwkv_mix'][j], P['rwkv_w_rkv'][j],
                P['rwkv_decay_w1'][j], P['rwkv_decay_w2'][j], P['rwkv_decay_b'][j],
                P['rwkv_iclr_w1'][j], P['rwkv_iclr_w2'][j], P['rwkv_iclr_b'][j],
                P['rwkv_gate_w1'][j], P['rwkv_gate_w2'][j], P['rwkv_k_k'][j], P['rwkv_k_a'][j],
                P['rwkv_r_k'][j], P['rwkv_ln_g'][j], P['rwkv_ln_b'][j], P['rwkv_w_o'][j])
            new_wkv.append(S)
            new_shift.append(sh)
        else:
            out, buf = pool_mix(h, pool0[j], n_prev, P['pool_w'][j], P['pool_scale'][j])
            new_pool.append(buf)
        x = x + out
        x = x + peer_ffn(rmsnorm(x, P['norm_ffn'][i]), P['peer_wq'][i], P['peer_keys'][i],
                         P['peer_u'][i], P['peer_v'][i])
    y = rmsnorm(x, P['norm_final'])
    return y, jnp.stack(new_wkv), jnp.stack(new_shift), jnp.stack(new_pool)


def setup_inputs(seed: int = 0) -> dict:
    key = jax.random.key(seed)
    ks = jax.random.split(key, 32)
    D, L, Lp = D_MODEL, N_RWKV_LAYERS, N_POOL_LAYERS
    H, N = RWKV_HEADS, RWKV_HEAD

    def nrm(k, shape, scale):
        return jax.random.normal(k, shape, F32) * scale

    def gain(k, shape):
        return 1.0 + 0.02 * jax.random.normal(k, shape, F32)

    return {
        "x_prompt": nrm(ks[0], (BATCH, SEQ, D), 1.0),
        "x_sample": nrm(ks[1], (DEC_BATCH, DEC_SEQ, D), 1.0),
        "state_wkv": nrm(ks[2], (L, DEC_BATCH, H, N, N), 0.3),
        "state_shift": nrm(ks[3], (L, DEC_BATCH, D), 1.0),
        "state_pool": nrm(ks[4], (Lp, DEC_BATCH, POOL_BUF, D), 1.0),
        "meta_tokens": nrm(ks[5], (N_META, D), 1.0),
        "norm_mix": gain(ks[6], (DEPTH, D)),
        "norm_ffn": gain(ks[7], (DEPTH, D)),
        "norm_final": gain(ks[8], (D,)),
        "rwkv_mix": jax.random.uniform(ks[9], (L, 6, D), F32),
        "rwkv_w_rkv": nrm(ks[10], (L, 3, D, D), D ** -0.5),
        "rwkv_decay_w1": nrm(ks[11], (L, D, DECAY_LORA), D ** -0.5),
        "rwkv_decay_w2": nrm(ks[12], (L, DECAY_LORA, D), 0.5 * DECAY_LORA ** -0.5),
        "rwkv_decay_b": jax.random.uniform(ks[13], (L, D), F32, -6.0, -0.5),
        "rwkv_iclr_w1": nrm(ks[14], (L, D, ICLR_LORA), D ** -0.5),
        "rwkv_iclr_w2": nrm(ks[15], (L, ICLR_LORA, D), 0.5 * ICLR_LORA ** -0.5),
        "rwkv_iclr_b": nrm(ks[16], (L, D), 0.3),
        "rwkv_gate_w1": nrm(ks[17], (L, D, GATE_LORA), D ** -0.5),
        "rwkv_gate_w2": nrm(ks[18], (L, GATE_LORA, D), GATE_LORA ** -0.5),
        "rwkv_k_k": 0.85 + 0.05 * jax.random.normal(ks[19], (L, D), F32),
        "rwkv_k_a": 1.0 + 0.05 * jax.random.normal(ks[20], (L, D), F32),
        "rwkv_r_k": nrm(ks[21], (L, H, N), 0.1),
        "rwkv_ln_g": gain(ks[22], (L, D)),
        "rwkv_ln_b": nrm(ks[23], (L, D), 0.02),
        "rwkv_w_o": nrm(ks[24], (L, D, D), D ** -0.5),
        "pool_w": nrm(ks[25], (Lp, POOL_GROUPS, POOL_GROUP_DIM, POOL_GROUP_DIM), POOL_GROUP_DIM ** -0.5),
        "pool_scale": 0.5 + 0.05 * jax.random.normal(ks[26], (Lp, D), F32),
        "peer_wq": nrm(ks[27], (DEPTH, D, PEER_HEADS * PEER_QDIM), D ** -0.5),
        "peer_keys": nrm(ks[28], (DEPTH, PEER_HEADS, 2, PEER_NKEYS, PEER_HALF), PEER_HALF ** -0.5),
        "peer_u": nrm(ks[29], (DEPTH, PEER_EXPERTS, D), D ** -0.5),
        "peer_v": nrm(ks[30], (DEPTH, PEER_EXPERTS, D), PEER_HEADS ** -0.5),
    }


def reference(x_prompt, x_sample, state_wkv, state_shift, state_pool, meta_tokens,
              norm_mix, norm_ffn, norm_final, rwkv_mix, rwkv_w_rkv, rwkv_decay_w1,
              rwkv_decay_w2, rwkv_decay_b, rwkv_iclr_w1, rwkv_iclr_w2, rwkv_iclr_b,
              rwkv_gate_w1, rwkv_gate_w2, rwkv_k_k, rwkv_k_a, rwkv_r_k, rwkv_ln_g,
              rwkv_ln_b, rwkv_w_o, pool_w, pool_scale, peer_wq, peer_keys, peer_u, peer_v):
    P = {
        'norm_mix': norm_mix, 'norm_ffn': norm_ffn, 'norm_final': norm_final,
        'rwkv_mix': rwkv_mix, 'rwkv_w_rkv': rwkv_w_rkv,
        'rwkv_decay_w1': rwkv_decay_w1, 'rwkv_decay_w2': rwkv_decay_w2, 'rwkv_decay_b': rwkv_decay_b,
        'rwkv_iclr_w1': rwkv_iclr_w1, 'rwkv_iclr_w2': rwkv_iclr_w2, 'rwkv_iclr_b': rwkv_iclr_b,
        'rwkv_gate_w1': rwkv_gate_w1, 'rwkv_gate_w2': rwkv_gate_w2,
        'rwkv_k_k': rwkv_k_k, 'rwkv_k_a': rwkv_k_a, 'rwkv_r_k': rwkv_r_k,
        'rwkv_ln_g': rwkv_ln_g, 'rwkv_ln_b': rwkv_ln_b, 'rwkv_w_o': rwkv_w_o,
        'pool_w': pool_w, 'pool_scale': pool_scale,
        'peer_wq': peer_wq, 'peer_keys': peer_keys, 'peer_u': peer_u, 'peer_v': peer_v,
    }
    dt = x_prompt.dtype
    meta = jnp.broadcast_to(meta_tokens.astype(dt)[None], (BATCH, N_META, D_MODEL))
    xp = jnp.concatenate([meta, x_prompt], axis=1)
    wkv0 = jnp.zeros((N_RWKV_LAYERS, BATCH, RWKV_HEADS, RWKV_HEAD, RWKV_HEAD), dt)
    shift0 = jnp.zeros((N_RWKV_LAYERS, BATCH, D_MODEL), dt)
    pool0 = jnp.zeros((N_POOL_LAYERS, BATCH, POOL_BUF, D_MODEL), dt)
    yp_full, wkv_p, shift_p, pool_p = trunk(xp, wkv0, shift0, pool0, 0.0, P)
    y_prompt = yp_full[:, N_META:]
    y_sample, wkv_s, shift_s, pool_s = trunk(x_sample, state_wkv, state_shift, state_pool,
                                             float(POOL_BUF), P)
    return (y_prompt, y_sample, wkv_p, shift_p, pool_p, wkv_s, shift_s, pool_s)
```

```python
import functools

import jax
import jax.numpy as jnp
import numpy as np
from jax import lax
from jax.experimental import pallas as pl
from jax.experimental.pallas import tpu as pltpu

F32 = jnp.float32
BF16 = jnp.bfloat16

N_META = 16
RMS_EPS = 1e-6
GN_EPS = 64e-5
HEAD_DIM = 64
LANES = 128
POOL_WINDOWS = (2, 4, 8, 16)
POOL_HALO = 16
PEER_HEADS = 8
PEER_NKEYS = 128
PEER_TOPK = 16
PEER_CHUNK = 1024
VMEM_LIMIT = 56 * 1024 * 1024

NEG_INF = float("-inf")


def _cparams(sem):
    return pltpu.CompilerParams(dimension_semantics=sem, vmem_limit_bytes=VMEM_LIMIT)


def _row_tile(n, pref):
    for t in (pref, 256, 128, 64, 32, 16, 8):
        if t <= n and n % t == 0:
            return t
    return n


def _rms(x, g):
    return x * lax.rsqrt(jnp.mean(x * x, axis=-1, keepdims=True) + RMS_EPS) * g


def _bdot(a, b):
    return jnp.dot(a.astype(BF16), b.astype(BF16), preferred_element_type=F32)


def _head_sum(x, blockdiag):
    outs = []
    for blk in range(x.shape[-1] // LANES):
        xb = x[:, blk * LANES:(blk + 1) * LANES]
        hi = xb.astype(BF16)
        lo = (xb - hi.astype(F32)).astype(BF16)
        outs.append(jnp.dot(hi, blockdiag, preferred_element_type=F32)
                    + jnp.dot(lo, blockdiag, preferred_element_type=F32))
    return jnp.concatenate(outs, axis=-1)


def _norm_kernel(x_ref, g_ref, h_ref):
    h_ref[...] = _rms(x_ref[...], g_ref[...])


def _norm_call(x, g):
    n, d = x.shape
    tm = _row_tile(n, 512)
    row = pl.BlockSpec((tm, d), lambda i: (i, 0))
    return pl.pallas_call(
        _norm_kernel, grid=(n // tm,),
        in_specs=[row, pl.BlockSpec((1, d), lambda i: (0, 0))],
        out_specs=row, out_shape=jax.ShapeDtypeStruct((n, d), F32),
        compiler_params=_cparams(("parallel",)), name="rmsnorm")(x, g.reshape(1, d))


def _add_norm_kernel(a_ref, b_ref, g_ref, x_ref, h_ref):
    x = a_ref[...] + b_ref[...]
    x_ref[...] = x
    h_ref[...] = _rms(x, g_ref[...])


def _add_norm_call(a, b, g):
    n, d = a.shape
    tm = _row_tile(n, 512)
    row = pl.BlockSpec((tm, d), lambda i: (i, 0))
    return pl.pallas_call(
        _add_norm_kernel, grid=(n // tm,),
        in_specs=[row, row, pl.BlockSpec((1, d), lambda i: (0, 0))],
        out_specs=[row, row],
        out_shape=[jax.ShapeDtypeStruct((n, d), F32)] * 2,
        compiler_params=_cparams(("parallel",)), name="add_rmsnorm")(a, b, g.reshape(1, d))


def _rwkv_proj_kernel(h_ref, prev_ref, mix_ref, wrkv_ref, dw1_ref, dw2_ref, db_ref,
                      aw1_ref, aw2_ref, ab_ref, gw1_ref, gw2_ref, kk_ref, ka_ref, bd_ref,
                      r_ref, k_ref, v_ref, dec_ref, aa_ref, bb_ref, g_ref):
    h = h_ref[...]
    xx = prev_ref[...] - h

    def xs(c):
        return h + xx * mix_ref[c:c + 1, :]

    r = _bdot(xs(0), wrkv_ref[0])
    k = _bdot(xs(1), wrkv_ref[1])
    v = _bdot(xs(2), wrkv_ref[2])
    w = -jax.nn.softplus(-(db_ref[...] + _bdot(jnp.tanh(_bdot(xs(3), dw1_ref[...])), dw2_ref[...]))) - 0.5
    a = jax.nn.sigmoid(ab_ref[...] + _bdot(_bdot(xs(4), aw1_ref[...]), aw2_ref[...]))
    g = _bdot(jax.nn.sigmoid(_bdot(xs(5), gw1_ref[...])), gw2_ref[...])
    kk = k * kk_ref[...]
    nrm = jnp.sqrt(_head_sum(kk * kk, bd_ref[...]))
    kk = kk / jnp.maximum(nrm, 1e-12)
    r_ref[...] = r
    k_ref[...] = k * (1.0 + (a - 1.0) * ka_ref[...])
    v_ref[...] = v
    dec_ref[...] = jnp.exp(-jnp.exp(w))
    aa_ref[...] = -kk
    bb_ref[...] = kk * a
    g_ref[...] = g


def _rwkv_proj_call(h, prev, mix, w_rkv, dw1, dw2, db, aw1, aw2, ab, gw1, gw2, k_k, k_a, blockdiag):
    n, d = h.shape
    tm = _row_tile(n, 256)
    row = pl.BlockSpec((tm, d), lambda i: (i, 0))

    def full(x):
        nd = x.ndim
        return pl.BlockSpec(x.shape, lambda i: (0,) * nd)

    params = [mix, w_rkv.astype(BF16), dw1.astype(BF16), dw2.astype(BF16), db.reshape(1, d),
              aw1.astype(BF16), aw2.astype(BF16), ab.reshape(1, d), gw1.astype(BF16),
              gw2.astype(BF16), k_k.reshape(1, d), k_a.reshape(1, d), blockdiag]
    return pl.pallas_call(
        _rwkv_proj_kernel, grid=(n // tm,),
        in_specs=[row, row] + [full(p) for p in params],
        out_specs=[row] * 7,
        out_shape=[jax.ShapeDtypeStruct((n, d), F32)] * 7,
        compiler_params=_cparams(("parallel",)), name="rwkv_proj")(h, prev, *params)


def _wkv_kernel(r_ref, d_ref, k_ref, v_ref, a_ref, b_ref, s0_ref, y_ref, s_ref, *, steps):
    nk = s_ref.shape[0]

    @pl.when(pl.program_id(1) == 0)
    def _():
        s_ref[...] = s0_ref[...]

    def step(t, carry):
        def row(ref, kk):
            return ref[t, kk:kk + 1, :]

        parts = [None] * 4
        for kk in range(nk):
            term = s_ref[kk] * row(a_ref, kk)
            parts[kk % 4] = term if parts[kk % 4] is None else parts[kk % 4] + term
        sa = (parts[0] + parts[1]) + (parts[2] + parts[3])
        vv = v_ref[t]
        parts = [None] * 4
        for kk in range(nk):
            sk = s_ref[kk] * row(d_ref, kk) + sa * row(b_ref, kk) + vv * row(k_ref, kk)
            s_ref[kk] = sk
            term = sk * row(r_ref, kk)
            parts[kk % 4] = term if parts[kk % 4] is None else parts[kk % 4] + term
        y_ref[t] = (parts[0] + parts[1]) + (parts[2] + parts[3])
        return carry

    lax.fori_loop(0, steps, step, 0)


def _wkv_call(r, dec, k, v, aa, bb, s0, steps):
    t_len, hd, lanes = r.shape
    seq = pl.BlockSpec((steps, hd, LANES), lambda l, t: (t, 0, l))
    st = pl.BlockSpec((hd, hd, LANES), lambda l, t: (0, 0, l))
    return pl.pallas_call(
        functools.partial(_wkv_kernel, steps=steps),
        grid=(lanes // LANES, t_len // steps),
        in_specs=[seq] * 6 + [st],
        out_specs=[seq, st],
        out_shape=[jax.ShapeDtypeStruct((t_len, hd, lanes), F32),
                   jax.ShapeDtypeStruct((hd, hd, lanes), F32)],
        compiler_params=_cparams(("parallel", "arbitrary")), name="wkv_scan")(r, dec, k, v, aa, bb, s0)


def _to_scan(x, b, t):
    heads = x.shape[-1] // HEAD_DIM
    return x.reshape(b, t, heads, HEAD_DIM).transpose(1, 3, 0, 2).reshape(t, HEAD_DIM, b * heads)


def _from_scan(y, b, t):
    heads = y.shape[-1] // b
    return y.reshape(t, HEAD_DIM, b, heads).transpose(2, 0, 3, 1).reshape(b * t, heads * HEAD_DIM)


def _wkv_group(rows, s0, b, t, steps):
    heads = s0.shape[1]
    s0_l = s0.transpose(3, 2, 0, 1).reshape(HEAD_DIM, HEAD_DIM, b * heads)
    y, s_l = _wkv_call(*[_to_scan(x, b, t) for x in rows], s0_l, steps)
    s_out = s_l.reshape(HEAD_DIM, HEAD_DIM, b, heads).transpose(2, 3, 1, 0)
    return _from_scan(y, b, t), s_out


def _rwkv_out_kernel(y_ref, r_ref, k_ref, v_ref, g_ref, x_ref, lng_ref, lnb_ref, rk_ref, wo_ref,
                     bd_ref, o_ref):
    bd = bd_ref[...]
    y = y_ref[...]
    inv_n = 1.0 / HEAD_DIM
    mu = _head_sum(y, bd) * inv_n
    yc = y - mu
    var = _head_sum(yc * yc, bd) * inv_n
    yn = yc * lax.rsqrt(var + GN_EPS) * lng_ref[...] + lnb_ref[...]
    bonus = _head_sum(r_ref[...] * k_ref[...] * rk_ref[...], bd) * v_ref[...]
    o_ref[...] = x_ref[...] + _bdot((yn + bonus) * g_ref[...], wo_ref[...])


def _rwkv_out_call(y, r, k, v, g, x, ln_g, ln_b, r_k, w_o, blockdiag):
    n, d = y.shape
    tm = _row_tile(n, 256)
    row = pl.BlockSpec((tm, d), lambda i: (i, 0))
    vec = pl.BlockSpec((1, d), lambda i: (0, 0))
    return pl.pallas_call(
        _rwkv_out_kernel, grid=(n // tm,),
        in_specs=[row] * 6 + [vec, vec, vec, pl.BlockSpec((d, d), lambda i: (0, 0)),
                              pl.BlockSpec((LANES, LANES), lambda i: (0, 0))],
        out_specs=row, out_shape=jax.ShapeDtypeStruct((n, d), F32),
        compiler_params=_cparams(("parallel",)), name="rwkv_out")(
            y, r, k, v, g, x, ln_g.reshape(1, d), ln_b.reshape(1, d), r_k.reshape(1, d),
            w_o.astype(BF16), blockdiag)


def _peer_scores_kernel(x_ref, g_ref, wq_ref, keys_ref, s_ref, ht_ref):
    h = _rms(x_ref[...], g_ref[...])
    ht_ref[...] = h.T.astype(BF16)
    q = _bdot(h, wq_ref[...]).astype(BF16)
    for hc in range(keys_ref.shape[0]):
        s_ref[hc] = lax.dot_general(keys_ref[hc], q[:, hc * LANES:(hc + 1) * LANES],
                                    (((1,), (1,)), ((), ())), preferred_element_type=F32)


def _peer_scores_call(x, g, wq, keys):
    n, d = x.shape
    tm = _row_tile(n, 256)
    nhc = keys.shape[0]
    return pl.pallas_call(
        _peer_scores_kernel, grid=(n // tm,),
        in_specs=[pl.BlockSpec((tm, d), lambda i: (i, 0)), pl.BlockSpec((1, d), lambda i: (0, 0)),
                  pl.BlockSpec(wq.shape, lambda i: (0, 0)),
                  pl.BlockSpec(keys.shape, lambda i: (0, 0, 0))],
        out_specs=[pl.BlockSpec((nhc, PEER_NKEYS, tm), lambda i: (0, 0, i)),
                   pl.BlockSpec((d, tm), lambda i: (0, i))],
        out_shape=[jax.ShapeDtypeStruct((nhc, PEER_NKEYS, n), F32),
                   jax.ShapeDtypeStruct((d, n), BF16)],
        compiler_params=_cparams(("parallel",)), name="peer_scores")(x, g.reshape(1, d), wq, keys)


def _top16(s, iota):
    rank = jnp.full(s.shape, float(PEER_TOPK), F32)
    vals = []
    for a in range(PEER_TOPK):
        m = jnp.max(s, axis=0, keepdims=True)
        idx = jnp.min(jnp.where(s == m, iota, PEER_NKEYS), axis=0, keepdims=True)
        hit = iota == idx
        rank = jnp.where(hit, float(a), rank)
        s = jnp.where(hit, NEG_INF, s)
        vals.append(m)
    return rank, vals


def _peer_select_kernel(s_ref, l_ref, p1_ref, r2_ref, p2_ref):
    lanes = s_ref.shape[-1]
    iota = lax.broadcasted_iota(jnp.int32, (PEER_NKEYS, lanes), 0)
    iota16 = lax.broadcasted_iota(jnp.int32, (PEER_TOPK, lanes), 0)
    for h in range(PEER_HEADS):
        s1 = s_ref[2 * h]
        s2 = s_ref[2 * h + 1]
        rank1, v1 = _top16(s1, iota)
        rank2, v2 = _top16(s2, iota)
        v1a = jnp.concatenate(v1, axis=0)
        v2a = jnp.concatenate(v2, axis=0)
        top = v1[0] + v2[0]
        taken = jnp.zeros((PEER_TOPK, lanes), F32)
        front = v1a + v2[0]
        zsum = jnp.zeros((1, lanes), F32)
        for _ in range(PEER_TOPK):
            m = jnp.max(front, axis=0, keepdims=True)
            a_star = jnp.min(jnp.where(front == m, iota16, PEER_TOPK), axis=0, keepdims=True)
            hit = iota16 == a_star
            zsum = zsum + jnp.exp(m - top)
            taken = jnp.where(hit, taken + 1.0, taken)
            cnt = jnp.max(jnp.where(hit, taken, -1.0), axis=0, keepdims=True)
            nxt = jnp.max(jnp.where(iota16.astype(F32) == cnt, v2a, NEG_INF), axis=0, keepdims=True)
            front = jnp.where(hit, v1a + nxt, front)
        lim = jnp.zeros((PEER_NKEYS, lanes), F32)
        for a in range(PEER_TOPK):
            lim = jnp.where(rank1 == float(a), taken[a:a + 1, :], lim)
        l_ref[h] = lim
        r2_ref[h] = rank2
        p1_ref[h] = jnp.exp(s1 - v1[0])
        p2_ref[h] = jnp.exp(s2 - v2[0]) / zsum


def _peer_select_call(scores):
    nhc, nk, n = scores.shape
    tl = LANES
    out = pl.BlockSpec((PEER_HEADS, nk, tl), lambda i: (0, 0, i))
    return pl.pallas_call(
        _peer_select_kernel, grid=(n // tl,),
        in_specs=[pl.BlockSpec((nhc, nk, tl), lambda i: (0, 0, i))],
        out_specs=[out] * 4,
        out_shape=[jax.ShapeDtypeStruct((PEER_HEADS, nk, n), F32)] * 4,
        compiler_params=_cparams(("parallel",)), name="peer_select")(scores)


def _peer_dense_kernel(ht_ref, u_ref, vt_ref, l_ref, p1_ref, r2_ref, p2_ref, o_ref, z_ref, g_ref):
    c = pl.program_id(1)

    @pl.when(c == 0)
    def _():
        o_ref[...] = jnp.zeros_like(o_ref)

    z_ref[...] = jnp.dot(u_ref[...], ht_ref[...], preferred_element_type=F32)
    blocks = PEER_CHUNK // PEER_NKEYS
    for ii in range(blocks):
        i = c * blocks + ii
        z = z_ref[ii * PEER_NKEYS:(ii + 1) * PEER_NKEYS, :]
        act = 0.5 * z * (1.0 + lax.erf(z * np.float32(np.sqrt(0.5))))
        w = jnp.zeros_like(z)
        for h in range(PEER_HEADS):
            lim = l_ref[h, pl.ds(i, 1), :]
            p1 = p1_ref[h, pl.ds(i, 1), :]
            w = w + jnp.where(r2_ref[h] < lim, p2_ref[h], 0.0) * p1
        g_ref[ii * PEER_NKEYS:(ii + 1) * PEER_NKEYS, :] = (act * w).astype(BF16)
    o_ref[...] += jnp.dot(vt_ref[...], g_ref[...], preferred_element_type=F32)


def _peer_dense_call(ht, u, vt, lim, p1, r2, p2):
    d, n = ht.shape
    ne = u.shape[0]
    tt = 256 if n % 256 == 0 else LANES
    sel = pl.BlockSpec((PEER_HEADS, PEER_NKEYS, tt), lambda t, c: (0, 0, t))
    return pl.pallas_call(
        _peer_dense_kernel, grid=(n // tt, ne // PEER_CHUNK),
        in_specs=[pl.BlockSpec((d, tt), lambda t, c: (0, t)),
                  pl.BlockSpec((PEER_CHUNK, d), lambda t, c: (c, 0)),
                  pl.BlockSpec((d, PEER_CHUNK), lambda t, c: (0, c)),
                  sel, sel, sel, sel],
        out_specs=pl.BlockSpec((d, tt), lambda t, c: (0, t)),
        out_shape=jax.ShapeDtypeStruct((d, n), F32),
        scratch_shapes=[pltpu.VMEM((PEER_CHUNK, tt), F32), pltpu.VMEM((PEER_CHUNK, tt), BF16)],
        compiler_params=_cparams(("parallel", "arbitrary")), name="peer_dense")(
            ht, u, vt, lim, p1, r2, p2)


def _peer(x, g, wq, keys, u, v):
    nh, _, nk, half = keys.shape
    scores, ht = _peer_scores_call(x, g, wq.astype(BF16), keys.reshape(nh * 2, nk, half).astype(BF16))
    lim, p1, r2, p2 = _peer_select_call(scores)
    out_t = _peer_dense_call(ht, u.astype(BF16), v.T.astype(BF16), lim, p1, r2, p2)
    return out_t.T


def _window_sums(ext, halo):
    d = ext.shape[-1]
    gd = d // len(POOL_WINDOWS)
    outs = []
    for gi, w in enumerate(POOL_WINDOWS):
        acc = ext[:, gi * gd:(gi + 1) * gd]
        shift = 1
        while shift < w:
            acc = acc + pltpu.roll(acc, shift, 0)
            shift *= 2
        outs.append(acc[halo:, :])
    return outs


def _pool_prompt_kernel(prev_ref, cur_ref, x_ref, w_ref, sc_ref, o_ref):
    tm = cur_ref.shape[1]
    cur = cur_ref[0]
    ext = jnp.concatenate([prev_ref[0, tm - POOL_HALO:, :], cur], axis=0)
    sums = _window_sums(ext, POOL_HALO)
    pos = (pl.program_id(1) * tm + lax.broadcasted_iota(jnp.int32, (tm, 1), 0)).astype(F32)
    gd = cur.shape[-1] // len(POOL_WINDOWS)
    outs = []
    for gi, w in enumerate(POOL_WINDOWS):
        cnt = jnp.minimum(float(w), pos + 1.0)
        mixed = sums[gi] / cnt - cur[:, gi * gd:(gi + 1) * gd]
        outs.append(_bdot(mixed, w_ref[gi]))
    o_ref[0] = x_ref[0] + jnp.concatenate(outs, axis=-1) * sc_ref[...]


def _pool_prompt_call(h, x, pool_w, pool_scale, tm):
    b, t, d = h.shape
    hp = jnp.concatenate([jnp.zeros((b, tm, d), F32), h], axis=1)
    ng = pool_w.shape[0]
    return pl.pallas_call(
        _pool_prompt_kernel, grid=(b, t // tm),
        in_specs=[pl.BlockSpec((1, tm, d), lambda i, j: (i, j, 0)),
                  pl.BlockSpec((1, tm, d), lambda i, j: (i, j + 1, 0)),
                  pl.BlockSpec((1, tm, d), lambda i, j: (i, j, 0)),
                  pl.BlockSpec(pool_w.shape, lambda i, j: (0, 0, 0)),
                  pl.BlockSpec((1, d), lambda i, j: (0, 0))],
        out_specs=pl.BlockSpec((1, tm, d), lambda i, j: (i, j, 0)),
        out_shape=jax.ShapeDtypeStruct((b, t, d), F32),
        compiler_params=_cparams(("parallel", "parallel")), name="pool_prompt")(
            hp, hp, x, pool_w.astype(BF16), pool_scale.reshape(1, d))


def _pool_sample_kernel(hist_ref, h_ref, x_ref, w_ref, sc_ref, o_ref):
    h = h_ref[...]
    nhist = hist_ref.shape[0]
    gd = h.shape[-1] // len(POOL_WINDOWS)
    outs = []
    for gi, w in enumerate(POOL_WINDOWS):
        sl = slice(gi * gd, (gi + 1) * gd)
        acc = h[:, sl]
        for back in range(1, w):
            acc = acc + hist_ref[nhist - back][:, sl]
        mixed = acc / float(w) - h[:, sl]
        outs.append(_bdot(mixed, w_ref[gi]))
    o_ref[...] = x_ref[...] + jnp.concatenate(outs, axis=-1) * sc_ref[...]


def _pool_sample_call(hist, h, x, pool_w, pool_scale):
    b, d = h.shape
    full2 = pl.BlockSpec((b, d), lambda i: (0, 0))
    return pl.pallas_call(
        _pool_sample_kernel, grid=(1,),
        in_specs=[pl.BlockSpec(hist.shape, lambda i: (0, 0, 0)), full2, full2,
                  pl.BlockSpec(pool_w.shape, lambda i: (0, 0, 0)),
                  pl.BlockSpec((1, d), lambda i: (0, 0))],
        out_specs=full2, out_shape=jax.ShapeDtypeStruct((b, d), F32),
        compiler_params=_cparams(("arbitrary",)), name="pool_sample")(
            hist, h, x, pool_w.astype(BF16), pool_scale.reshape(1, d))


def _time_block(t, pref):
    for c in range(min(pref, t), 0, -1):
        if t % c == 0 and (c % 8 == 0 or c == t):
            return c
    return t


def kernel(x_prompt, x_sample, state_wkv, state_shift, state_pool, meta_tokens, norm_mix, norm_ffn, norm_final, rwkv_mix, rwkv_w_rkv, rwkv_decay_w1, rwkv_decay_w2, rwkv_decay_b, rwkv_iclr_w1, rwkv_iclr_w2, rwkv_iclr_b, rwkv_gate_w1, rwkv_gate_w2, rwkv_k_k, rwkv_k_a, rwkv_r_k, rwkv_ln_g, rwkv_ln_b, rwkv_w_o, pool_w, pool_scale, peer_wq, peer_keys, peer_u, peer_v):
    bp, seq, d = x_prompt.shape
    bs = x_sample.shape[0]
    tp = N_META + seq
    n_p = bp * tp
    heads = d // HEAD_DIM
    assert x_sample.shape[1] == 1 and (bp * heads) % LANES == 0 and (bs * heads) % LANES == 0
    assert rwkv_w_rkv.shape[0] == 1 and pool_w.shape[0] == 1 and norm_mix.shape[0] == 2

    blockdiag = jnp.asarray(np.kron(np.eye(LANES // HEAD_DIM), np.ones((HEAD_DIM, HEAD_DIM))), BF16)
    meta = jnp.broadcast_to(meta_tokens[None], (bp, N_META, d))
    x0 = jnp.concatenate([jnp.concatenate([meta, x_prompt], axis=1).reshape(n_p, d),
                          x_sample.reshape(bs, d)], axis=0)

    h0 = _norm_call(x0, norm_mix[0])
    h0p = h0[:n_p].reshape(bp, tp, d)
    prev = jnp.concatenate([jnp.concatenate([jnp.zeros((bp, 1, d), F32), h0p[:, :-1]], axis=1).reshape(n_p, d),
                            state_shift[0]], axis=0)
    proj = _rwkv_proj_call(h0, prev, rwkv_mix[0], rwkv_w_rkv[0], rwkv_decay_w1[0], rwkv_decay_w2[0],
                           rwkv_decay_b[0], rwkv_iclr_w1[0], rwkv_iclr_w2[0], rwkv_iclr_b[0],
                           rwkv_gate_w1[0], rwkv_gate_w2[0], rwkv_k_k[0], rwkv_k_a[0], blockdiag)
    r, k, v, dec, aa, bb, g = proj
    scan_rows = (r, dec, k, v, aa, bb)
    y_p, wkv_p = _wkv_group([a[:n_p] for a in scan_rows],
                            jnp.zeros((bp, heads, HEAD_DIM, HEAD_DIM), F32), bp, tp, _time_block(tp, 48))
    y_s, wkv_s = _wkv_group([a[n_p:] for a in scan_rows], state_wkv[0], bs, 1, 1)
    y = jnp.concatenate([y_p, y_s], axis=0)
    x1 = _rwkv_out_call(y, r, k, v, g, x0, rwkv_ln_g[0], rwkv_ln_b[0], rwkv_r_k[0], rwkv_w_o[0], blockdiag)
    f0 = _peer(x1, norm_ffn[0], peer_wq[0], peer_keys[0], peer_u[0], peer_v[0])

    x2, h1 = _add_norm_call(x1, f0, norm_mix[1])
    h1p = h1[:n_p].reshape(bp, tp, d)
    h1s = h1[n_p:]
    x3p = _pool_prompt_call(h1p, x2[:n_p].reshape(bp, tp, d), pool_w[0], pool_scale[0], _time_block(tp, 344))
    hist = state_pool[0].transpose(1, 0, 2)
    x3s = _pool_sample_call(hist, h1s, x2[n_p:], pool_w[0], pool_scale[0])
    x3 = jnp.concatenate([x3p.reshape(n_p, d), x3s], axis=0)
    f1 = _peer(x3, norm_ffn[1], peer_wq[1], peer_keys[1], peer_u[1], peer_v[1])
    _, yf = _add_norm_call(x3, f1, norm_final)

    nbuf = state_pool.shape[2]
    y_prompt = yf[:n_p].reshape(bp, tp, d)[:, N_META:]
    y_sample = yf[n_p:].reshape(bs, 1, d)
    shift_p = h0p[:, -1][None]
    shift_s = h0[n_p:][None]
    pool_p = jnp.concatenate([jnp.zeros((bp, nbuf, d), F32), h1p], axis=1)[:, tp:][None]
    pool_s = jnp.concatenate([state_pool[0][:, 1:], h1s[:, None, :]], axis=1)[None]
    return (y_prompt, y_sample, wkv_p[None], shift_p, pool_p, wkv_s[None], shift_s, pool_s)
```

```python
import functools

import jax
import jax.numpy as jnp
import numpy as np
from jax import lax
from jax.experimental import pallas as pl
from jax.experimental.pallas import tpu as pltpu

F32 = jnp.float32
BF16 = jnp.bfloat16

N_META = 16
RMS_EPS = 1e-6
GN_EPS = 64e-5
HEAD_DIM = 64
LANES = 128
POOL_WINDOWS = (2, 4, 8, 16)
POOL_HALO = 16
PEER_HEADS = 8
PEER_NKEYS = 128
PEER_TOPK = 16
PEER_CHUNK = 1024
PEER_TOKENS = 512
MASK_DTYPE = jnp.bfloat16
MASK_ROWS = 16
ROW_TILE = 256
VMEM_LIMIT = 56 * 1024 * 1024

NEG_INF = float("-inf")


def _cparams(sem):
    return pltpu.CompilerParams(dimension_semantics=sem, vmem_limit_bytes=VMEM_LIMIT)


def _row_tile(n):
    return min(ROW_TILE, n)


def _rms(x, g):
    return x * lax.rsqrt(jnp.mean(x * x, axis=-1, keepdims=True) + RMS_EPS) * g


def _bdot(a, b):
    return jnp.dot(a.astype(BF16), b.astype(BF16), preferred_element_type=F32)


def _head_sum(x, blockdiag):
    outs = []
    for blk in range(x.shape[-1] // LANES):
        xb = x[:, blk * LANES:(blk + 1) * LANES]
        hi = xb.astype(BF16)
        lo = (xb - hi.astype(F32)).astype(BF16)
        outs.append(jnp.dot(hi, blockdiag, preferred_element_type=F32)
                    + jnp.dot(lo, blockdiag, preferred_element_type=F32))
    return jnp.concatenate(outs, axis=-1)


def _norm_kernel(x_ref, g_ref, h_ref):
    h_ref[...] = _rms(x_ref[...], g_ref[...])


def _norm_call(x, g):
    n, d = x.shape
    tm = _row_tile(n)
    row = pl.BlockSpec((tm, d), lambda i: (i, 0))
    return pl.pallas_call(
        _norm_kernel, grid=(pl.cdiv(n, tm),),
        in_specs=[row, pl.BlockSpec((1, d), lambda i: (0, 0))],
        out_specs=row, out_shape=jax.ShapeDtypeStruct((n, d), F32),
        compiler_params=_cparams(("parallel",)), name="rmsnorm")(x, g.reshape(1, d))


def _add_norm_kernel(a_ref, bt_ref, g_ref, x_ref, h_ref):
    x = a_ref[...] + bt_ref[...].T
    x_ref[...] = x
    h_ref[...] = _rms(x, g_ref[...])


def _add_norm_call(a, bt, g):
    n, d = a.shape
    tm = _row_tile(n)
    row = pl.BlockSpec((tm, d), lambda i: (i, 0))
    return pl.pallas_call(
        _add_norm_kernel, grid=(pl.cdiv(n, tm),),
        in_specs=[row, pl.BlockSpec((d, tm), lambda i: (0, i)), pl.BlockSpec((1, d), lambda i: (0, 0))],
        out_specs=[row, row],
        out_shape=[jax.ShapeDtypeStruct((n, d), F32)] * 2,
        compiler_params=_cparams(("parallel",)), name="add_rmsnorm")(a, bt, g.reshape(1, d))


def _rwkv_proj_kernel(h_ref, prev_ref, mix_ref, wrkv_ref, dw1_ref, dw2_ref, db_ref,
                      aw1_ref, aw2_ref, ab_ref, gw1_ref, gw2_ref, kk_ref, ka_ref, bd_ref,
                      r_ref, k_ref, v_ref, dec_ref, aa_ref, bb_ref, g_ref):
    h = h_ref[...]
    xx = prev_ref[...] - h

    def xs(c):
        return h + xx * mix_ref[c:c + 1, :]

    r = _bdot(xs(0), wrkv_ref[0])
    k = _bdot(xs(1), wrkv_ref[1])
    v = _bdot(xs(2), wrkv_ref[2])
    w = -jax.nn.softplus(-(db_ref[...] + _bdot(jnp.tanh(_bdot(xs(3), dw1_ref[...])), dw2_ref[...]))) - 0.5
    a = jax.nn.sigmoid(ab_ref[...] + _bdot(_bdot(xs(4), aw1_ref[...]), aw2_ref[...]))
    g = _bdot(jax.nn.sigmoid(_bdot(xs(5), gw1_ref[...])), gw2_ref[...])
    kk = k * kk_ref[...]
    nrm = jnp.sqrt(_head_sum(kk * kk, bd_ref[...]))
    kk = kk / jnp.maximum(nrm, 1e-12)
    r_ref[...] = r
    k_ref[...] = k * (1.0 + (a - 1.0) * ka_ref[...])
    v_ref[...] = v
    dec_ref[...] = jnp.exp(-jnp.exp(w))
    aa_ref[...] = -kk
    bb_ref[...] = kk * a
    g_ref[...] = g


def _rwkv_proj_params(mix, w_rkv, dw1, dw2, db, aw1, aw2, ab, gw1, gw2, k_k, k_a, blockdiag):
    d = mix.shape[-1]
    return [mix, w_rkv.astype(BF16), dw1.astype(BF16), dw2.astype(BF16), db.reshape(1, d),
            aw1.astype(BF16), aw2.astype(BF16), ab.reshape(1, d), gw1.astype(BF16),
            gw2.astype(BF16), k_k.reshape(1, d), k_a.reshape(1, d), blockdiag]


def _rwkv_proj_call(h, prev, params):
    n, d = h.shape
    tm = _row_tile(n)
    row = pl.BlockSpec((tm, d), lambda i: (i, 0))

    def full(x):
        nd = x.ndim
        return pl.BlockSpec(x.shape, lambda i: (0,) * nd)

    return pl.pallas_call(
        _rwkv_proj_kernel, grid=(pl.cdiv(n, tm),),
        in_specs=[row, row] + [full(p) for p in params],
        out_specs=[row] * 7,
        out_shape=[jax.ShapeDtypeStruct((n, d), F32)] * 7,
        compiler_params=_cparams(("parallel",)), name="rwkv_proj")(h, prev, *params)


def _wkv_kernel(r_ref, d_ref, k_ref, v_ref, a_ref, b_ref, s0_ref, y_ref, s_ref, *, steps):
    nk = s_ref.shape[0]

    @pl.when(pl.program_id(1) == 0)
    def _():
        s_ref[...] = s0_ref[...]

    def step(t, carry):
        def row(ref, kk):
            return ref[t, kk:kk + 1, :]

        parts = [None] * 4
        for kk in range(nk):
            term = s_ref[kk] * row(a_ref, kk)
            parts[kk % 4] = term if parts[kk % 4] is None else parts[kk % 4] + term
        sa = (parts[0] + parts[1]) + (parts[2] + parts[3])
        vv = v_ref[t]
        parts = [None] * 4
        for kk in range(nk):
            sk = s_ref[kk] * row(d_ref, kk) + sa * row(b_ref, kk) + vv * row(k_ref, kk)
            s_ref[kk] = sk
            term = sk * row(r_ref, kk)
            parts[kk % 4] = term if parts[kk % 4] is None else parts[kk % 4] + term
        y_ref[t] = (parts[0] + parts[1]) + (parts[2] + parts[3])
        return carry

    lax.fori_loop(0, steps, step, 0)


def _wkv_call(r, dec, k, v, aa, bb, s0, steps):
    t_len, hd, lanes = r.shape
    seq = pl.BlockSpec((steps, hd, LANES), lambda l, t: (t, 0, l))
    st = pl.BlockSpec((hd, hd, LANES), lambda l, t: (0, 0, l))
    return pl.pallas_call(
        functools.partial(_wkv_kernel, steps=steps),
        grid=(lanes // LANES, t_len // steps),
        in_specs=[seq] * 6 + [st],
        out_specs=[seq, st],
        out_shape=[jax.ShapeDtypeStruct((t_len, hd, lanes), F32),
                   jax.ShapeDtypeStruct((hd, hd, lanes), F32)],
        compiler_params=_cparams(("parallel", "arbitrary")), name="wkv_scan")(r, dec, k, v, aa, bb, s0)


def _to_scan(x, b, t):
    heads = x.shape[-1] // HEAD_DIM
    return x.reshape(b, t, heads, HEAD_DIM).transpose(1, 3, 0, 2).reshape(t, HEAD_DIM, b * heads)


def _from_scan(y, b, t):
    heads = y.shape[-1] // b
    return y.reshape(t, HEAD_DIM, b, heads).transpose(2, 0, 3, 1).reshape(b * t, heads * HEAD_DIM)


def _wkv_group(rows, s0, b, t, steps):
    heads = s0.shape[1]
    s0_l = s0.transpose(3, 2, 0, 1).reshape(HEAD_DIM, HEAD_DIM, b * heads)
    y, s_l = _wkv_call(*[_to_scan(x, b, t) for x in rows], s0_l, steps)
    s_out = s_l.reshape(HEAD_DIM, HEAD_DIM, b, heads).transpose(2, 3, 1, 0)
    return _from_scan(y, b, t), s_out


def _rwkv_out_kernel(y_ref, r_ref, k_ref, v_ref, g_ref, x_ref, lng_ref, lnb_ref, rk_ref, wo_ref,
                     bd_ref, o_ref):
    bd = bd_ref[...]
    y = y_ref[...]
    inv_n = 1.0 / HEAD_DIM
    mu = _head_sum(y, bd) * inv_n
    yc = y - mu
    var = _head_sum(yc * yc, bd) * inv_n
    yn = yc * lax.rsqrt(var + GN_EPS) * lng_ref[...] + lnb_ref[...]
    bonus = _head_sum(r_ref[...] * k_ref[...] * rk_ref[...], bd) * v_ref[...]
    o_ref[...] = x_ref[...] + _bdot((yn + bonus) * g_ref[...], wo_ref[...])


def _rwkv_out_call(y, r, k, v, g, x, ln_g, ln_b, r_k, w_o, blockdiag):
    n, d = y.shape
    tm = _row_tile(n)
    row = pl.BlockSpec((tm, d), lambda i: (i, 0))
    vec = pl.BlockSpec((1, d), lambda i: (0, 0))
    return pl.pallas_call(
        _rwkv_out_kernel, grid=(pl.cdiv(n, tm),),
        in_specs=[row] * 6 + [vec, vec, vec, pl.BlockSpec((d, d), lambda i: (0, 0)),
                              pl.BlockSpec((LANES, LANES), lambda i: (0, 0))],
        out_specs=row, out_shape=jax.ShapeDtypeStruct((n, d), F32),
        compiler_params=_cparams(("parallel",)), name="rwkv_out")(
            y, r, k, v, g, x, ln_g.reshape(1, d), ln_b.reshape(1, d), r_k.reshape(1, d), w_o, blockdiag)


def _peer_scores_kernel(x_ref, g_ref, wq_ref, keys_ref, s_ref, ht_ref):
    h = _rms(x_ref[...], g_ref[...])
    ht_ref[...] = h.T.astype(BF16)
    q = _bdot(h, wq_ref[...]).astype(BF16)
    for hc in range(keys_ref.shape[0]):
        s_ref[hc] = lax.dot_general(keys_ref[hc], q[:, hc * LANES:(hc + 1) * LANES],
                                    (((1,), (1,)), ((), ())), preferred_element_type=F32)


def _peer_scores_call(x, g, wq, keys):
    n, d = x.shape
    tm = _row_tile(n)
    nhc = keys.shape[0]
    return pl.pallas_call(
        _peer_scores_kernel, grid=(pl.cdiv(n, tm),),
        in_specs=[pl.BlockSpec((tm, d), lambda i: (i, 0)), pl.BlockSpec((1, d), lambda i: (0, 0)),
                  pl.BlockSpec(wq.shape, lambda i: (0, 0)),
                  pl.BlockSpec(keys.shape, lambda i: (0, 0, 0))],
        out_specs=[pl.BlockSpec((nhc, PEER_NKEYS, tm), lambda i: (0, 0, i)),
                   pl.BlockSpec((d, tm), lambda i: (0, i))],
        out_shape=[jax.ShapeDtypeStruct((nhc, PEER_NKEYS, n), F32),
                   jax.ShapeDtypeStruct((d, n), BF16)],
        compiler_params=_cparams(("parallel",)), name="peer_scores")(x, g.reshape(1, d), wq, keys)


def _top16(s, iota):
    rank = jnp.full(s.shape, float(PEER_TOPK), F32)
    vals = []
    for a in range(PEER_TOPK):
        m = jnp.max(s, axis=0, keepdims=True)
        idx = jnp.min(jnp.where(s == m, iota, float(PEER_NKEYS)), axis=0, keepdims=True)
        hit = iota == idx
        rank = jnp.where(hit, float(a), rank)
        s = jnp.where(hit, NEG_INF, s)
        vals.append(m)
    return rank, vals


def _dup_mask_bits(x):
    bits = lax.bitcast_convert_type(x.astype(MASK_DTYPE).astype(F32), jnp.uint32)
    return bits | (bits >> 16)


def _peer_select_kernel(s_ref, l_ref, p1_ref, r2_ref, p2_ref):
    lanes = s_ref.shape[-1]
    iota = lax.broadcasted_iota(jnp.int32, (PEER_NKEYS, lanes), 0).astype(F32)
    iota16 = lax.broadcasted_iota(jnp.int32, (PEER_TOPK, lanes), 0).astype(F32)
    for h in range(PEER_HEADS):
        s1 = s_ref[2 * h]
        s2 = s_ref[2 * h + 1]
        rank1, v1 = _top16(s1, iota)
        rank2, v2 = _top16(s2, iota)
        v1a = jnp.concatenate(v1, axis=0)
        v2a = jnp.concatenate(v2, axis=0)
        top = v1[0] + v2[0]
        taken = jnp.zeros((PEER_TOPK, lanes), F32)
        front = v1a + v2[0]
        zsum = jnp.zeros((1, lanes), F32)
        for _ in range(PEER_TOPK):
            m = jnp.max(front, axis=0, keepdims=True)
            a_star = jnp.min(jnp.where(front == m, iota16, float(PEER_TOPK)), axis=0, keepdims=True)
            hit = iota16 == a_star
            zsum = zsum + jnp.exp(m - top)
            taken = jnp.where(hit, taken + 1.0, taken)
            cnt = jnp.max(jnp.where(hit, taken, -1.0), axis=0, keepdims=True)
            nxt = jnp.max(jnp.where(iota16 == cnt, v2a, NEG_INF), axis=0, keepdims=True)
            front = jnp.where(hit, v1a + nxt, front)
        lim = jnp.zeros((PEER_NKEYS, lanes), F32)
        for a in range(PEER_TOPK):
            lim = jnp.where(rank1 == float(a), taken[a:a + 1, :], lim)
        l_ref[h] = _dup_mask_bits(lim)
        p1_ref[h] = _dup_mask_bits(jnp.exp(s1 - v1[0]))
        p2 = jnp.exp(s2 - v2[0]) / zsum
        for r in range(PEER_NKEYS // MASK_ROWS):
            rows = slice(r * MASK_ROWS, (r + 1) * MASK_ROWS)
            r2_ref[h, r] = rank2[rows].astype(MASK_DTYPE)
            p2_ref[h, r] = p2[rows].astype(MASK_DTYPE)


def _peer_select_call(scores):
    nhc, nk, n = scores.shape
    tl = LANES
    groups = nk // MASK_ROWS
    words = pl.BlockSpec((PEER_HEADS, nk, tl), lambda i: (0, 0, i))
    packed = pl.BlockSpec((PEER_HEADS, groups, MASK_ROWS, tl), lambda i: (0, 0, 0, i))
    return pl.pallas_call(
        _peer_select_kernel, grid=(pl.cdiv(n, tl),),
        in_specs=[pl.BlockSpec((nhc, nk, tl), lambda i: (0, 0, i))],
        out_specs=[words, words, packed, packed],
        out_shape=[jax.ShapeDtypeStruct((PEER_HEADS, nk, n), jnp.uint32)] * 2
        + [jax.ShapeDtypeStruct((PEER_HEADS, groups, MASK_ROWS, n), MASK_DTYPE)] * 2,
        compiler_params=_cparams(("parallel",)), name="peer_select")(scores)


def _gate_weights(cb, ii, heads, w, l_ref, p1_ref, r2_ref, p2_ref):
    blocks = PEER_CHUNK // PEER_NKEYS
    base = pl.multiple_of(cb * blocks, blocks)
    w = list(w)
    for h in heads:
        lw = l_ref[h, pl.ds(base, blocks), :]
        pw = p1_ref[h, pl.ds(base, blocks), :]
        lim = pltpu.bitcast(jnp.broadcast_to(lw[ii:ii + 1, :], lw.shape), MASK_DTYPE)
        p1 = pltpu.bitcast(jnp.broadcast_to(pw[ii:ii + 1, :], pw.shape), MASK_DTYPE)
        for r in range(len(w)):
            term = jnp.where(r2_ref[h, r] < lim, p2_ref[h, r], jnp.zeros_like(p1)) * p1
            w[r] = term if w[r] is None else w[r] + term
    return w


def _gate_store(ii, w, z_ref, g_ref):
    for r in range(len(w)):
        rows = slice(ii * PEER_NKEYS + r * MASK_ROWS, ii * PEER_NKEYS + (r + 1) * MASK_ROWS)
        z = z_ref[rows, :]
        act = 0.5 * z * (1.0 + lax.erf(z * np.float32(np.sqrt(0.5))))
        g_ref[rows, :] = (act.astype(MASK_DTYPE) * w[r]).astype(g_ref.dtype)


def _peer_dense_kernel(ht_ref, u_ref, vt_ref, l_ref, p1_ref, r2_ref, p2_ref, o_ref,
                       z0_ref, z1_ref, g0_ref, g1_ref, acc_ref, *, items, chunks):
    s = pl.program_id(0)

    @pl.when(s == 0)
    def _():
        for ref in (z0_ref, z1_ref, g0_ref, g1_ref):
            ref[...] = jnp.zeros_like(ref)

    gate_chunk = jnp.clip(s - 1, 0, items - 1) % chunks

    def stages(z_new, z_old, g_new, g_old):
        blocks = PEER_CHUNK // PEER_NKEYS
        drows = o_ref.shape[0] // blocks
        sel = (l_ref, p1_ref, r2_ref, p2_ref)
        tt = o_ref.shape[1]
        nsplit = 2 if tt % (2 * LANES) == 0 else 1
        ncols = [slice(c * (tt // nsplit), (c + 1) * (tt // nsplit)) for c in range(nsplit)]
        hq = PEER_HEADS // (2 * nsplit)
        for ii in range(blocks):
            er = slice(ii * PEER_NKEYS, (ii + 1) * PEER_NKEYS)
            dr = slice(ii * drows, (ii + 1) * drows)
            w = [None] * (PEER_NKEYS // MASK_ROWS)
            q = 0
            for nc in ncols:
                w = _gate_weights(gate_chunk, ii, range(q * hq, (q + 1) * hq), w, *sel)
                q += 1
                z_new[er, nc] = jnp.dot(u_ref[er, :], ht_ref[:, nc], preferred_element_type=F32)
            for c, nc in enumerate(ncols):
                w = _gate_weights(gate_chunk, ii, range(q * hq, (q + 1) * hq), w, *sel)
                q += 1
                if c == nsplit - 1:
                    _gate_store(ii, w, z_old, g_new)
                acc_ref[dr, nc] = jnp.dot(vt_ref[dr, :], g_old[:, nc], preferred_element_type=F32)

    @pl.when(s % 2 == 0)
    def _():
        stages(z0_ref, z1_ref, g1_ref, g0_ref)

    @pl.when(s % 2 == 1)
    def _():
        stages(z1_ref, z0_ref, g0_ref, g1_ref)

    first = jnp.clip(s - 2, 0, items - 1) % chunks == 0

    @pl.when(first)
    def _():
        o_ref[...] = acc_ref[...]

    @pl.when(jnp.logical_not(first))
    def _():
        o_ref[...] += acc_ref[...]


def _peer_dense_call(ht, u, vt, lim, p1, r2, p2):
    d, n = ht.shape
    chunks = u.shape[0] // PEER_CHUNK
    tt = min(PEER_TOKENS, n)
    items = pl.cdiv(n, tt) * chunks

    def item(s, lag):
        return jnp.clip(s - lag, 0, items - 1)

    words = pl.BlockSpec((PEER_HEADS, PEER_NKEYS, tt), lambda s: (0, 0, item(s, 1) // chunks))
    packed = pl.BlockSpec(r2.shape[:3] + (tt,), lambda s: (0, 0, 0, item(s, 1) // chunks))
    return pl.pallas_call(
        functools.partial(_peer_dense_kernel, items=items, chunks=chunks),
        grid=(items + 2,),
        in_specs=[pl.BlockSpec((d, tt), lambda s: (0, item(s, 0) // chunks)),
                  pl.BlockSpec((PEER_CHUNK, d), lambda s: (item(s, 0) % chunks, 0)),
                  pl.BlockSpec((d, PEER_CHUNK), lambda s: (0, item(s, 2) % chunks)),
                  words, words, packed, packed],
        out_specs=pl.BlockSpec((d, tt), lambda s: (0, item(s, 2) // chunks)),
        out_shape=jax.ShapeDtypeStruct((d, n), F32),
        scratch_shapes=[pltpu.VMEM((PEER_CHUNK, tt), F32)] * 2 + [pltpu.VMEM((PEER_CHUNK, tt), BF16)] * 2
        + [pltpu.VMEM((d, tt), F32)],
        compiler_params=_cparams(("arbitrary",)), name="peer_dense")(ht, u, vt, lim, p1, r2, p2)


def _peer_params(g, wq, keys, u, v):
    nh, _, nk, half = keys.shape
    return g, wq.astype(BF16), keys.reshape(nh * 2, nk, half).astype(BF16), u.astype(BF16), v.astype(BF16).T


def _peer(x, params):
    g, wq, keys, u, vt = params
    scores, ht = _peer_scores_call(x, g, wq, keys)
    lim, p1, r2, p2 = _peer_select_call(scores)
    return _peer_dense_call(ht, u, vt, lim, p1, r2, p2)


def _window_sums(ext, halo):
    d = ext.shape[-1]
    gd = d // len(POOL_WINDOWS)
    outs = []
    for gi, w in enumerate(POOL_WINDOWS):
        acc = ext[:, gi * gd:(gi + 1) * gd]
        shift = 1
        while shift < w:
            acc = acc + pltpu.roll(acc, shift, 0)
            shift *= 2
        outs.append(acc[halo:, :])
    return outs


def _pool_prompt_kernel(prev_ref, cur_ref, x_ref, w_ref, sc_ref, o_ref):
    tm = cur_ref.shape[1]
    cur = cur_ref[0]
    ext = jnp.concatenate([prev_ref[0, tm - POOL_HALO:, :], cur], axis=0)
    sums = _window_sums(ext, POOL_HALO)
    pos = (pl.program_id(1) * tm + lax.broadcasted_iota(jnp.int32, (tm, 1), 0)).astype(F32)
    gd = cur.shape[-1] // len(POOL_WINDOWS)
    outs = []
    for gi, w in enumerate(POOL_WINDOWS):
        cnt = jnp.minimum(float(w), pos + 1.0)
        mixed = sums[gi] / cnt - cur[:, gi * gd:(gi + 1) * gd]
        outs.append(_bdot(mixed, w_ref[gi]))
    o_ref[0] = x_ref[0] + jnp.concatenate(outs, axis=-1) * sc_ref[...]


def _pool_prompt_call(h, x, pool_w, pool_scale, tm):
    b, t, d = h.shape
    hp = jnp.concatenate([jnp.zeros((b, tm, d), F32), h], axis=1)
    ng = pool_w.shape[0]
    return pl.pallas_call(
        _pool_prompt_kernel, grid=(b, t // tm),
        in_specs=[pl.BlockSpec((1, tm, d), lambda i, j: (i, j, 0)),
                  pl.BlockSpec((1, tm, d), lambda i, j: (i, j + 1, 0)),
                  pl.BlockSpec((1, tm, d), lambda i, j: (i, j, 0)),
                  pl.BlockSpec(pool_w.shape, lambda i, j: (0, 0, 0)),
                  pl.BlockSpec((1, d), lambda i, j: (0, 0))],
        out_specs=pl.BlockSpec((1, tm, d), lambda i, j: (i, j, 0)),
        out_shape=jax.ShapeDtypeStruct((b, t, d), F32),
        compiler_params=_cparams(("parallel", "parallel")), name="pool_prompt")(
            hp, hp, x, pool_w, pool_scale.reshape(1, d))


def _pool_sample_kernel(hist_ref, h_ref, x_ref, w_ref, sc_ref, o_ref):
    h = h_ref[...]
    nhist = hist_ref.shape[0]
    gd = h.shape[-1] // len(POOL_WINDOWS)
    outs = []
    for gi, w in enumerate(POOL_WINDOWS):
        sl = slice(gi * gd, (gi + 1) * gd)
        acc = h[:, sl]
        for back in range(1, w):
            acc = acc + hist_ref[nhist - back][:, sl]
        mixed = acc / float(w) - h[:, sl]
        outs.append(_bdot(mixed, w_ref[gi]))
    o_ref[...] = x_ref[...] + jnp.concatenate(outs, axis=-1) * sc_ref[...]


def _pool_sample_call(hist, h, x, pool_w, pool_scale):
    b, d = h.shape
    full2 = pl.BlockSpec((b, d), lambda i: (0, 0))
    return pl.pallas_call(
        _pool_sample_kernel, grid=(1,),
        in_specs=[pl.BlockSpec(hist.shape, lambda i: (0, 0, 0)), full2, full2,
                  pl.BlockSpec(pool_w.shape, lambda i: (0, 0, 0)),
                  pl.BlockSpec((1, d), lambda i: (0, 0))],
        out_specs=full2, out_shape=jax.ShapeDtypeStruct((b, d), F32),
        compiler_params=_cparams(("arbitrary",)), name="pool_sample")(
            hist, h, x, pool_w, pool_scale.reshape(1, d))


def _time_block(t, pref):
    for c in range(min(pref, t), 0, -1):
        if t % c == 0 and (c % 8 == 0 or c == t):
            return c
    return t


def kernel(x_prompt, x_sample, state_wkv, state_shift, state_pool, meta_tokens, norm_mix, norm_ffn, norm_final, rwkv_mix, rwkv_w_rkv, rwkv_decay_w1, rwkv_decay_w2, rwkv_decay_b, rwkv_iclr_w1, rwkv_iclr_w2, rwkv_iclr_b, rwkv_gate_w1, rwkv_gate_w2, rwkv_k_k, rwkv_k_a, rwkv_r_k, rwkv_ln_g, rwkv_ln_b, rwkv_w_o, pool_w, pool_scale, peer_wq, peer_keys, peer_u, peer_v):
    bp, seq, d = x_prompt.shape
    bs = x_sample.shape[0]
    tp = N_META + seq
    n_p = bp * tp
    heads = d // HEAD_DIM
    assert x_sample.shape[1] == 1 and (bp * heads) % LANES == 0 and (bs * heads) % LANES == 0
    assert rwkv_w_rkv.shape[0] == 1 and pool_w.shape[0] == 1 and norm_mix.shape[0] == 2

    blockdiag = jnp.asarray(np.kron(np.eye(LANES // HEAD_DIM), np.ones((HEAD_DIM, HEAD_DIM))), BF16)
    proj_params = _rwkv_proj_params(
        rwkv_mix[0], rwkv_w_rkv[0], rwkv_decay_w1[0], rwkv_decay_w2[0], rwkv_decay_b[0], rwkv_iclr_w1[0],
        rwkv_iclr_w2[0], rwkv_iclr_b[0], rwkv_gate_w1[0], rwkv_gate_w2[0], rwkv_k_k[0], rwkv_k_a[0], blockdiag)
    w_o = rwkv_w_o[0].astype(BF16)
    peer_params = [_peer_params(norm_ffn[i], peer_wq[i], peer_keys[i], peer_u[i], peer_v[i]) for i in range(2)]
    pool_wb = pool_w[0].astype(BF16)
    meta = jnp.broadcast_to(meta_tokens[None], (bp, N_META, d))
    x0p = jnp.concatenate([meta, x_prompt], axis=1).reshape(n_p, d)
    x0s = x_sample.reshape(bs, d)

    def rwkv_layer(x0, prev_of, s0, b, t, steps):
        h0 = _norm_call(x0, norm_mix[0])
        r, k, v, dec, aa, bb, g = _rwkv_proj_call(h0, prev_of(h0), proj_params)
        y, s_out = _wkv_group((r, dec, k, v, aa, bb), s0, b, t, steps)
        x1 = _rwkv_out_call(y, r, k, v, g, x0, rwkv_ln_g[0], rwkv_ln_b[0], rwkv_r_k[0], w_o, blockdiag)
        return h0, s_out, x1

    def prev_prompt(h0):
        h3 = h0.reshape(bp, tp, d)
        return jnp.concatenate([jnp.zeros((bp, 1, d), F32), h3[:, :-1]], axis=1).reshape(n_p, d)

    h0p, wkv_p, x1p = rwkv_layer(x0p, prev_prompt, jnp.zeros((bp, heads, HEAD_DIM, HEAD_DIM), F32),
                                 bp, tp, _time_block(tp, 48))
    h0s, wkv_s, x1s = rwkv_layer(x0s, lambda h0: state_shift[0], state_wkv[0], bs, 1, 1)
    f0p = _peer(x1p, peer_params[0])
    f0s = _peer(x1s, peer_params[0])

    x2p, h1p = _add_norm_call(x1p, f0p, norm_mix[1])
    x2s, h1s = _add_norm_call(x1s, f0s, norm_mix[1])
    h1p = h1p.reshape(bp, tp, d)
    x3p = _pool_prompt_call(h1p, x2p.reshape(bp, tp, d), pool_wb, pool_scale[0],
                            _time_block(tp, 344)).reshape(n_p, d)
    x3s = _pool_sample_call(state_pool[0].transpose(1, 0, 2), h1s, x2s, pool_wb, pool_scale[0])
    f1p = _peer(x3p, peer_params[1])
    f1s = _peer(x3s, peer_params[1])
    _, yfp = _add_norm_call(x3p, f1p, norm_final)
    _, yfs = _add_norm_call(x3s, f1s, norm_final)

    nbuf = state_pool.shape[2]
    y_prompt = yfp.reshape(bp, tp, d)[:, N_META:]
    y_sample = yfs.reshape(bs, 1, d)
    shift_p = h0p.reshape(bp, tp, d)[:, -1][None]
    shift_s = h0s[None]
    pool_p = jnp.concatenate([jnp.zeros((bp, nbuf, d), F32), h1p], axis=1)[:, tp:][None]
    pool_s = jnp.concatenate([state_pool[0][:, 1:], h1s[:, None, :]], axis=1)[None]
    return (y_prompt, y_sample, wkv_p[None], shift_p, pool_p, wkv_s[None], shift_s, pool_s)
```

```python
import functools

import jax
import jax.numpy as jnp
import numpy as np
from jax import lax
from jax.experimental import pallas as pl
from jax.experimental.pallas import tpu as pltpu

F32 = jnp.float32
BF16 = jnp.bfloat16

N_META = 16
RMS_EPS = 1e-6
GN_EPS = 64e-5
HEAD_DIM = 64
LANES = 128
POOL_WINDOWS = (2, 4, 8, 16)
POOL_HALO = 16
PEER_HEADS = 8
PEER_NKEYS = 128
PEER_TOPK = 16
PEER_CHUNK = 1024
PEER_TOKENS = 512
MASK_DTYPE = jnp.bfloat16
MASK_ROWS = 16
ROW_TILE = 256
VMEM_LIMIT = 56 * 1024 * 1024

NEG_INF = float("-inf")


def _cparams(sem):
    return pltpu.CompilerParams(dimension_semantics=sem, vmem_limit_bytes=VMEM_LIMIT)


def _row_tile(n):
    return min(ROW_TILE, n)


def _rms(x, g):
    return x * lax.rsqrt(jnp.mean(x * x, axis=-1, keepdims=True) + RMS_EPS) * g


def _bdot(a, b):
    return jnp.dot(a.astype(BF16), b.astype(BF16), preferred_element_type=F32)


def _head_sum(x, blockdiag):
    outs = []
    for blk in range(x.shape[-1] // LANES):
        xb = x[:, blk * LANES:(blk + 1) * LANES]
        hi = xb.astype(BF16)
        lo = (xb - hi.astype(F32)).astype(BF16)
        outs.append(jnp.dot(hi, blockdiag, preferred_element_type=F32)
                    + jnp.dot(lo, blockdiag, preferred_element_type=F32))
    return jnp.concatenate(outs, axis=-1)


def _norm_kernel(x_ref, g_ref, h_ref):
    h_ref[...] = _rms(x_ref[...], g_ref[...])


def _norm_call(x, g):
    n, d = x.shape
    tm = _row_tile(n)
    row = pl.BlockSpec((tm, d), lambda i: (i, 0))
    return pl.pallas_call(
        _norm_kernel, grid=(pl.cdiv(n, tm),),
        in_specs=[row, pl.BlockSpec((1, d), lambda i: (0, 0))],
        out_specs=row, out_shape=jax.ShapeDtypeStruct((n, d), F32),
        compiler_params=_cparams(("parallel",)), name="rmsnorm")(x, g.reshape(1, d))


def _add_norm_kernel(a_ref, bt_ref, g_ref, x_ref, h_ref):
    x = a_ref[...] + bt_ref[...].T
    x_ref[...] = x
    h_ref[...] = _rms(x, g_ref[...])


def _add_norm_call(a, bt, g):
    n, d = a.shape
    tm = _row_tile(n)
    row = pl.BlockSpec((tm, d), lambda i: (i, 0))
    return pl.pallas_call(
        _add_norm_kernel, grid=(pl.cdiv(n, tm),),
        in_specs=[row, pl.BlockSpec((d, tm), lambda i: (0, i)), pl.BlockSpec((1, d), lambda i: (0, 0))],
        out_specs=[row, row],
        out_shape=[jax.ShapeDtypeStruct((n, d), F32)] * 2,
        compiler_params=_cparams(("parallel",)), name="add_rmsnorm")(a, bt, g.reshape(1, d))


def _rwkv_proj_kernel(h_ref, prev_ref, mix_ref, wrkv_ref, dw1_ref, dw2_ref, db_ref,
                      aw1_ref, aw2_ref, ab_ref, gw1_ref, gw2_ref, kk_ref, ka_ref, bd_ref,
                      r_ref, k_ref, v_ref, dec_ref, aa_ref, bb_ref, g_ref):
    h = h_ref[...]
    xx = prev_ref[...] - h

    def xs(c):
        return h + xx * mix_ref[c:c + 1, :]

    r = _bdot(xs(0), wrkv_ref[0])
    k = _bdot(xs(1), wrkv_ref[1])
    v = _bdot(xs(2), wrkv_ref[2])
    w = -jax.nn.softplus(-(db_ref[...] + _bdot(jnp.tanh(_bdot(xs(3), dw1_ref[...])), dw2_ref[...]))) - 0.5
    a = jax.nn.sigmoid(ab_ref[...] + _bdot(_bdot(xs(4), aw1_ref[...]), aw2_ref[...]))
    g = _bdot(jax.nn.sigmoid(_bdot(xs(5), gw1_ref[...])), gw2_ref[...])
    kk = k * kk_ref[...]
    nrm = jnp.sqrt(_head_sum(kk * kk, bd_ref[...]))
    kk = kk / jnp.maximum(nrm, 1e-12)
    r_ref[...] = r
    k_ref[...] = k * (1.0 + (a - 1.0) * ka_ref[...])
    v_ref[...] = v
    dec_ref[...] = jnp.exp(-jnp.exp(w))
    aa_ref[...] = -kk
    bb_ref[...] = kk * a
    g_ref[...] = g


def _rwkv_proj_params(mix, w_rkv, dw1, dw2, db, aw1, aw2, ab, gw1, gw2, k_k, k_a, blockdiag):
    d = mix.shape[-1]
    return [mix, w_rkv.astype(BF16), dw1.astype(BF16), dw2.astype(BF16), db.reshape(1, d),
            aw1.astype(BF16), aw2.astype(BF16), ab.reshape(1, d), gw1.astype(BF16),
            gw2.astype(BF16), k_k.reshape(1, d), k_a.reshape(1, d), blockdiag]


def _rwkv_proj_call(h, prev, params):
    n, d = h.shape
    tm = _row_tile(n)
    row = pl.BlockSpec((tm, d), lambda i: (i, 0))

    def full(x):
        nd = x.ndim
        return pl.BlockSpec(x.shape, lambda i: (0,) * nd)

    return pl.pallas_call(
        _rwkv_proj_kernel, grid=(pl.cdiv(n, tm),),
        in_specs=[row, row] + [full(p) for p in params],
        out_specs=[row] * 7,
        out_shape=[jax.ShapeDtypeStruct((n, d), F32)] * 7,
        compiler_params=_cparams(("parallel",)), name="rwkv_proj")(h, prev, *params)


def _wkv_kernel(r_ref, d_ref, k_ref, v_ref, a_ref, b_ref, s0_ref, y_ref, s_ref, *, steps):
    nk = s_ref.shape[0]

    @pl.when(pl.program_id(1) == 0)
    def _():
        s_ref[...] = s0_ref[...]

    def step(t, carry):
        def row(ref, kk):
            return ref[t, kk:kk + 1, :]

        parts = [None] * 4
        for kk in range(nk):
            term = s_ref[kk] * row(a_ref, kk)
            parts[kk % 4] = term if parts[kk % 4] is None else parts[kk % 4] + term
        sa = (parts[0] + parts[1]) + (parts[2] + parts[3])
        vv = v_ref[t]
        parts = [None] * 4
        for kk in range(nk):
            sk = s_ref[kk] * row(d_ref, kk) + sa * row(b_ref, kk) + vv * row(k_ref, kk)
            s_ref[kk] = sk
            term = sk * row(r_ref, kk)
            parts[kk % 4] = term if parts[kk % 4] is None else parts[kk % 4] + term
        y_ref[t] = (parts[0] + parts[1]) + (parts[2] + parts[3])
        return carry

    lax.fori_loop(0, steps, step, 0)


def _wkv_call(r, dec, k, v, aa, bb, s0, steps):
    t_len, hd, lanes = r.shape
    seq = pl.BlockSpec((steps, hd, LANES), lambda l, t: (t, 0, l))
    st = pl.BlockSpec((hd, hd, LANES), lambda l, t: (0, 0, l))
    return pl.pallas_call(
        functools.partial(_wkv_kernel, steps=steps),
        grid=(lanes // LANES, t_len // steps),
        in_specs=[seq] * 6 + [st],
        out_specs=[seq, st],
        out_shape=[jax.ShapeDtypeStruct((t_len, hd, lanes), F32),
                   jax.ShapeDtypeStruct((hd, hd, lanes), F32)],
        compiler_params=_cparams(("parallel", "arbitrary")), name="wkv_scan")(r, dec, k, v, aa, bb, s0)


def _to_scan(x, b, t):
    heads = x.shape[-1] // HEAD_DIM
    return x.reshape(b, t, heads, HEAD_DIM).transpose(1, 3, 0, 2).reshape(t, HEAD_DIM, b * heads)


def _from_scan(y, b, t):
    heads = y.shape[-1] // b
    return y.reshape(t, HEAD_DIM, b, heads).transpose(2, 0, 3, 1).reshape(b * t, heads * HEAD_DIM)


def _wkv_group(rows, s0, b, t, steps):
    heads = s0.shape[1]
    s0_l = s0.transpose(3, 2, 0, 1).reshape(HEAD_DIM, HEAD_DIM, b * heads)
    y, s_l = _wkv_call(*[_to_scan(x, b, t) for x in rows], s0_l, steps)
    s_out = s_l.reshape(HEAD_DIM, HEAD_DIM, b, heads).transpose(2, 3, 1, 0)
    return _from_scan(y, b, t), s_out


def _rwkv_out_kernel(y_ref, r_ref, k_ref, v_ref, g_ref, x_ref, lng_ref, lnb_ref, rk_ref, wo_ref,
                     bd_ref, o_ref):
    bd = bd_ref[...]
    y = y_ref[...]
    inv_n = 1.0 / HEAD_DIM
    mu = _head_sum(y, bd) * inv_n
    yc = y - mu
    var = _head_sum(yc * yc, bd) * inv_n
    yn = yc * lax.rsqrt(var + GN_EPS) * lng_ref[...] + lnb_ref[...]
    bonus = _head_sum(r_ref[...] * k_ref[...] * rk_ref[...], bd) * v_ref[...]
    o_ref[...] = x_ref[...] + _bdot((yn + bonus) * g_ref[...], wo_ref[...])


def _rwkv_out_call(y, r, k, v, g, x, ln_g, ln_b, r_k, w_o, blockdiag):
    n, d = y.shape
    tm = _row_tile(n)
    row = pl.BlockSpec((tm, d), lambda i: (i, 0))
    vec = pl.BlockSpec((1, d), lambda i: (0, 0))
    return pl.pallas_call(
        _rwkv_out_kernel, grid=(pl.cdiv(n, tm),),
        in_specs=[row] * 6 + [vec, vec, vec, pl.BlockSpec((d, d), lambda i: (0, 0)),
                              pl.BlockSpec((LANES, LANES), lambda i: (0, 0))],
        out_specs=row, out_shape=jax.ShapeDtypeStruct((n, d), F32),
        compiler_params=_cparams(("parallel",)), name="rwkv_out")(
            y, r, k, v, g, x, ln_g.reshape(1, d), ln_b.reshape(1, d), r_k.reshape(1, d), w_o, blockdiag)


def _peer_scores_kernel(x_ref, g_ref, wq_ref, keys_ref, s_ref, ht_ref):
    h = _rms(x_ref[...], g_ref[...])
    ht_ref[...] = h.T.astype(BF16)
    q = _bdot(h, wq_ref[...]).astype(BF16)
    for hc in range(keys_ref.shape[0]):
        s_ref[hc] = lax.dot_general(keys_ref[hc], q[:, hc * LANES:(hc + 1) * LANES],
                                    (((1,), (1,)), ((), ())), preferred_element_type=F32)


def _peer_scores_call(x, g, wq, keys):
    n, d = x.shape
    tm = _row_tile(n)
    nhc = keys.shape[0]
    return pl.pallas_call(
        _peer_scores_kernel, grid=(pl.cdiv(n, tm),),
        in_specs=[pl.BlockSpec((tm, d), lambda i: (i, 0)), pl.BlockSpec((1, d), lambda i: (0, 0)),
                  pl.BlockSpec(wq.shape, lambda i: (0, 0)),
                  pl.BlockSpec(keys.shape, lambda i: (0, 0, 0))],
        out_specs=[pl.BlockSpec((nhc, PEER_NKEYS, tm), lambda i: (0, 0, i)),
                   pl.BlockSpec((d, tm), lambda i: (0, i))],
        out_shape=[jax.ShapeDtypeStruct((nhc, PEER_NKEYS, n), F32),
                   jax.ShapeDtypeStruct((d, n), BF16)],
        compiler_params=_cparams(("parallel",)), name="peer_scores")(x, g.reshape(1, d), wq, keys)


def _top16(s, iota, exact_ties):
    rank = jnp.full(s.shape, float(PEER_TOPK), F32)
    vals = []
    for a in range(PEER_TOPK):
        m = jnp.max(s, axis=0, keepdims=True)
        hit = s == m
        if exact_ties:
            idx = jnp.min(jnp.where(hit, iota, float(PEER_NKEYS)), axis=0, keepdims=True)
            hit = iota == idx
        rank = jnp.where(hit, float(a), rank)
        s = jnp.where(hit, NEG_INF, s)
        vals.append(m)
    ranked = jnp.sum(jnp.where(rank < float(PEER_TOPK), 1.0, 0.0), axis=0, keepdims=True)
    return rank, vals, ranked


def _dup_mask_bits(x):
    bits = lax.bitcast_convert_type(x.astype(MASK_DTYPE).astype(F32), jnp.uint32)
    return bits | (bits >> 16)


def _peer_select_kernel(s_ref, l_ref, p1_ref, r2_ref, p2_ref):
    miscount = _peer_select_body(s_ref, l_ref, p1_ref, r2_ref, p2_ref, exact_ties=False)

    @pl.when(jnp.max(miscount) > 0.0)
    def _():
        _peer_select_body(s_ref, l_ref, p1_ref, r2_ref, p2_ref, exact_ties=True)


def _peer_select_body(s_ref, l_ref, p1_ref, r2_ref, p2_ref, exact_ties):
    lanes = s_ref.shape[-1]
    iota = lax.broadcasted_iota(jnp.int32, (PEER_NKEYS, lanes), 0).astype(F32)
    iota16 = lax.broadcasted_iota(jnp.int32, (PEER_TOPK, lanes), 0).astype(F32)
    miscount = jnp.zeros((1, lanes), F32)
    for h in range(PEER_HEADS):
        s1 = s_ref[2 * h]
        s2 = s_ref[2 * h + 1]
        rank1, v1, n1 = _top16(s1, iota, exact_ties)
        rank2, v2, n2 = _top16(s2, iota, exact_ties)
        miscount = miscount + jnp.abs(n1 - float(PEER_TOPK)) + jnp.abs(n2 - float(PEER_TOPK))
        v1a = jnp.concatenate(v1, axis=0)
        v2a = jnp.concatenate(v2, axis=0)
        top = v1[0] + v2[0]
        taken = jnp.zeros((PEER_TOPK, lanes), F32)
        front = v1a + v2[0]
        zsum = jnp.zeros((1, lanes), F32)
        for _ in range(PEER_TOPK):
            m = jnp.max(front, axis=0, keepdims=True)
            a_star = jnp.min(jnp.where(front == m, iota16, float(PEER_TOPK)), axis=0, keepdims=True)
            hit = iota16 == a_star
            zsum = zsum + jnp.exp(m - top)
            taken = jnp.where(hit, taken + 1.0, taken)
            cnt = jnp.max(jnp.where(hit, taken, -1.0), axis=0, keepdims=True)
            nxt = jnp.max(jnp.where(iota16 == cnt, v2a, NEG_INF), axis=0, keepdims=True)
            front = jnp.where(hit, v1a + nxt, front)
        lim = jnp.zeros((PEER_NKEYS, lanes), F32)
        for a in range(PEER_TOPK):
            lim = jnp.where(rank1 == float(a), taken[a:a + 1, :], lim)
        l_ref[h] = _dup_mask_bits(lim)
        p1_ref[h] = _dup_mask_bits(jnp.exp(s1 - v1[0]))
        p2 = jnp.exp(s2 - v2[0]) / zsum
        for r in range(PEER_NKEYS // MASK_ROWS):
            rows = slice(r * MASK_ROWS, (r + 1) * MASK_ROWS)
            r2_ref[h, r] = rank2[rows].astype(MASK_DTYPE)
            p2_ref[h, r] = p2[rows].astype(MASK_DTYPE)
    return miscount


def _peer_select_call(scores):
    nhc, nk, n = scores.shape
    tl = LANES
    groups = nk // MASK_ROWS
    words = pl.BlockSpec((PEER_HEADS, nk, tl), lambda i: (0, 0, i))
    packed = pl.BlockSpec((PEER_HEADS, groups, MASK_ROWS, tl), lambda i: (0, 0, 0, i))
    return pl.pallas_call(
        _peer_select_kernel, grid=(pl.cdiv(n, tl),),
        in_specs=[pl.BlockSpec((nhc, nk, tl), lambda i: (0, 0, i))],
        out_specs=[words, words, packed, packed],
        out_shape=[jax.ShapeDtypeStruct((PEER_HEADS, nk, n), jnp.uint32)] * 2
        + [jax.ShapeDtypeStruct((PEER_HEADS, groups, MASK_ROWS, n), MASK_DTYPE)] * 2,
        compiler_params=_cparams(("parallel",)), name="peer_select")(scores)


def _gate_weights(cb, ii, heads, w, l_ref, p1_ref, r2_ref, p2_ref):
    blocks = PEER_CHUNK // PEER_NKEYS
    base = pl.multiple_of(cb * blocks, blocks)
    w = list(w)
    for h in heads:
        lw = l_ref[h, pl.ds(base, blocks), :]
        pw = p1_ref[h, pl.ds(base, blocks), :]
        lim = pltpu.bitcast(jnp.broadcast_to(lw[ii:ii + 1, :], lw.shape), MASK_DTYPE)
        p1 = pltpu.bitcast(jnp.broadcast_to(pw[ii:ii + 1, :], pw.shape), MASK_DTYPE)
        for r in range(len(w)):
            term = jnp.where(r2_ref[h, r] < lim, p2_ref[h, r], jnp.zeros_like(p1)) * p1
            w[r] = term if w[r] is None else w[r] + term
    return w


def _gate_store(ii, w, z_ref, g_ref):
    for r in range(len(w)):
        rows = slice(ii * PEER_NKEYS + r * MASK_ROWS, ii * PEER_NKEYS + (r + 1) * MASK_ROWS)
        z = z_ref[rows, :]
        act = 0.5 * z * (1.0 + lax.erf(z * np.float32(np.sqrt(0.5))))
        g_ref[rows, :] = (act.astype(MASK_DTYPE) * w[r]).astype(g_ref.dtype)


def _peer_dense_kernel(ht_ref, u_ref, vt_ref, l_ref, p1_ref, r2_ref, p2_ref, o_ref,
                       z0_ref, z1_ref, g0_ref, g1_ref, acc_ref, *, items, chunks, full_rows):
    s = pl.program_id(0)

    @pl.when(s == 0)
    def _():
        for ref in (z0_ref, z1_ref, g0_ref, g1_ref):
            ref[...] = jnp.zeros_like(ref)

    gate_chunk = jnp.clip(s - 1, 0, items - 1) % chunks

    def stages(z_new, z_old, g_new, g_old):
        blocks = PEER_CHUNK // PEER_NKEYS
        drows = o_ref.shape[0] // blocks
        sel = (l_ref, p1_ref, r2_ref, p2_ref)
        tt = o_ref.shape[1]
        nsplit = 2 if tt % (2 * LANES) == 0 else 1
        ncols = [slice(c * (tt // nsplit), (c + 1) * (tt // nsplit)) for c in range(nsplit)]
        hq = PEER_HEADS // (2 * nsplit)
        if full_rows:
            for ii in range(blocks):
                w = _gate_weights(gate_chunk, ii, range(PEER_HEADS), [None] * (PEER_NKEYS // MASK_ROWS), *sel)
                _gate_store(ii, w, z_old, g_new)
                rows = slice((ii // 2) * 2 * PEER_NKEYS, (ii // 2 + 1) * 2 * PEER_NKEYS)
                if ii % 2 == 0:
                    z_new[rows, :] = jnp.dot(u_ref[rows, :], ht_ref[...], preferred_element_type=F32)
                else:
                    acc_ref[rows, :] = jnp.dot(vt_ref[rows, :], g_old[...], preferred_element_type=F32)
            return
        for ii in range(blocks):
            er = slice(ii * PEER_NKEYS, (ii + 1) * PEER_NKEYS)
            dr = slice(ii * drows, (ii + 1) * drows)
            w = [None] * (PEER_NKEYS // MASK_ROWS)
            q = 0
            for nc in ncols:
                w = _gate_weights(gate_chunk, ii, range(q * hq, (q + 1) * hq), w, *sel)
                q += 1
                z_new[er, nc] = jnp.dot(u_ref[er, :], ht_ref[:, nc], preferred_element_type=F32)
            for c, nc in enumerate(ncols):
                w = _gate_weights(gate_chunk, ii, range(q * hq, (q + 1) * hq), w, *sel)
                q += 1
                if c == nsplit - 1:
                    _gate_store(ii, w, z_old, g_new)
                acc_ref[dr, nc] = jnp.dot(vt_ref[dr, :], g_old[:, nc], preferred_element_type=F32)

    @pl.when(s % 2 == 0)
    def _():
        stages(z0_ref, z1_ref, g1_ref, g0_ref)

    @pl.when(s % 2 == 1)
    def _():
        stages(z1_ref, z0_ref, g0_ref, g1_ref)

    first = jnp.clip(s - 2, 0, items - 1) % chunks == 0

    @pl.when(first)
    def _():
        o_ref[...] = acc_ref[...]

    @pl.when(jnp.logical_not(first))
    def _():
        o_ref[...] += acc_ref[...]


def _peer_dense_call(ht, u, vt, lim, p1, r2, p2, full_rows):
    d, n = ht.shape
    chunks = u.shape[0] // PEER_CHUNK
    tt = min(PEER_TOKENS, n)
    items = pl.cdiv(n, tt) * chunks

    def item(s, lag):
        return jnp.clip(s - lag, 0, items - 1)

    words = pl.BlockSpec((PEER_HEADS, PEER_NKEYS, tt), lambda s: (0, 0, item(s, 1) // chunks))
    packed = pl.BlockSpec(r2.shape[:3] + (tt,), lambda s: (0, 0, 0, item(s, 1) // chunks))
    return pl.pallas_call(
        functools.partial(_peer_dense_kernel, items=items, chunks=chunks, full_rows=full_rows),
        grid=(items + 2,),
        in_specs=[pl.BlockSpec((d, tt), lambda s: (0, item(s, 0) // chunks)),
                  pl.BlockSpec((PEER_CHUNK, d), lambda s: (item(s, 0) % chunks, 0)),
                  pl.BlockSpec((d, PEER_CHUNK), lambda s: (0, item(s, 2) % chunks)),
                  words, words, packed, packed],
        out_specs=pl.BlockSpec((d, tt), lambda s: (0, item(s, 2) // chunks)),
        out_shape=jax.ShapeDtypeStruct((d, n), F32),
        scratch_shapes=[pltpu.VMEM((PEER_CHUNK, tt), F32)] * 2 + [pltpu.VMEM((PEER_CHUNK, tt), BF16)] * 2
        + [pltpu.VMEM((d, tt), F32)],
        compiler_params=_cparams(("arbitrary",)), name="peer_dense")(ht, u, vt, lim, p1, r2, p2)


def _peer_params(g, wq, keys, u, v):
    nh, _, nk, half = keys.shape
    return g, wq.astype(BF16), keys.reshape(nh * 2, nk, half).astype(BF16), u.astype(BF16), v.astype(BF16).T


def _peer(x, params, full_rows=False):
    g, wq, keys, u, vt = params
    scores, ht = _peer_scores_call(x, g, wq, keys)
    lim, p1, r2, p2 = _peer_select_call(scores)
    return _peer_dense_call(ht, u, vt, lim, p1, r2, p2, full_rows)


def _window_sums(ext, halo):
    d = ext.shape[-1]
    gd = d // len(POOL_WINDOWS)
    outs = []
    for gi, w in enumerate(POOL_WINDOWS):
        acc = ext[:, gi * gd:(gi + 1) * gd]
        shift = 1
        while shift < w:
            acc = acc + pltpu.roll(acc, shift, 0)
            shift *= 2
        outs.append(acc[halo:, :])
    return outs


def _pool_prompt_kernel(prev_ref, cur_ref, x_ref, w_ref, sc_ref, o_ref):
    tm = cur_ref.shape[1]
    cur = cur_ref[0]
    ext = jnp.concatenate([prev_ref[0, tm - POOL_HALO:, :], cur], axis=0)
    sums = _window_sums(ext, POOL_HALO)
    pos = (pl.program_id(1) * tm + lax.broadcasted_iota(jnp.int32, (tm, 1), 0)).astype(F32)
    gd = cur.shape[-1] // len(POOL_WINDOWS)
    outs = []
    for gi, w in enumerate(POOL_WINDOWS):
        cnt = jnp.minimum(float(w), pos + 1.0)
        mixed = sums[gi] / cnt - cur[:, gi * gd:(gi + 1) * gd]
        outs.append(_bdot(mixed, w_ref[gi]))
    o_ref[0] = x_ref[0] + jnp.concatenate(outs, axis=-1) * sc_ref[...]


def _pool_prompt_call(h, x, pool_w, pool_scale, tm):
    b, t, d = h.shape
    hp = jnp.concatenate([jnp.zeros((b, tm, d), F32), h], axis=1)
    ng = pool_w.shape[0]
    return pl.pallas_call(
        _pool_prompt_kernel, grid=(b, t // tm),
        in_specs=[pl.BlockSpec((1, tm, d), lambda i, j: (i, j, 0)),
                  pl.BlockSpec((1, tm, d), lambda i, j: (i, j + 1, 0)),
                  pl.BlockSpec((1, tm, d), lambda i, j: (i, j, 0)),
                  pl.BlockSpec(pool_w.shape, lambda i, j: (0, 0, 0)),
                  pl.BlockSpec((1, d), lambda i, j: (0, 0))],
        out_specs=pl.BlockSpec((1, tm, d), lambda i, j: (i, j, 0)),
        out_shape=jax.ShapeDtypeStruct((b, t, d), F32),
        compiler_params=_cparams(("parallel", "parallel")), name="pool_prompt")(
            hp, hp, x, pool_w, pool_scale.reshape(1, d))


def _pool_sample_kernel(hist_ref, h_ref, x_ref, w_ref, sc_ref, o_ref):
    h = h_ref[...]
    nhist = hist_ref.shape[0]
    gd = h.shape[-1] // len(POOL_WINDOWS)
    outs = []
    for gi, w in enumerate(POOL_WINDOWS):
        sl = slice(gi * gd, (gi + 1) * gd)
        acc = h[:, sl]
        for back in range(1, w):
            acc = acc + hist_ref[nhist - back][:, sl]
        mixed = acc / float(w) - h[:, sl]
        outs.append(_bdot(mixed, w_ref[gi]))
    o_ref[...] = x_ref[...] + jnp.concatenate(outs, axis=-1) * sc_ref[...]


def _pool_sample_call(hist, h, x, pool_w, pool_scale):
    b, d = h.shape
    full2 = pl.BlockSpec((b, d), lambda i: (0, 0))
    return pl.pallas_call(
        _pool_sample_kernel, grid=(1,),
        in_specs=[pl.BlockSpec(hist.shape, lambda i: (0, 0, 0)), full2, full2,
                  pl.BlockSpec(pool_w.shape, lambda i: (0, 0, 0)),
                  pl.BlockSpec((1, d), lambda i: (0, 0))],
        out_specs=full2, out_shape=jax.ShapeDtypeStruct((b, d), F32),
        compiler_params=_cparams(("arbitrary",)), name="pool_sample")(
            hist, h, x, pool_w, pool_scale.reshape(1, d))


def _time_block(t, pref):
    for c in range(min(pref, t), 0, -1):
        if t % c == 0 and (c % 8 == 0 or c == t):
            return c
    return t


def kernel(x_prompt, x_sample, state_wkv, state_shift, state_pool, meta_tokens, norm_mix, norm_ffn, norm_final, rwkv_mix, rwkv_w_rkv, rwkv_decay_w1, rwkv_decay_w2, rwkv_decay_b, rwkv_iclr_w1, rwkv_iclr_w2, rwkv_iclr_b, rwkv_gate_w1, rwkv_gate_w2, rwkv_k_k, rwkv_k_a, rwkv_r_k, rwkv_ln_g, rwkv_ln_b, rwkv_w_o, pool_w, pool_scale, peer_wq, peer_keys, peer_u, peer_v):
    bp, seq, d = x_prompt.shape
    bs = x_sample.shape[0]
    tp = N_META + seq
    n_p = bp * tp
    heads = d // HEAD_DIM
    assert x_sample.shape[1] == 1 and (bp * heads) % LANES == 0 and (bs * heads) % LANES == 0
    assert rwkv_w_rkv.shape[0] == 1 and pool_w.shape[0] == 1 and norm_mix.shape[0] == 2

    blockdiag = jnp.asarray(np.kron(np.eye(LANES // HEAD_DIM), np.ones((HEAD_DIM, HEAD_DIM))), BF16)
    proj_params = _rwkv_proj_params(
        rwkv_mix[0], rwkv_w_rkv[0], rwkv_decay_w1[0], rwkv_decay_w2[0], rwkv_decay_b[0], rwkv_iclr_w1[0],
        rwkv_iclr_w2[0], rwkv_iclr_b[0], rwkv_gate_w1[0], rwkv_gate_w2[0], rwkv_k_k[0], rwkv_k_a[0], blockdiag)
    w_o = rwkv_w_o[0].astype(BF16)
    peer_params = [_peer_params(norm_ffn[i], peer_wq[i], peer_keys[i], peer_u[i], peer_v[i]) for i in range(2)]
    pool_wb = pool_w[0].astype(BF16)
    meta = jnp.broadcast_to(meta_tokens[None], (bp, N_META, d))
    x0p = jnp.concatenate([meta, x_prompt], axis=1).reshape(n_p, d)
    x0s = x_sample.reshape(bs, d)

    def rwkv_layer(x0, prev_of, s0, b, t, steps):
        h0 = _norm_call(x0, norm_mix[0])
        r, k, v, dec, aa, bb, g = _rwkv_proj_call(h0, prev_of(h0), proj_params)
        y, s_out = _wkv_group((r, dec, k, v, aa, bb), s0, b, t, steps)
        x1 = _rwkv_out_call(y, r, k, v, g, x0, rwkv_ln_g[0], rwkv_ln_b[0], rwkv_r_k[0], w_o, blockdiag)
        return h0, s_out, x1

    def prev_prompt(h0):
        h3 = h0.reshape(bp, tp, d)
        return jnp.concatenate([jnp.zeros((bp, 1, d), F32), h3[:, :-1]], axis=1).reshape(n_p, d)

    h0p, wkv_p, x1p = rwkv_layer(x0p, prev_prompt, jnp.zeros((bp, heads, HEAD_DIM, HEAD_DIM), F32),
                                 bp, tp, _time_block(tp, 48))
    h0s, wkv_s, x1s = rwkv_layer(x0s, lambda h0: state_shift[0], state_wkv[0], bs, 1, 1)
    f0p = _peer(x1p, peer_params[0])
    f0s = _peer(x1s, peer_params[0])

    x2p, h1p = _add_norm_call(x1p, f0p, norm_mix[1])
    x2s, h1s = _add_norm_call(x1s, f0s, norm_mix[1])
    h1p = h1p.reshape(bp, tp, d)
    x3p = _pool_prompt_call(h1p, x2p.reshape(bp, tp, d), pool_wb, pool_scale[0],
                            _time_block(tp, 344)).reshape(n_p, d)
    x3s = _pool_sample_call(state_pool[0].transpose(1, 0, 2), h1s, x2s, pool_wb, pool_scale[0])
    f1p = _peer(x3p, peer_params[1], full_rows=True)
    f1s = _peer(x3s, peer_params[1])
    _, yfp = _add_norm_call(x3p, f1p, norm_final)
    _, yfs = _add_norm_call(x3s, f1s, norm_final)

    nbuf = state_pool.shape[2]
    y_prompt = yfp.reshape(bp, tp, d)[:, N_META:]
    y_sample = yfs.reshape(bs, 1, d)
    shift_p = h0p.reshape(bp, tp, d)[:, -1][None]
    shift_s = h0s[None]
    pool_p = jnp.concatenate([jnp.zeros((bp, nbuf, d), F32), h1p], axis=1)[:, tp:][None]
    pool_s = jnp.concatenate([state_pool[0][:, 1:], h1s[:, None, :]], axis=1)[None]
    return (y_prompt, y_sample, wkv_p[None], shift_p, pool_p, wkv_s[None], shift_s, pool_s)
```

```python
import functools

import jax
import jax.numpy as jnp
import numpy as np
from jax import lax
from jax.experimental import pallas as pl
from jax.experimental.pallas import tpu as pltpu

F32 = jnp.float32
BF16 = jnp.bfloat16

N_META = 16
RMS_EPS = 1e-6
GN_EPS = 64e-5
HEAD_DIM = 64
LANES = 128
POOL_WINDOWS = (2, 4, 8, 16)
POOL_HALO = 16
PEER_HEADS = 8
PEER_NKEYS = 128
PEER_TOPK = 16
PEER_CHUNK = 1024
PEER_TOKENS = 512
MASK_DTYPE = jnp.bfloat16
MASK_ROWS = 16
ROW_TILE = 256
VMEM_LIMIT = 56 * 1024 * 1024

NEG_INF = float("-inf")


def _cparams(sem):
    return pltpu.CompilerParams(dimension_semantics=sem, vmem_limit_bytes=VMEM_LIMIT)


def _row_tile(n):
    return min(ROW_TILE, n)


def _rms(x, g):
    return x * lax.rsqrt(jnp.mean(x * x, axis=-1, keepdims=True) + RMS_EPS) * g


def _bdot(a, b):
    return jnp.dot(a.astype(BF16), b.astype(BF16), preferred_element_type=F32)


def _head_sum(x, blockdiag):
    outs = []
    for blk in range(x.shape[-1] // LANES):
        xb = x[:, blk * LANES:(blk + 1) * LANES]
        hi = xb.astype(BF16)
        lo = (xb - hi.astype(F32)).astype(BF16)
        outs.append(jnp.dot(hi, blockdiag, preferred_element_type=F32)
                    + jnp.dot(lo, blockdiag, preferred_element_type=F32))
    return jnp.concatenate(outs, axis=-1)


def _norm_kernel(x_ref, g_ref, h_ref):
    h_ref[...] = _rms(x_ref[...], g_ref[...])


def _norm_call(x, g):
    n, d = x.shape
    tm = _row_tile(n)
    row = pl.BlockSpec((tm, d), lambda i: (i, 0))
    return pl.pallas_call(
        _norm_kernel, grid=(pl.cdiv(n, tm),),
        in_specs=[row, pl.BlockSpec((1, d), lambda i: (0, 0))],
        out_specs=row, out_shape=jax.ShapeDtypeStruct((n, d), F32),
        compiler_params=_cparams(("parallel",)), name="rmsnorm")(x, g.reshape(1, d))


def _add_norm_kernel(a_ref, bt_ref, g_ref, x_ref, h_ref):
    x = a_ref[...] + bt_ref[...].T
    x_ref[...] = x
    h_ref[...] = _rms(x, g_ref[...])


def _add_norm_call(a, bt, g):
    n, d = a.shape
    tm = _row_tile(n)
    row = pl.BlockSpec((tm, d), lambda i: (i, 0))
    return pl.pallas_call(
        _add_norm_kernel, grid=(pl.cdiv(n, tm),),
        in_specs=[row, pl.BlockSpec((d, tm), lambda i: (0, i)), pl.BlockSpec((1, d), lambda i: (0, 0))],
        out_specs=[row, row],
        out_shape=[jax.ShapeDtypeStruct((n, d), F32)] * 2,
        compiler_params=_cparams(("parallel",)), name="add_rmsnorm")(a, bt, g.reshape(1, d))


def _rwkv_proj_kernel(h_ref, prev_ref, mix_ref, wrkv_ref, dw1_ref, dw2_ref, db_ref,
                      aw1_ref, aw2_ref, ab_ref, gw1_ref, gw2_ref, kk_ref, ka_ref, bd_ref,
                      r_ref, k_ref, v_ref, dec_ref, aa_ref, bb_ref, g_ref):
    h = h_ref[...]
    xx = prev_ref[...] - h

    def xs(c):
        return h + xx * mix_ref[c:c + 1, :]

    r = _bdot(xs(0), wrkv_ref[0])
    k = _bdot(xs(1), wrkv_ref[1])
    v = _bdot(xs(2), wrkv_ref[2])
    w = -jax.nn.softplus(-(db_ref[...] + _bdot(jnp.tanh(_bdot(xs(3), dw1_ref[...])), dw2_ref[...]))) - 0.5
    a = jax.nn.sigmoid(ab_ref[...] + _bdot(_bdot(xs(4), aw1_ref[...]), aw2_ref[...]))
    g = _bdot(jax.nn.sigmoid(_bdot(xs(5), gw1_ref[...])), gw2_ref[...])
    kk = k * kk_ref[...]
    nrm = jnp.sqrt(_head_sum(kk * kk, bd_ref[...]))
    kk = kk / jnp.maximum(nrm, 1e-12)
    r_ref[...] = r
    k_ref[...] = k * (1.0 + (a - 1.0) * ka_ref[...])
    v_ref[...] = v
    dec_ref[...] = jnp.exp(-jnp.exp(w))
    aa_ref[...] = -kk
    bb_ref[...] = kk * a
    g_ref[...] = g


def _rwkv_proj_params(mix, w_rkv, dw1, dw2, db, aw1, aw2, ab, gw1, gw2, k_k, k_a, blockdiag):
    d = mix.shape[-1]
    return [mix, w_rkv.astype(BF16), dw1.astype(BF16), dw2.astype(BF16), db.reshape(1, d),
            aw1.astype(BF16), aw2.astype(BF16), ab.reshape(1, d), gw1.astype(BF16),
            gw2.astype(BF16), k_k.reshape(1, d), k_a.reshape(1, d), blockdiag]


def _rwkv_proj_call(h, prev, params):
    n, d = h.shape
    tm = _row_tile(n)
    row = pl.BlockSpec((tm, d), lambda i: (i, 0))

    def full(x):
        nd = x.ndim
        return pl.BlockSpec(x.shape, lambda i: (0,) * nd)

    return pl.pallas_call(
        _rwkv_proj_kernel, grid=(pl.cdiv(n, tm),),
        in_specs=[row, row] + [full(p) for p in params],
        out_specs=[row] * 7,
        out_shape=[jax.ShapeDtypeStruct((n, d), F32)] * 7,
        compiler_params=_cparams(("parallel",)), name="rwkv_proj")(h, prev, *params)


def _wkv_kernel(r_ref, d_ref, k_ref, v_ref, a_ref, b_ref, anext_ref, s0_ref, y_ref, s_ref, sa_ref, *, steps):
    nk = s_ref.shape[0]

    @pl.when(pl.program_id(1) == 0)
    def _():
        s_ref[...] = s0_ref[...]
        sa0 = None
        for kk in range(nk):
            term = s0_ref[kk] * a_ref[0, kk:kk + 1, :]
            sa0 = term if sa0 is None else sa0 + term
        sa_ref[...] = sa0

    def step(t, sa, next_row):
        vv = v_ref[t]
        y = None
        sa_next = None
        for kk in range(nk):
            sk = (s_ref[kk] * d_ref[t, kk:kk + 1, :] + sa * b_ref[t, kk:kk + 1, :]
                  + vv * k_ref[t, kk:kk + 1, :])
            s_ref[kk] = sk
            ty = sk * r_ref[t, kk:kk + 1, :]
            ta = sk * next_row(kk)
            y = ty if y is None else y + ty
            sa_next = ta if sa_next is None else sa_next + ta
        y_ref[t] = y
        return sa_next

    sa = lax.fori_loop(0, steps - 1, lambda t, sa: step(t, sa, lambda kk: a_ref[t + 1, kk:kk + 1, :]),
                       sa_ref[...])
    sa_ref[...] = step(steps - 1, sa, lambda kk: anext_ref[0, kk:kk + 1, :])


def _wkv_call(r, dec, k, v, aa, bb, s0, steps):
    t_len, hd, lanes = r.shape
    seq = pl.BlockSpec((steps, hd, LANES), lambda l, t: (t, 0, l))
    nxt = pl.BlockSpec((1, hd, LANES), lambda l, t: (jnp.minimum((t + 1) * steps, t_len - 1), 0, l))
    st = pl.BlockSpec((hd, hd, LANES), lambda l, t: (0, 0, l))
    return pl.pallas_call(
        functools.partial(_wkv_kernel, steps=steps),
        grid=(lanes // LANES, t_len // steps),
        in_specs=[seq] * 6 + [nxt, st],
        out_specs=[seq, st],
        out_shape=[jax.ShapeDtypeStruct((t_len, hd, lanes), F32),
                   jax.ShapeDtypeStruct((hd, hd, lanes), F32)],
        scratch_shapes=[pltpu.VMEM((hd, LANES), F32)],
        compiler_params=_cparams(("parallel", "arbitrary")), name="wkv_scan")(r, dec, k, v, aa, bb, aa, s0)


def _to_scan(x, b, t):
    heads = x.shape[-1] // HEAD_DIM
    return x.reshape(t, b, heads, HEAD_DIM).transpose(0, 3, 1, 2).reshape(t, HEAD_DIM, b * heads)


def _from_scan(y, b, t):
    heads = y.shape[-1] // b
    return y.reshape(t, HEAD_DIM, b, heads).transpose(0, 2, 3, 1).reshape(t * b, heads * HEAD_DIM)


def _wkv_group(rows, s0, b, t, steps):
    heads = s0.shape[1]
    s0_l = s0.transpose(3, 2, 0, 1).reshape(HEAD_DIM, HEAD_DIM, b * heads)
    y, s_l = _wkv_call(*[_to_scan(x, b, t) for x in rows], s0_l, steps)
    s_out = s_l.reshape(HEAD_DIM, HEAD_DIM, b, heads).transpose(2, 3, 1, 0)
    return _from_scan(y, b, t), s_out


def _rwkv_out_kernel(y_ref, r_ref, k_ref, v_ref, g_ref, x_ref, lng_ref, lnb_ref, rk_ref, wo_ref,
                     bd_ref, o_ref):
    bd = bd_ref[...]
    y = y_ref[...]
    inv_n = 1.0 / HEAD_DIM
    mu = _head_sum(y, bd) * inv_n
    yc = y - mu
    var = _head_sum(yc * yc, bd) * inv_n
    yn = yc * lax.rsqrt(var + GN_EPS) * lng_ref[...] + lnb_ref[...]
    bonus = _head_sum(r_ref[...] * k_ref[...] * rk_ref[...], bd) * v_ref[...]
    o_ref[...] = x_ref[...] + _bdot((yn + bonus) * g_ref[...], wo_ref[...])


def _rwkv_out_call(y, r, k, v, g, x, ln_g, ln_b, r_k, w_o, blockdiag):
    n, d = y.shape
    tm = _row_tile(n)
    row = pl.BlockSpec((tm, d), lambda i: (i, 0))
    vec = pl.BlockSpec((1, d), lambda i: (0, 0))
    return pl.pallas_call(
        _rwkv_out_kernel, grid=(pl.cdiv(n, tm),),
        in_specs=[row] * 6 + [vec, vec, vec, pl.BlockSpec((d, d), lambda i: (0, 0)),
                              pl.BlockSpec((LANES, LANES), lambda i: (0, 0))],
        out_specs=row, out_shape=jax.ShapeDtypeStruct((n, d), F32),
        compiler_params=_cparams(("parallel",)), name="rwkv_out")(
            y, r, k, v, g, x, ln_g.reshape(1, d), ln_b.reshape(1, d), r_k.reshape(1, d), w_o, blockdiag)


def _peer_scores_kernel(x_ref, g_ref, wq_ref, keys_ref, s_ref, ht_ref):
    h = _rms(x_ref[...], g_ref[...])
    ht_ref[...] = h.T.astype(BF16)
    q = _bdot(h, wq_ref[...]).astype(BF16)
    for hc in range(keys_ref.shape[0]):
        s_ref[hc] = lax.dot_general(keys_ref[hc], q[:, hc * LANES:(hc + 1) * LANES],
                                    (((1,), (1,)), ((), ())), preferred_element_type=F32)


def _peer_scores_call(x, g, wq, keys):
    n, d = x.shape
    tm = _row_tile(n)
    nhc = keys.shape[0]
    return pl.pallas_call(
        _peer_scores_kernel, grid=(pl.cdiv(n, tm),),
        in_specs=[pl.BlockSpec((tm, d), lambda i: (i, 0)), pl.BlockSpec((1, d), lambda i: (0, 0)),
                  pl.BlockSpec(wq.shape, lambda i: (0, 0)),
                  pl.BlockSpec(keys.shape, lambda i: (0, 0, 0))],
        out_specs=[pl.BlockSpec((nhc, PEER_NKEYS, tm), lambda i: (0, 0, i)),
                   pl.BlockSpec((d, tm), lambda i: (0, i))],
        out_shape=[jax.ShapeDtypeStruct((nhc, PEER_NKEYS, n), F32),
                   jax.ShapeDtypeStruct((d, n), BF16)],
        compiler_params=_cparams(("parallel",)), name="peer_scores")(x, g.reshape(1, d), wq, keys)


def _top16(s, iota, exact_ties):
    rank = jnp.full(s.shape, float(PEER_TOPK), F32)
    vals = []
    for a in range(PEER_TOPK):
        m = jnp.max(s, axis=0, keepdims=True)
        hit = s == m
        if exact_ties:
            idx = jnp.min(jnp.where(hit, iota, float(PEER_NKEYS)), axis=0, keepdims=True)
            hit = iota == idx
        rank = jnp.where(hit, float(a), rank)
        s = jnp.where(hit, NEG_INF, s)
        vals.append(m)
    ranked = jnp.sum(jnp.where(rank < float(PEER_TOPK), 1.0, 0.0), axis=0, keepdims=True)
    return rank, vals, ranked


def _dup_mask_bits(x):
    bits = lax.bitcast_convert_type(x.astype(MASK_DTYPE).astype(F32), jnp.uint32)
    return bits | (bits >> 16)


def _peer_select_kernel(s_ref, l_ref, p1_ref, r2_ref, p2_ref):
    miscount = _peer_select_body(s_ref, l_ref, p1_ref, r2_ref, p2_ref, exact_ties=False)

    @pl.when(jnp.max(miscount) > 0.0)
    def _():
        _peer_select_body(s_ref, l_ref, p1_ref, r2_ref, p2_ref, exact_ties=True)


def _peer_select_body(s_ref, l_ref, p1_ref, r2_ref, p2_ref, exact_ties):
    lanes = s_ref.shape[-1]
    iota = lax.broadcasted_iota(jnp.int32, (PEER_NKEYS, lanes), 0).astype(F32)
    iota16 = lax.broadcasted_iota(jnp.int32, (PEER_TOPK, lanes), 0).astype(F32)
    miscount = jnp.zeros((1, lanes), F32)
    for h in range(PEER_HEADS):
        s1 = s_ref[2 * h]
        s2 = s_ref[2 * h + 1]
        rank1, v1, n1 = _top16(s1, iota, exact_ties)
        rank2, v2, n2 = _top16(s2, iota, exact_ties)
        miscount = miscount + jnp.abs(n1 - float(PEER_TOPK)) + jnp.abs(n2 - float(PEER_TOPK))
        v1a = jnp.concatenate(v1, axis=0)
        v2a = jnp.concatenate(v2, axis=0)
        top = v1[0] + v2[0]
        taken = jnp.zeros((PEER_TOPK, lanes), F32)
        front = v1a + v2[0]
        zsum = jnp.zeros((1, lanes), F32)
        for _ in range(PEER_TOPK):
            m = jnp.max(front, axis=0, keepdims=True)
            a_star = jnp.min(jnp.where(front == m, iota16, float(PEER_TOPK)), axis=0, keepdims=True)
            hit = iota16 == a_star
            zsum = zsum + jnp.exp(m - top)
            taken = jnp.where(hit, taken + 1.0, taken)
            cnt = jnp.max(jnp.where(hit, taken, -1.0), axis=0, keepdims=True)
            nxt = jnp.max(jnp.where(iota16 == cnt, v2a, NEG_INF), axis=0, keepdims=True)
            front = jnp.where(hit, v1a + nxt, front)
        lim = jnp.zeros((PEER_NKEYS, lanes), F32)
        for a in range(PEER_TOPK):
            lim = jnp.where(rank1 == float(a), taken[a:a + 1, :], lim)
        l_ref[h] = _dup_mask_bits(lim)
        p1_ref[h] = _dup_mask_bits(jnp.exp(s1 - v1[0]))
        p2 = jnp.exp(s2 - v2[0]) / zsum
        for r in range(PEER_NKEYS // MASK_ROWS):
            rows = slice(r * MASK_ROWS, (r + 1) * MASK_ROWS)
            r2_ref[h, r] = rank2[rows].astype(MASK_DTYPE)
            p2_ref[h, r] = p2[rows].astype(MASK_DTYPE)
    return miscount


def _peer_select_call(scores):
    nhc, nk, n = scores.shape
    tl = LANES
    groups = nk // MASK_ROWS
    words = pl.BlockSpec((PEER_HEADS, nk, tl), lambda i: (0, 0, i))
    packed = pl.BlockSpec((PEER_HEADS, groups, MASK_ROWS, tl), lambda i: (0, 0, 0, i))
    return pl.pallas_call(
        _peer_select_kernel, grid=(pl.cdiv(n, tl),),
        in_specs=[pl.BlockSpec((nhc, nk, tl), lambda i: (0, 0, i))],
        out_specs=[words, words, packed, packed],
        out_shape=[jax.ShapeDtypeStruct((PEER_HEADS, nk, n), jnp.uint32)] * 2
        + [jax.ShapeDtypeStruct((PEER_HEADS, groups, MASK_ROWS, n), MASK_DTYPE)] * 2,
        compiler_params=_cparams(("parallel",)), name="peer_select")(scores)


def _gate_weights(cb, ii, heads, w, l_ref, p1_ref, r2_ref, p2_ref):
    blocks = PEER_CHUNK // PEER_NKEYS
    base = pl.multiple_of(cb * blocks, blocks)
    w = list(w)
    for h in heads:
        lw = l_ref[h, pl.ds(base, blocks), :]
        pw = p1_ref[h, pl.ds(base, blocks), :]
        lim = pltpu.bitcast(jnp.broadcast_to(lw[ii:ii + 1, :], lw.shape), MASK_DTYPE)
        p1 = pltpu.bitcast(jnp.broadcast_to(pw[ii:ii + 1, :], pw.shape), MASK_DTYPE)
        for r in range(len(w)):
            term = jnp.where(r2_ref[h, r] < lim, p2_ref[h, r], jnp.zeros_like(p1)) * p1
            w[r] = term if w[r] is None else w[r] + term
    return w


def _gate_store(ii, w, z_ref, g_ref):
    for r in range(len(w)):
        rows = slice(ii * PEER_NKEYS + r * MASK_ROWS, ii * PEER_NKEYS + (r + 1) * MASK_ROWS)
        z = z_ref[rows, :]
        act = 0.5 * z * (1.0 + lax.erf(z * np.float32(np.sqrt(0.5))))
        g_ref[rows, :] = (act.astype(MASK_DTYPE) * w[r]).astype(g_ref.dtype)


def _peer_dense_kernel(ht_ref, u_ref, vt_ref, l_ref, p1_ref, r2_ref, p2_ref, o_ref,
                       z0_ref, z1_ref, g0_ref, g1_ref, acc_ref, *, items, chunks):
    s = pl.program_id(0)

    @pl.when(s == 0)
    def _():
        for ref in (z0_ref, z1_ref, g0_ref, g1_ref):
            ref[...] = jnp.zeros_like(ref)

    gate_chunk = jnp.clip(s - 1, 0, items - 1) % chunks

    def stages(z_new, z_old, g_new, g_old):
        blocks = PEER_CHUNK // PEER_NKEYS
        drows = o_ref.shape[0] // blocks
        sel = (l_ref, p1_ref, r2_ref, p2_ref)
        tt = o_ref.shape[1]
        nsplit = 2 if tt % (2 * LANES) == 0 else 1
        ncols = [slice(c * (tt // nsplit), (c + 1) * (tt // nsplit)) for c in range(nsplit)]
        hq = PEER_HEADS // (2 * nsplit)
        for ii in range(blocks):
            er = slice(ii * PEER_NKEYS, (ii + 1) * PEER_NKEYS)
            dr = slice(ii * drows, (ii + 1) * drows)
            w = [None] * (PEER_NKEYS // MASK_ROWS)
            q = 0
            for nc in ncols:
                w = _gate_weights(gate_chunk, ii, range(q * hq, (q + 1) * hq), w, *sel)
                q += 1
                z_new[er, nc] = jnp.dot(u_ref[er, :], ht_ref[:, nc], preferred_element_type=F32)
            for c, nc in enumerate(ncols):
                w = _gate_weights(gate_chunk, ii, range(q * hq, (q + 1) * hq), w, *sel)
                q += 1
                if c == nsplit - 1:
                    _gate_store(ii, w, z_old, g_new)
                acc_ref[dr, nc] = jnp.dot(vt_ref[dr, :], g_old[:, nc], preferred_element_type=F32)

    @pl.when(s % 2 == 0)
    def _():
        stages(z0_ref, z1_ref, g1_ref, g0_ref)

    @pl.when(s % 2 == 1)
    def _():
        stages(z1_ref, z0_ref, g0_ref, g1_ref)

    first = jnp.clip(s - 2, 0, items - 1) % chunks == 0

    @pl.when(first)
    def _():
        o_ref[...] = acc_ref[...]

    @pl.when(jnp.logical_not(first))
    def _():
        o_ref[...] += acc_ref[...]


def _peer_dense_call(ht, u, vt, lim, p1, r2, p2):
    d, n = ht.shape
    chunks = u.shape[0] // PEER_CHUNK
    tt = min(PEER_TOKENS, n)
    items = pl.cdiv(n, tt) * chunks

    def item(s, lag):
        return jnp.clip(s - lag, 0, items - 1)

    words = pl.BlockSpec((PEER_HEADS, PEER_NKEYS, tt), lambda s: (0, 0, item(s, 1) // chunks))
    packed = pl.BlockSpec(r2.shape[:3] + (tt,), lambda s: (0, 0, 0, item(s, 1) // chunks))
    return pl.pallas_call(
        functools.partial(_peer_dense_kernel, items=items, chunks=chunks),
        grid=(items + 2,),
        in_specs=[pl.BlockSpec((d, tt), lambda s: (0, item(s, 0) // chunks)),
                  pl.BlockSpec((PEER_CHUNK, d), lambda s: (item(s, 0) % chunks, 0)),
                  pl.BlockSpec((d, PEER_CHUNK), lambda s: (0, item(s, 2) % chunks)),
                  words, words, packed, packed],
        out_specs=pl.BlockSpec((d, tt), lambda s: (0, item(s, 2) // chunks)),
        out_shape=jax.ShapeDtypeStruct((d, n), F32),
        scratch_shapes=[pltpu.VMEM((PEER_CHUNK, tt), F32)] * 2 + [pltpu.VMEM((PEER_CHUNK, tt), BF16)] * 2
        + [pltpu.VMEM((d, tt), F32)],
        compiler_params=_cparams(("arbitrary",)), name="peer_dense")(ht, u, vt, lim, p1, r2, p2)


def _peer_params(g, wq, keys, u, v):
    nh, _, nk, half = keys.shape
    return g, wq.astype(BF16), keys.reshape(nh * 2, nk, half).astype(BF16), u.astype(BF16), v.astype(BF16).T


def _peer(x, params):
    g, wq, keys, u, vt = params
    scores, ht = _peer_scores_call(x, g, wq, keys)
    lim, p1, r2, p2 = _peer_select_call(scores)
    return _peer_dense_call(ht, u, vt, lim, p1, r2, p2)


def _pool_prompt_kernel(halo_ref, cur_ref, x_ref, w_ref, sc_ref, o_ref, *, batch):
    tm = cur_ref.shape[0]
    hr = halo_ref.shape[0]
    cur = cur_ref[...]
    halo = jnp.where(pl.program_id(0) == 0, 0.0, halo_ref[...])
    ext = jnp.concatenate([halo, cur], axis=0)
    row = (pl.program_id(0) * tm + lax.broadcasted_iota(jnp.int32, (tm, 1), 0)).astype(F32)
    pos = jnp.floor(row / float(batch))
    gd = cur.shape[-1] // len(POOL_WINDOWS)
    outs = []
    for gi, w in enumerate(POOL_WINDOWS):
        acc = ext[:, gi * gd:(gi + 1) * gd]
        dropped = 0
        shift = 1
        while shift < w:
            k = shift * batch
            acc = acc[k:] + acc[:-k]
            dropped += k
            shift *= 2
        cnt = jnp.minimum(float(w), pos + 1.0)
        mixed = acc[hr - dropped:] / cnt - cur[:, gi * gd:(gi + 1) * gd]
        outs.append(_bdot(mixed, w_ref[gi]))
    o_ref[...] = x_ref[...] + jnp.concatenate(outs, axis=-1) * sc_ref[...]


def _pool_prompt_call(h, x, pool_w, pool_scale, batch):
    n, d = h.shape
    tm = _row_tile(n)
    hr = POOL_HALO * batch
    assert tm % hr == 0 and hr % 8 == 0
    row = pl.BlockSpec((tm, d), lambda i: (i, 0))
    return pl.pallas_call(
        functools.partial(_pool_prompt_kernel, batch=batch), grid=(pl.cdiv(n, tm),),
        in_specs=[pl.BlockSpec((hr, d), lambda i: (jnp.maximum(i * (tm // hr) - 1, 0), 0)), row, row,
                  pl.BlockSpec(pool_w.shape, lambda i: (0, 0, 0)),
                  pl.BlockSpec((1, d), lambda i: (0, 0))],
        out_specs=row, out_shape=jax.ShapeDtypeStruct((n, d), F32),
        compiler_params=_cparams(("parallel",)), name="pool_prompt")(
            h, h, x, pool_w, pool_scale.reshape(1, d))


def _pool_sample_kernel(hist_ref, h_ref, x_ref, w_ref, sc_ref, o_ref):
    h = h_ref[...]
    nhist = hist_ref.shape[0]
    gd = h.shape[-1] // len(POOL_WINDOWS)
    outs = []
    for gi, w in enumerate(POOL_WINDOWS):
        sl = slice(gi * gd, (gi + 1) * gd)
        acc = h[:, sl]
        for back in range(1, w):
            acc = acc + hist_ref[nhist - back][:, sl]
        mixed = acc / float(w) - h[:, sl]
        outs.append(_bdot(mixed, w_ref[gi]))
    o_ref[...] = x_ref[...] + jnp.concatenate(outs, axis=-1) * sc_ref[...]


def _pool_sample_call(hist, h, x, pool_w, pool_scale):
    b, d = h.shape
    full2 = pl.BlockSpec((b, d), lambda i: (0, 0))
    return pl.pallas_call(
        _pool_sample_kernel, grid=(1,),
        in_specs=[pl.BlockSpec(hist.shape, lambda i: (0, 0, 0)), full2, full2,
                  pl.BlockSpec(pool_w.shape, lambda i: (0, 0, 0)),
                  pl.BlockSpec((1, d), lambda i: (0, 0))],
        out_specs=full2, out_shape=jax.ShapeDtypeStruct((b, d), F32),
        compiler_params=_cparams(("arbitrary",)), name="pool_sample")(
            hist, h, x, pool_w, pool_scale.reshape(1, d))


def _time_block(t, pref):
    for c in range(min(pref, t), 0, -1):
        if t % c == 0 and (c % 8 == 0 or c == t):
            return c
    return t


def kernel(x_prompt, x_sample, state_wkv, state_shift, state_pool, meta_tokens, norm_mix, norm_ffn, norm_final, rwkv_mix, rwkv_w_rkv, rwkv_decay_w1, rwkv_decay_w2, rwkv_decay_b, rwkv_iclr_w1, rwkv_iclr_w2, rwkv_iclr_b, rwkv_gate_w1, rwkv_gate_w2, rwkv_k_k, rwkv_k_a, rwkv_r_k, rwkv_ln_g, rwkv_ln_b, rwkv_w_o, pool_w, pool_scale, peer_wq, peer_keys, peer_u, peer_v):
    bp, seq, d = x_prompt.shape
    bs = x_sample.shape[0]
    tp = N_META + seq
    n_p = bp * tp
    heads = d // HEAD_DIM
    assert x_sample.shape[1] == 1 and (bp * heads) % LANES == 0 and (bs * heads) % LANES == 0
    assert rwkv_w_rkv.shape[0] == 1 and pool_w.shape[0] == 1 and norm_mix.shape[0] == 2

    blockdiag = jnp.asarray(np.kron(np.eye(LANES // HEAD_DIM), np.ones((HEAD_DIM, HEAD_DIM))), BF16)
    proj_params = _rwkv_proj_params(
        rwkv_mix[0], rwkv_w_rkv[0], rwkv_decay_w1[0], rwkv_decay_w2[0], rwkv_decay_b[0], rwkv_iclr_w1[0],
        rwkv_iclr_w2[0], rwkv_iclr_b[0], rwkv_gate_w1[0], rwkv_gate_w2[0], rwkv_k_k[0], rwkv_k_a[0], blockdiag)
    w_o = rwkv_w_o[0].astype(BF16)
    peer_params = [_peer_params(norm_ffn[i], peer_wq[i], peer_keys[i], peer_u[i], peer_v[i]) for i in range(2)]
    pool_wb = pool_w[0].astype(BF16)
    meta = jnp.broadcast_to(meta_tokens[:, None], (N_META, bp, d))
    x0p = jnp.concatenate([meta, x_prompt.transpose(1, 0, 2)], axis=0).reshape(n_p, d)
    x0s = x_sample.reshape(bs, d)

    def rwkv_layer(x0, prev_of, s0, b, t, steps):
        h0 = _norm_call(x0, norm_mix[0])
        r, k, v, dec, aa, bb, g = _rwkv_proj_call(h0, prev_of(h0), proj_params)
        y, s_out = _wkv_group((r, dec, k, v, aa, bb), s0, b, t, steps)
        x1 = _rwkv_out_call(y, r, k, v, g, x0, rwkv_ln_g[0], rwkv_ln_b[0], rwkv_r_k[0], w_o, blockdiag)
        return h0, s_out, x1

    def prev_prompt(h0):
        return jnp.concatenate([jnp.zeros((bp, d), F32), h0[:n_p - bp]], axis=0)

    h0p, wkv_p, x1p = rwkv_layer(x0p, prev_prompt, jnp.zeros((bp, heads, HEAD_DIM, HEAD_DIM), F32),
                                 bp, tp, _time_block(tp, 48))
    h0s, wkv_s, x1s = rwkv_layer(x0s, lambda h0: state_shift[0], state_wkv[0], bs, 1, 1)
    f0p = _peer(x1p, peer_params[0])
    f0s = _peer(x1s, peer_params[0])

    x2p, h1p = _add_norm_call(x1p, f0p, norm_mix[1])
    x2s, h1s = _add_norm_call(x1s, f0s, norm_mix[1])
    x3p = _pool_prompt_call(h1p, x2p, pool_wb, pool_scale[0], bp)
    x3s = _pool_sample_call(state_pool[0].transpose(1, 0, 2), h1s, x2s, pool_wb, pool_scale[0])
    f1p = _peer(x3p, peer_params[1])
    f1s = _peer(x3s, peer_params[1])
    _, yfp = _add_norm_call(x3p, f1p, norm_final)
    _, yfs = _add_norm_call(x3s, f1s, norm_final)

    nbuf = state_pool.shape[2]
    y_prompt = yfp.reshape(tp, bp, d)[N_META:].transpose(1, 0, 2)
    y_sample = yfs.reshape(bs, 1, d)
    shift_p = h0p[n_p - bp:][None]
    shift_s = h0s[None]
    if tp >= nbuf:
        tail = h1p[n_p - nbuf * bp:]
    else:
        tail = jnp.concatenate([jnp.zeros(((nbuf - tp) * bp, d), F32), h1p], axis=0)
    pool_p = tail.reshape(nbuf, bp, d).transpose(1, 0, 2)[None]
    pool_s = jnp.concatenate([state_pool[0][:, 1:], h1s[:, None, :]], axis=1)[None]
    return (y_prompt, y_sample, wkv_p[None], shift_p, pool_p, wkv_s[None], shift_s, pool_s)
```

```python
import functools

import jax
import jax.numpy as jnp
import numpy as np
from jax import lax
from jax.experimental import pallas as pl
from jax.experimental.pallas import tpu as pltpu

F32 = jnp.float32
BF16 = jnp.bfloat16

N_META = 16
RMS_EPS = 1e-6
GN_EPS = 64e-5
HEAD_DIM = 64
LANES = 128
POOL_WINDOWS = (2, 4, 8, 16)
POOL_HALO = 16
PEER_HEADS = 8
PEER_NKEYS = 128
PEER_TOPK = 16
PEER_CHUNK = 1024
PEER_TOKENS = 512
MASK_DTYPE = jnp.bfloat16
MASK_ROWS = 16
ROW_TILE = 256
VMEM_LIMIT = 56 * 1024 * 1024

NEG_INF = float("-inf")


def _cparams(sem):
    return pltpu.CompilerParams(dimension_semantics=sem, vmem_limit_bytes=VMEM_LIMIT)


def _row_tile(n):
    return min(ROW_TILE, n)


def _rms(x, g):
    return x * lax.rsqrt(jnp.mean(x * x, axis=-1, keepdims=True) + RMS_EPS) * g


def _bdot(a, b):
    return jnp.dot(a.astype(BF16), b.astype(BF16), preferred_element_type=F32)


def _head_sum(x, blockdiag):
    outs = []
    for blk in range(x.shape[-1] // LANES):
        xb = x[:, blk * LANES:(blk + 1) * LANES]
        hi = xb.astype(BF16)
        lo = (xb - hi.astype(F32)).astype(BF16)
        outs.append(jnp.dot(hi, blockdiag, preferred_element_type=F32)
                    + jnp.dot(lo, blockdiag, preferred_element_type=F32))
    return jnp.concatenate(outs, axis=-1)


def _norm_kernel(x_ref, g_ref, h_ref):
    h_ref[...] = _rms(x_ref[...], g_ref[...])


def _norm_call(x, g):
    n, d = x.shape
    tm = _row_tile(n)
    row = pl.BlockSpec((tm, d), lambda i: (i, 0))
    return pl.pallas_call(
        _norm_kernel, grid=(pl.cdiv(n, tm),),
        in_specs=[row, pl.BlockSpec((1, d), lambda i: (0, 0))],
        out_specs=row, out_shape=jax.ShapeDtypeStruct((n, d), F32),
        compiler_params=_cparams(("parallel",)), name="rmsnorm")(x, g.reshape(1, d))


def _add_norm_kernel(a_ref, bt_ref, g_ref, x_ref, h_ref):
    x = a_ref[...] + bt_ref[...].T
    x_ref[...] = x
    h_ref[...] = _rms(x, g_ref[...])


def _add_norm_call(a, bt, g):
    n, d = a.shape
    tm = _row_tile(n)
    row = pl.BlockSpec((tm, d), lambda i: (i, 0))
    return pl.pallas_call(
        _add_norm_kernel, grid=(pl.cdiv(n, tm),),
        in_specs=[row, pl.BlockSpec((d, tm), lambda i: (0, i)), pl.BlockSpec((1, d), lambda i: (0, 0))],
        out_specs=[row, row],
        out_shape=[jax.ShapeDtypeStruct((n, d), F32)] * 2,
        compiler_params=_cparams(("parallel",)), name="add_rmsnorm")(a, bt, g.reshape(1, d))


def _rows_to_scan(x8):
    pairs = x8.shape[-1] // LANES
    a = jnp.concatenate([x8[:, p * LANES:(p + 1) * LANES] for p in range(pairs)], axis=0)
    at = a.T
    return jnp.concatenate([at[:HEAD_DIM], at[HEAD_DIM:]], axis=1)


def _scan_to_rows(tile):
    at = jnp.concatenate([tile[:, :HEAD_DIM], tile[:, HEAD_DIM:]], axis=0)
    a = at.T
    return jnp.concatenate([a[p * 8:(p + 1) * 8, :] for p in range(a.shape[0] // 8)], axis=1)


def _store_scan(ref, x, time_groups):
    for j in range(x.shape[0] // 8):
        tile = _rows_to_scan(x[j * 8:(j + 1) * 8, :])
        if time_groups:
            ref[j] = tile
        else:
            ref[0, :, j * LANES:(j + 1) * LANES] = tile


def _load_scan(ref, time_groups):
    groups = ref.shape[0] if time_groups else ref.shape[2] // LANES
    tiles = [ref[j] if time_groups else ref[0, :, j * LANES:(j + 1) * LANES] for j in range(groups)]
    return jnp.concatenate([_scan_to_rows(t) for t in tiles], axis=0)


def _scan_spec(tm, time_groups):
    if time_groups:
        return pl.BlockSpec((tm // 8, HEAD_DIM, LANES), lambda i: (i, 0, 0))
    return pl.BlockSpec((1, HEAD_DIM, tm // 8 * LANES), lambda i: (0, 0, i))


def _scan_shape(n, time_groups):
    return (n // 8, HEAD_DIM, LANES) if time_groups else (1, HEAD_DIM, n // 8 * LANES)


def _state_to_scan(s):
    b, heads, nv, nk = s.shape
    s = s.reshape(b // 8, 8, heads // 2, 2, nv, nk).transpose(5, 4, 0, 3, 2, 1)
    return s.reshape(nk, nv, b * heads)


def _state_from_scan(s, b):
    nk, nv, lanes = s.shape
    heads = lanes // b
    return s.reshape(nk, nv, b // 8, 2, heads // 2, 8).transpose(2, 5, 4, 3, 1, 0).reshape(b, heads, nv, nk)


def _rwkv_proj_kernel(h_ref, prev_ref, mix_ref, wrkv_ref, dw1_ref, dw2_ref, db_ref,
                      aw1_ref, aw2_ref, ab_ref, gw1_ref, gw2_ref, kk_ref, ka_ref, rk_ref, bd_ref,
                      r_ref, dec_ref, k_ref, v_ref, aa_ref, bb_ref, g_ref, bonus_ref, *, time_groups):
    h = h_ref[...]
    xx = prev_ref[...] - h

    def xs(c):
        return h + xx * mix_ref[c:c + 1, :]

    r = _bdot(xs(0), wrkv_ref[0])
    k = _bdot(xs(1), wrkv_ref[1])
    v = _bdot(xs(2), wrkv_ref[2])
    w = -jax.nn.softplus(-(db_ref[...] + _bdot(jnp.tanh(_bdot(xs(3), dw1_ref[...])), dw2_ref[...]))) - 0.5
    a = jax.nn.sigmoid(ab_ref[...] + _bdot(_bdot(xs(4), aw1_ref[...]), aw2_ref[...]))
    g = _bdot(jax.nn.sigmoid(_bdot(xs(5), gw1_ref[...])), gw2_ref[...])
    kk = k * kk_ref[...]
    nrm = jnp.sqrt(_head_sum(kk * kk, bd_ref[...]))
    kk = kk / jnp.maximum(nrm, 1e-12)
    kf = k * (1.0 + (a - 1.0) * ka_ref[...])
    _store_scan(r_ref, r, time_groups)
    _store_scan(dec_ref, jnp.exp(-jnp.exp(w)), time_groups)
    _store_scan(k_ref, kf, time_groups)
    _store_scan(v_ref, v, time_groups)
    _store_scan(aa_ref, -kk, time_groups)
    _store_scan(bb_ref, kk * a, time_groups)
    g_ref[...] = g
    bonus_ref[...] = _head_sum(r * kf * rk_ref[...], bd_ref[...]) * v


def _rwkv_proj_params(mix, w_rkv, dw1, dw2, db, aw1, aw2, ab, gw1, gw2, k_k, k_a, r_k, blockdiag):
    d = mix.shape[-1]
    return [mix, w_rkv.astype(BF16), dw1.astype(BF16), dw2.astype(BF16), db.reshape(1, d),
            aw1.astype(BF16), aw2.astype(BF16), ab.reshape(1, d), gw1.astype(BF16),
            gw2.astype(BF16), k_k.reshape(1, d), k_a.reshape(1, d), r_k.reshape(1, d), blockdiag]


def _rwkv_proj_call(h, prev, params, time_groups):
    n, d = h.shape
    tm = _row_tile(n)
    assert d == 8 * LANES and tm % 8 == 0
    row = pl.BlockSpec((tm, d), lambda i: (i, 0))
    scan = _scan_spec(tm, time_groups)

    def full(x):
        nd = x.ndim
        return pl.BlockSpec(x.shape, lambda i: (0,) * nd)

    return pl.pallas_call(
        functools.partial(_rwkv_proj_kernel, time_groups=time_groups), grid=(pl.cdiv(n, tm),),
        in_specs=[row, row] + [full(p) for p in params],
        out_specs=[scan] * 6 + [row] * 2,
        out_shape=[jax.ShapeDtypeStruct(_scan_shape(n, time_groups), F32)] * 6
        + [jax.ShapeDtypeStruct((n, d), F32)] * 2,
        compiler_params=_cparams(("parallel",)), name="rwkv_proj")(h, prev, *params)


def _wkv_kernel(r_ref, d_ref, k_ref, v_ref, a_ref, b_ref, anext_ref, s0_ref, y_ref, s_ref, sa_ref, *, steps):
    nk = s_ref.shape[0]

    @pl.when(pl.program_id(1) == 0)
    def _():
        s_ref[...] = s0_ref[...]
        sa0 = None
        for kk in range(nk):
            term = s0_ref[kk] * a_ref[0, kk:kk + 1, :]
            sa0 = term if sa0 is None else sa0 + term
        sa_ref[...] = sa0

    def step(t, sa, next_row):
        vv = v_ref[t]
        y = None
        sa_next = None
        for kk in range(nk):
            sk = (s_ref[kk] * d_ref[t, kk:kk + 1, :] + sa * b_ref[t, kk:kk + 1, :]
                  + vv * k_ref[t, kk:kk + 1, :])
            s_ref[kk] = sk
            ty = sk * r_ref[t, kk:kk + 1, :]
            ta = sk * next_row(kk)
            y = ty if y is None else y + ty
            sa_next = ta if sa_next is None else sa_next + ta
        y_ref[t] = y
        return sa_next

    sa = lax.fori_loop(0, steps - 1, lambda t, sa: step(t, sa, lambda kk: a_ref[t + 1, kk:kk + 1, :]),
                       sa_ref[...])
    sa_ref[...] = step(steps - 1, sa, lambda kk: anext_ref[0, kk:kk + 1, :])


def _wkv_call(r, dec, k, v, aa, bb, s0, steps):
    t_len, hd, lanes = r.shape
    seq = pl.BlockSpec((steps, hd, LANES), lambda l, t: (t, 0, l))
    nxt = pl.BlockSpec((1, hd, LANES), lambda l, t: (jnp.minimum((t + 1) * steps, t_len - 1), 0, l))
    st = pl.BlockSpec((hd, hd, LANES), lambda l, t: (0, 0, l))
    return pl.pallas_call(
        functools.partial(_wkv_kernel, steps=steps),
        grid=(lanes // LANES, t_len // steps),
        in_specs=[seq] * 6 + [nxt, st],
        out_specs=[seq, st],
        out_shape=[jax.ShapeDtypeStruct((t_len, hd, lanes), F32),
                   jax.ShapeDtypeStruct((hd, hd, lanes), F32)],
        scratch_shapes=[pltpu.VMEM((hd, LANES), F32)],
        compiler_params=_cparams(("parallel", "arbitrary")), name="wkv_scan")(r, dec, k, v, aa, bb, aa, s0)


def _wkv_group(scan_inputs, s0, steps):
    y, s_l = _wkv_call(*scan_inputs, _state_to_scan(s0), steps)
    return y, _state_from_scan(s_l, s0.shape[0])


def _rwkv_out_kernel(y_ref, bonus_ref, g_ref, x_ref, lng_ref, lnb_ref, wo_ref, bd_ref, o_ref, *, time_groups):
    bd = bd_ref[...]
    y = _load_scan(y_ref, time_groups)
    inv_n = 1.0 / HEAD_DIM
    mu = _head_sum(y, bd) * inv_n
    yc = y - mu
    var = _head_sum(yc * yc, bd) * inv_n
    yn = yc * lax.rsqrt(var + GN_EPS) * lng_ref[...] + lnb_ref[...]
    o_ref[...] = x_ref[...] + _bdot((yn + bonus_ref[...]) * g_ref[...], wo_ref[...])


def _rwkv_out_call(y, bonus, g, x, ln_g, ln_b, w_o, blockdiag, time_groups):
    n, d = x.shape
    tm = _row_tile(n)
    row = pl.BlockSpec((tm, d), lambda i: (i, 0))
    vec = pl.BlockSpec((1, d), lambda i: (0, 0))
    return pl.pallas_call(
        functools.partial(_rwkv_out_kernel, time_groups=time_groups), grid=(pl.cdiv(n, tm),),
        in_specs=[_scan_spec(tm, time_groups), row, row, row, vec, vec,
                  pl.BlockSpec((d, d), lambda i: (0, 0)), pl.BlockSpec((LANES, LANES), lambda i: (0, 0))],
        out_specs=row, out_shape=jax.ShapeDtypeStruct((n, d), F32),
        compiler_params=_cparams(("parallel",)), name="rwkv_out")(
            y, bonus, g, x, ln_g.reshape(1, d), ln_b.reshape(1, d), w_o, blockdiag)


def _peer_scores_kernel(x_ref, g_ref, wq_ref, keys_ref, s_ref, ht_ref):
    h = _rms(x_ref[...], g_ref[...])
    ht_ref[...] = h.T.astype(BF16)
    q = _bdot(h, wq_ref[...]).astype(BF16)
    for hc in range(keys_ref.shape[0]):
        s_ref[hc] = lax.dot_general(keys_ref[hc], q[:, hc * LANES:(hc + 1) * LANES],
                                    (((1,), (1,)), ((), ())), preferred_element_type=F32)


def _peer_scores_call(x, g, wq, keys):
    n, d = x.shape
    tm = _row_tile(n)
    nhc = keys.shape[0]
    return pl.pallas_call(
        _peer_scores_kernel, grid=(pl.cdiv(n, tm),),
        in_specs=[pl.BlockSpec((tm, d), lambda i: (i, 0)), pl.BlockSpec((1, d), lambda i: (0, 0)),
                  pl.BlockSpec(wq.shape, lambda i: (0, 0)),
                  pl.BlockSpec(keys.shape, lambda i: (0, 0, 0))],
        out_specs=[pl.BlockSpec((nhc, PEER_NKEYS, tm), lambda i: (0, 0, i)),
                   pl.BlockSpec((d, tm), lambda i: (0, i))],
        out_shape=[jax.ShapeDtypeStruct((nhc, PEER_NKEYS, n), F32),
                   jax.ShapeDtypeStruct((d, n), BF16)],
        compiler_params=_cparams(("parallel",)), name="peer_scores")(x, g.reshape(1, d), wq, keys)


def _top16(s, iota, exact_ties):
    rank = jnp.full(s.shape, float(PEER_TOPK), F32)
    vals = []
    for a in range(PEER_TOPK):
        m = jnp.max(s, axis=0, keepdims=True)
        hit = s == m
        if exact_ties:
            idx = jnp.min(jnp.where(hit, iota, float(PEER_NKEYS)), axis=0, keepdims=True)
            hit = iota == idx
        rank = jnp.where(hit, float(a), rank)
        s = jnp.where(hit, NEG_INF, s)
        vals.append(m)
    ranked = jnp.sum(jnp.where(rank < float(PEER_TOPK), 1.0, 0.0), axis=0, keepdims=True)
    return rank, vals, ranked


def _dup_mask_bits(x):
    bits = lax.bitcast_convert_type(x.astype(MASK_DTYPE).astype(F32), jnp.uint32)
    return bits | (bits >> 16)


def _peer_select_kernel(s_ref, l_ref, p1_ref, r2_ref, p2_ref):
    miscount = _peer_select_body(s_ref, l_ref, p1_ref, r2_ref, p2_ref, exact_ties=False)

    @pl.when(jnp.max(miscount) > 0.0)
    def _():
        _peer_select_body(s_ref, l_ref, p1_ref, r2_ref, p2_ref, exact_ties=True)


def _peer_select_body(s_ref, l_ref, p1_ref, r2_ref, p2_ref, exact_ties):
    lanes = s_ref.shape[-1]
    iota = lax.broadcasted_iota(jnp.int32, (PEER_NKEYS, lanes), 0).astype(F32)
    iota16 = lax.broadcasted_iota(jnp.int32, (PEER_TOPK, lanes), 0).astype(F32)
    miscount = jnp.zeros((1, lanes), F32)
    for h in range(PEER_HEADS):
        s1 = s_ref[2 * h]
        s2 = s_ref[2 * h + 1]
        rank1, v1, n1 = _top16(s1, iota, exact_ties)
        rank2, v2, n2 = _top16(s2, iota, exact_ties)
        miscount = miscount + jnp.abs(n1 - float(PEER_TOPK)) + jnp.abs(n2 - float(PEER_TOPK))
        v1a = jnp.concatenate(v1, axis=0)
        v2a = jnp.concatenate(v2, axis=0)
        top = v1[0] + v2[0]
        taken = jnp.zeros((PEER_TOPK, lanes), F32)
        front = v1a + v2[0]
        zsum = jnp.zeros((1, lanes), F32)
        for _ in range(PEER_TOPK):
            m = jnp.max(front, axis=0, keepdims=True)
            a_star = jnp.min(jnp.where(front == m, iota16, float(PEER_TOPK)), axis=0, keepdims=True)
            hit = iota16 == a_star
            zsum = zsum + jnp.exp(m - top)
            taken = jnp.where(hit, taken + 1.0, taken)
            cnt = jnp.max(jnp.where(hit, taken, -1.0), axis=0, keepdims=True)
            nxt = jnp.max(jnp.where(iota16 == cnt, v2a, NEG_INF), axis=0, keepdims=True)
            front = jnp.where(hit, v1a + nxt, front)
        lim = jnp.zeros((PEER_NKEYS, lanes), F32)
        for a in range(PEER_TOPK):
            lim = jnp.where(rank1 == float(a), taken[a:a + 1, :], lim)
        l_ref[h] = _dup_mask_bits(lim)
        p1_ref[h] = _dup_mask_bits(jnp.exp(s1 - v1[0]))
        p2 = jnp.exp(s2 - v2[0]) / zsum
        for r in range(PEER_NKEYS // MASK_ROWS):
            rows = slice(r * MASK_ROWS, (r + 1) * MASK_ROWS)
            r2_ref[h, r] = rank2[rows].astype(MASK_DTYPE)
            p2_ref[h, r] = p2[rows].astype(MASK_DTYPE)
    return miscount


def _peer_select_call(scores):
    nhc, nk, n = scores.shape
    tl = LANES
    groups = nk // MASK_ROWS
    words = pl.BlockSpec((PEER_HEADS, nk, tl), lambda i: (0, 0, i))
    packed = pl.BlockSpec((PEER_HEADS, groups, MASK_ROWS, tl), lambda i: (0, 0, 0, i))
    return pl.pallas_call(
        _peer_select_kernel, grid=(pl.cdiv(n, tl),),
        in_specs=[pl.BlockSpec((nhc, nk, tl), lambda i: (0, 0, i))],
        out_specs=[words, words, packed, packed],
        out_shape=[jax.ShapeDtypeStruct((PEER_HEADS, nk, n), jnp.uint32)] * 2
        + [jax.ShapeDtypeStruct((PEER_HEADS, groups, MASK_ROWS, n), MASK_DTYPE)] * 2,
        compiler_params=_cparams(("parallel",)), name="peer_select")(scores)


def _gate_weights(cb, ii, heads, w, l_ref, p1_ref, r2_ref, p2_ref):
    blocks = PEER_CHUNK // PEER_NKEYS
    base = pl.multiple_of(cb * blocks, blocks)
    w = list(w)
    for h in heads:
        lw = l_ref[h, pl.ds(base, blocks), :]
        pw = p1_ref[h, pl.ds(base, blocks), :]
        lim = pltpu.bitcast(jnp.broadcast_to(lw[ii:ii + 1, :], lw.shape), MASK_DTYPE)
        p1 = pltpu.bitcast(jnp.broadcast_to(pw[ii:ii + 1, :], pw.shape), MASK_DTYPE)
        for r in range(len(w)):
            term = jnp.where(r2_ref[h, r] < lim, p2_ref[h, r], jnp.zeros_like(p1)) * p1
            w[r] = term if w[r] is None else w[r] + term
    return w


def _gate_store(ii, w, z_ref, g_ref):
    for r in range(len(w)):
        rows = slice(ii * PEER_NKEYS + r * MASK_ROWS, ii * PEER_NKEYS + (r + 1) * MASK_ROWS)
        z = z_ref[rows, :]
        act = 0.5 * z * (1.0 + lax.erf(z * np.float32(np.sqrt(0.5))))
        g_ref[rows, :] = (act.astype(MASK_DTYPE) * w[r]).astype(g_ref.dtype)


def _peer_dense_kernel(ht_ref, u_ref, vt_ref, l_ref, p1_ref, r2_ref, p2_ref, o_ref,
                       z0_ref, z1_ref, g0_ref, g1_ref, acc_ref, *, items, chunks):
    s = pl.program_id(0)

    @pl.when(s == 0)
    def _():
        for ref in (z0_ref, z1_ref, g0_ref, g1_ref):
            ref[...] = jnp.zeros_like(ref)

    gate_chunk = jnp.clip(s - 1, 0, items - 1) % chunks

    def stages(z_new, z_old, g_new, g_old):
        blocks = PEER_CHUNK // PEER_NKEYS
        drows = o_ref.shape[0] // blocks
        sel = (l_ref, p1_ref, r2_ref, p2_ref)
        tt = o_ref.shape[1]
        nsplit = 2 if tt % (2 * LANES) == 0 else 1
        ncols = [slice(c * (tt // nsplit), (c + 1) * (tt // nsplit)) for c in range(nsplit)]
        hq = PEER_HEADS // (2 * nsplit)
        for ii in range(blocks):
            er = slice(ii * PEER_NKEYS, (ii + 1) * PEER_NKEYS)
            dr = slice(ii * drows, (ii + 1) * drows)
            w = [None] * (PEER_NKEYS // MASK_ROWS)
            q = 0
            for nc in ncols:
                w = _gate_weights(gate_chunk, ii, range(q * hq, (q + 1) * hq), w, *sel)
                q += 1
                z_new[er, nc] = jnp.dot(u_ref[er, :], ht_ref[:, nc], preferred_element_type=F32)
            for c, nc in enumerate(ncols):
                w = _gate_weights(gate_chunk, ii, range(q * hq, (q + 1) * hq), w, *sel)
                q += 1
                if c == nsplit - 1:
                    _gate_store(ii, w, z_old, g_new)
                acc_ref[dr, nc] = jnp.dot(vt_ref[dr, :], g_old[:, nc], preferred_element_type=F32)

    @pl.when(s % 2 == 0)
    def _():
        stages(z0_ref, z1_ref, g1_ref, g0_ref)

    @pl.when(s % 2 == 1)
    def _():
        stages(z1_ref, z0_ref, g0_ref, g1_ref)

    first = jnp.clip(s - 2, 0, items - 1) % chunks == 0

    @pl.when(first)
    def _():
        o_ref[...] = acc_ref[...]

    @pl.when(jnp.logical_not(first))
    def _():
        o_ref[...] += acc_ref[...]


def _peer_dense_call(ht, u, vt, lim, p1, r2, p2):
    d, n = ht.shape
    chunks = u.shape[0] // PEER_CHUNK
    tt = min(PEER_TOKENS, n)
    items = pl.cdiv(n, tt) * chunks

    def item(s, lag):
        return jnp.clip(s - lag, 0, items - 1)

    words = pl.BlockSpec((PEER_HEADS, PEER_NKEYS, tt), lambda s: (0, 0, item(s, 1) // chunks))
    packed = pl.BlockSpec(r2.shape[:3] + (tt,), lambda s: (0, 0, 0, item(s, 1) // chunks))
    return pl.pallas_call(
        functools.partial(_peer_dense_kernel, items=items, chunks=chunks),
        grid=(items + 2,),
        in_specs=[pl.BlockSpec((d, tt), lambda s: (0, item(s, 0) // chunks)),
                  pl.BlockSpec((PEER_CHUNK, d), lambda s: (item(s, 0) % chunks, 0)),
                  pl.BlockSpec((d, PEER_CHUNK), lambda s: (0, item(s, 2) % chunks)),
                  words, words, packed, packed],
        out_specs=pl.BlockSpec((d, tt), lambda s: (0, item(s, 2) // chunks)),
        out_shape=jax.ShapeDtypeStruct((d, n), F32),
        scratch_shapes=[pltpu.VMEM((PEER_CHUNK, tt), F32)] * 2 + [pltpu.VMEM((PEER_CHUNK, tt), BF16)] * 2
        + [pltpu.VMEM((d, tt), F32)],
        compiler_params=_cparams(("arbitrary",)), name="peer_dense")(ht, u, vt, lim, p1, r2, p2)


def _peer_params(g, wq, keys, u, v):
    nh, _, nk, half = keys.shape
    return g, wq.astype(BF16), keys.reshape(nh * 2, nk, half).astype(BF16), u.astype(BF16), v.astype(BF16).T


def _peer(x, params):
    g, wq, keys, u, vt = params
    scores, ht = _peer_scores_call(x, g, wq, keys)
    lim, p1, r2, p2 = _peer_select_call(scores)
    return _peer_dense_call(ht, u, vt, lim, p1, r2, p2)


def _pool_prompt_kernel(halo_ref, cur_ref, x_ref, w_ref, sc_ref, o_ref, *, batch):
    tm = cur_ref.shape[0]
    hr = halo_ref.shape[0]
    cur = cur_ref[...]
    halo = jnp.where(pl.program_id(0) == 0, 0.0, halo_ref[...])
    ext = jnp.concatenate([halo, cur], axis=0)
    row = (pl.program_id(0) * tm + lax.broadcasted_iota(jnp.int32, (tm, 1), 0)).astype(F32)
    pos = jnp.floor(row / float(batch))
    gd = cur.shape[-1] // len(POOL_WINDOWS)
    outs = []
    for gi, w in enumerate(POOL_WINDOWS):
        acc = ext[:, gi * gd:(gi + 1) * gd]
        dropped = 0
        shift = 1
        while shift < w:
            k = shift * batch
            acc = acc[k:] + acc[:-k]
            dropped += k
            shift *= 2
        cnt = jnp.minimum(float(w), pos + 1.0)
        mixed = acc[hr - dropped:] / cnt - cur[:, gi * gd:(gi + 1) * gd]
        outs.append(_bdot(mixed, w_ref[gi]))
    o_ref[...] = x_ref[...] + jnp.concatenate(outs, axis=-1) * sc_ref[...]


def _pool_prompt_call(h, x, pool_w, pool_scale, batch):
    n, d = h.shape
    tm = _row_tile(n)
    hr = POOL_HALO * batch
    assert tm % hr == 0 and hr % 8 == 0
    row = pl.BlockSpec((tm, d), lambda i: (i, 0))
    return pl.pallas_call(
        functools.partial(_pool_prompt_kernel, batch=batch), grid=(pl.cdiv(n, tm),),
        in_specs=[pl.BlockSpec((hr, d), lambda i: (jnp.maximum(i * (tm // hr) - 1, 0), 0)), row, row,
                  pl.BlockSpec(pool_w.shape, lambda i: (0, 0, 0)),
                  pl.BlockSpec((1, d), lambda i: (0, 0))],
        out_specs=row, out_shape=jax.ShapeDtypeStruct((n, d), F32),
        compiler_params=_cparams(("parallel",)), name="pool_prompt")(
            h, h, x, pool_w, pool_scale.reshape(1, d))


def _pool_sample_kernel(hist_ref, h_ref, x_ref, w_ref, sc_ref, o_ref):
    h = h_ref[...]
    nhist = hist_ref.shape[0]
    gd = h.shape[-1] // len(POOL_WINDOWS)
    outs = []
    for gi, w in enumerate(POOL_WINDOWS):
        sl = slice(gi * gd, (gi + 1) * gd)
        acc = h[:, sl]
        for back in range(1, w):
            acc = acc + hist_ref[nhist - back][:, sl]
        mixed = acc / float(w) - h[:, sl]
        outs.append(_bdot(mixed, w_ref[gi]))
    o_ref[...] = x_ref[...] + jnp.concatenate(outs, axis=-1) * sc_ref[...]


def _pool_sample_call(hist, h, x, pool_w, pool_scale):
    b, d = h.shape
    full2 = pl.BlockSpec((b, d), lambda i: (0, 0))
    return pl.pallas_call(
        _pool_sample_kernel, grid=(1,),
        in_specs=[pl.BlockSpec(hist.shape, lambda i: (0, 0, 0)), full2, full2,
                  pl.BlockSpec(pool_w.shape, lambda i: (0, 0, 0)),
                  pl.BlockSpec((1, d), lambda i: (0, 0))],
        out_specs=full2, out_shape=jax.ShapeDtypeStruct((b, d), F32),
        compiler_params=_cparams(("arbitrary",)), name="pool_sample")(
            hist, h, x, pool_w, pool_scale.reshape(1, d))


def _time_block(t, pref):
    for c in range(min(pref, t), 0, -1):
        if t % c == 0 and (c % 8 == 0 or c == t):
            return c
    return t


def kernel(x_prompt, x_sample, state_wkv, state_shift, state_pool, meta_tokens, norm_mix, norm_ffn, norm_final, rwkv_mix, rwkv_w_rkv, rwkv_decay_w1, rwkv_decay_w2, rwkv_decay_b, rwkv_iclr_w1, rwkv_iclr_w2, rwkv_iclr_b, rwkv_gate_w1, rwkv_gate_w2, rwkv_k_k, rwkv_k_a, rwkv_r_k, rwkv_ln_g, rwkv_ln_b, rwkv_w_o, pool_w, pool_scale, peer_wq, peer_keys, peer_u, peer_v):
    bp, seq, d = x_prompt.shape
    bs = x_sample.shape[0]
    tp = N_META + seq
    n_p = bp * tp
    heads = d // HEAD_DIM
    assert x_sample.shape[1] == 1 and bp == 8 and bs % 8 == 0 and heads * 8 == LANES
    assert rwkv_w_rkv.shape[0] == 1 and pool_w.shape[0] == 1 and norm_mix.shape[0] == 2

    blockdiag = jnp.asarray(np.kron(np.eye(LANES // HEAD_DIM), np.ones((HEAD_DIM, HEAD_DIM))), BF16)
    proj_params = _rwkv_proj_params(
        rwkv_mix[0], rwkv_w_rkv[0], rwkv_decay_w1[0], rwkv_decay_w2[0], rwkv_decay_b[0], rwkv_iclr_w1[0],
        rwkv_iclr_w2[0], rwkv_iclr_b[0], rwkv_gate_w1[0], rwkv_gate_w2[0], rwkv_k_k[0], rwkv_k_a[0],
        rwkv_r_k[0], blockdiag)
    w_o = rwkv_w_o[0].astype(BF16)
    peer_params = [_peer_params(norm_ffn[i], peer_wq[i], peer_keys[i], peer_u[i], peer_v[i]) for i in range(2)]
    pool_wb = pool_w[0].astype(BF16)
    meta = jnp.broadcast_to(meta_tokens[:, None], (N_META, bp, d))
    x0p = jnp.concatenate([meta, x_prompt.transpose(1, 0, 2)], axis=0).reshape(n_p, d)
    x0s = x_sample.reshape(bs, d)

    def rwkv_layer(x0, prev_of, s0, steps, time_groups):
        h0 = _norm_call(x0, norm_mix[0])
        *scan_inputs, g, bonus = _rwkv_proj_call(h0, prev_of(h0), proj_params, time_groups)
        y, s_out = _wkv_group(scan_inputs, s0, steps)
        x1 = _rwkv_out_call(y, bonus, g, x0, rwkv_ln_g[0], rwkv_ln_b[0], w_o, blockdiag, time_groups)
        return h0, s_out, x1

    def prev_prompt(h0):
        return jnp.concatenate([jnp.zeros((bp, d), F32), h0[:n_p - bp]], axis=0)

    h0p, wkv_p, x1p = rwkv_layer(x0p, prev_prompt, jnp.zeros((bp, heads, HEAD_DIM, HEAD_DIM), F32),
                                 _time_block(tp, 48), True)
    h0s, wkv_s, x1s = rwkv_layer(x0s, lambda h0: state_shift[0], state_wkv[0], 1, False)
    f0p = _peer(x1p, peer_params[0])
    f0s = _peer(x1s, peer_params[0])

    x2p, h1p = _add_norm_call(x1p, f0p, norm_mix[1])
    x2s, h1s = _add_norm_call(x1s, f0s, norm_mix[1])
    x3p = _pool_prompt_call(h1p, x2p, pool_wb, pool_scale[0], bp)
    x3s = _pool_sample_call(state_pool[0].transpose(1, 0, 2), h1s, x2s, pool_wb, pool_scale[0])
    f1p = _peer(x3p, peer_params[1])
    f1s = _peer(x3s, peer_params[1])
    _, yfp = _add_norm_call(x3p, f1p, norm_final)
    _, yfs = _add_norm_call(x3s, f1s, norm_final)

    nbuf = state_pool.shape[2]
    y_prompt = yfp.reshape(tp, bp, d)[N_META:].transpose(1, 0, 2)
    y_sample = yfs.reshape(bs, 1, d)
    shift_p = h0p[n_p - bp:][None]
    shift_s = h0s[None]
    if tp >= nbuf:
        tail = h1p[n_p - nbuf * bp:]
    else:
        tail = jnp.concatenate([jnp.zeros(((nbuf - tp) * bp, d), F32), h1p], axis=0)
    pool_p = tail.reshape(nbuf, bp, d).transpose(1, 0, 2)[None]
    pool_s = jnp.concatenate([state_pool[0][:, 1:], h1s[:, None, :]], axis=1)[None]
    return (y_prompt, y_sample, wkv_p[None], shift_p, pool_p, wkv_s[None], shift_s, pool_s)
```

```python
import functools

import jax
import jax.numpy as jnp
import numpy as np
from jax import lax
from jax.experimental import pallas as pl
from jax.experimental.pallas import tpu as pltpu

F32 = jnp.float32
BF16 = jnp.bfloat16

N_META = 16
RMS_EPS = 1e-6
GN_EPS = 64e-5
HEAD_DIM = 64
LANES = 128
POOL_WINDOWS = (2, 4, 8, 16)
POOL_HALO = 16
PEER_HEADS = 8
PEER_NKEYS = 128
PEER_TOPK = 16
PEER_CHUNK = 1024
PEER_TOKENS = 512
MASK_DTYPE = jnp.bfloat16
MASK_ROWS = 16
ROW_TILE = 256
VMEM_LIMIT = 56 * 1024 * 1024

NEG_INF = float("-inf")


def _cparams(sem):
    return pltpu.CompilerParams(dimension_semantics=sem, vmem_limit_bytes=VMEM_LIMIT)


def _row_tile(n):
    return min(ROW_TILE, n)


def _rms(x, g):
    return x * lax.rsqrt(jnp.mean(x * x, axis=-1, keepdims=True) + RMS_EPS) * g


def _bdot(a, b):
    return jnp.dot(a.astype(BF16), b.astype(BF16), preferred_element_type=F32)


def _head_sum(x, blockdiag):
    outs = []
    for blk in range(x.shape[-1] // LANES):
        xb = x[:, blk * LANES:(blk + 1) * LANES]
        hi = xb.astype(BF16)
        lo = (xb - hi.astype(F32)).astype(BF16)
        outs.append(jnp.dot(hi, blockdiag, preferred_element_type=F32)
                    + jnp.dot(lo, blockdiag, preferred_element_type=F32))
    return jnp.concatenate(outs, axis=-1)


def _norm_kernel(x_ref, g_ref, h_ref):
    h_ref[...] = _rms(x_ref[...], g_ref[...])


def _norm_call(x, g):
    n, d = x.shape
    tm = _row_tile(n)
    row = pl.BlockSpec((tm, d), lambda i: (i, 0))
    return pl.pallas_call(
        _norm_kernel, grid=(pl.cdiv(n, tm),),
        in_specs=[row, pl.BlockSpec((1, d), lambda i: (0, 0))],
        out_specs=row, out_shape=jax.ShapeDtypeStruct((n, d), F32),
        compiler_params=_cparams(("parallel",)), name="rmsnorm")(x, g.reshape(1, d))


def _add_norm_kernel(a_ref, bt_ref, g_ref, x_ref, h_ref):
    x = a_ref[...] + bt_ref[...].T
    x_ref[...] = x
    h_ref[...] = _rms(x, g_ref[...])


def _add_norm_call(a, bt, g):
    n, d = a.shape
    tm = _row_tile(n)
    row = pl.BlockSpec((tm, d), lambda i: (i, 0))
    return pl.pallas_call(
        _add_norm_kernel, grid=(pl.cdiv(n, tm),),
        in_specs=[row, pl.BlockSpec((d, tm), lambda i: (0, i)), pl.BlockSpec((1, d), lambda i: (0, 0))],
        out_specs=[row, row],
        out_shape=[jax.ShapeDtypeStruct((n, d), F32)] * 2,
        compiler_params=_cparams(("parallel",)), name="add_rmsnorm")(a, bt, g.reshape(1, d))


def _final_norm_prompt_kernel(a_ref, bt_ref, g_ref, o_ref, h_ref, *, batch):
    h = _rms(a_ref[...] + bt_ref[...].T, g_ref[...])
    for c in range(h_ref.shape[0]):
        h_ref[c] = h[:, c * LANES:(c + 1) * LANES]

    @pl.when(pl.program_id(0) > 0)
    def _():
        steps = h_ref.shape[1] // batch
        for b in range(batch):
            for c in range(h_ref.shape[0]):
                o_ref[b, :, c * LANES:(c + 1) * LANES] = h_ref[c, pl.ds(b, steps, stride=batch), :]


def _final_norm_prompt_call(a, bt, g, batch, meta_steps):
    n, d = a.shape
    tm = meta_steps * batch
    assert n % tm == 0 and tm % LANES == 0
    return pl.pallas_call(
        functools.partial(_final_norm_prompt_kernel, batch=batch), grid=(n // tm,),
        in_specs=[pl.BlockSpec((tm, d), lambda i: (i, 0)), pl.BlockSpec((d, tm), lambda i: (0, i)),
                  pl.BlockSpec((1, d), lambda i: (0, 0))],
        out_specs=pl.BlockSpec((batch, meta_steps, d), lambda i: (0, jnp.maximum(i - 1, 0), 0)),
        out_shape=jax.ShapeDtypeStruct((batch, n // batch - meta_steps, d), F32),
        scratch_shapes=[pltpu.VMEM((d // LANES, tm, LANES), F32)],
        compiler_params=_cparams(("arbitrary",)), name="final_norm_prompt")(a, bt, g.reshape(1, d))


def _rows_to_scan(x8):
    pairs = x8.shape[-1] // LANES
    a = jnp.concatenate([x8[:, p * LANES:(p + 1) * LANES] for p in range(pairs)], axis=0)
    at = a.T
    return jnp.concatenate([at[:HEAD_DIM], at[HEAD_DIM:]], axis=1)


def _scan_to_rows(tile):
    at = jnp.concatenate([tile[:, :HEAD_DIM], tile[:, HEAD_DIM:]], axis=0)
    a = at.T
    return jnp.concatenate([a[p * 8:(p + 1) * 8, :] for p in range(a.shape[0] // 8)], axis=1)


def _store_scan(ref, x, time_groups):
    for j in range(x.shape[0] // 8):
        tile = _rows_to_scan(x[j * 8:(j + 1) * 8, :])
        if time_groups:
            ref[j] = tile
        else:
            ref[0, :, j * LANES:(j + 1) * LANES] = tile


def _load_scan(ref, time_groups):
    groups = ref.shape[0] if time_groups else ref.shape[2] // LANES
    tiles = [ref[j] if time_groups else ref[0, :, j * LANES:(j + 1) * LANES] for j in range(groups)]
    return jnp.concatenate([_scan_to_rows(t) for t in tiles], axis=0)


def _scan_spec(tm, time_groups):
    if time_groups:
        return pl.BlockSpec((tm // 8, HEAD_DIM, LANES), lambda i: (i, 0, 0))
    return pl.BlockSpec((1, HEAD_DIM, tm // 8 * LANES), lambda i: (0, 0, i))


def _scan_shape(n, time_groups):
    return (n // 8, HEAD_DIM, LANES) if time_groups else (1, HEAD_DIM, n // 8 * LANES)


def _state_to_scan(s):
    b, heads, nv, nk = s.shape
    s = s.reshape(b // 8, 8, heads // 2, 2, nv, nk).transpose(5, 4, 0, 3, 2, 1)
    return s.reshape(nk, nv, b * heads)


def _state_from_scan(s, b):
    nk, nv, lanes = s.shape
    heads = lanes // b
    return s.reshape(nk, nv, b // 8, 2, heads // 2, 8).transpose(2, 5, 4, 3, 1, 0).reshape(b, heads, nv, nk)


def _rwkv_proj_kernel(x_ref, prev_ref, gn_ref, mix_ref, wrkv_ref, dw1_ref, dw2_ref, db_ref,
                      aw1_ref, aw2_ref, ab_ref, gw1_ref, gw2_ref, kk_ref, ka_ref, rk_ref, bd_ref,
                      r_ref, dec_ref, k_ref, v_ref, aa_ref, bb_ref, g_ref, bonus_ref, *, time_groups):
    h = _rms(x_ref[...], gn_ref[...])
    if time_groups:
        before = jnp.where(pl.program_id(0) == 0, 0.0, _rms(prev_ref[...], gn_ref[...]))
        prev = jnp.concatenate([before, h[:h.shape[0] - 8]], axis=0)
    else:
        prev = prev_ref[...]
    xx = prev - h

    def xs(c):
        return h + xx * mix_ref[c:c + 1, :]

    r = _bdot(xs(0), wrkv_ref[0])
    k = _bdot(xs(1), wrkv_ref[1])
    v = _bdot(xs(2), wrkv_ref[2])
    w = -jax.nn.softplus(-(db_ref[...] + _bdot(jnp.tanh(_bdot(xs(3), dw1_ref[...])), dw2_ref[...]))) - 0.5
    a = jax.nn.sigmoid(ab_ref[...] + _bdot(_bdot(xs(4), aw1_ref[...]), aw2_ref[...]))
    g = _bdot(jax.nn.sigmoid(_bdot(xs(5), gw1_ref[...])), gw2_ref[...])
    kk = k * kk_ref[...]
    nrm = jnp.sqrt(_head_sum(kk * kk, bd_ref[...]))
    kk = kk / jnp.maximum(nrm, 1e-12)
    kf = k * (1.0 + (a - 1.0) * ka_ref[...])
    _store_scan(r_ref, r, time_groups)
    _store_scan(dec_ref, jnp.exp(-jnp.exp(w)), time_groups)
    _store_scan(k_ref, kf, time_groups)
    _store_scan(v_ref, v, time_groups)
    _store_scan(aa_ref, -kk, time_groups)
    _store_scan(bb_ref, kk * a, time_groups)
    g_ref[...] = g
    bonus_ref[...] = _head_sum(r * kf * rk_ref[...], bd_ref[...]) * v


def _rwkv_proj_params(gn, mix, w_rkv, dw1, dw2, db, aw1, aw2, ab, gw1, gw2, k_k, k_a, r_k, blockdiag):
    d = mix.shape[-1]
    return [gn.reshape(1, d), mix, w_rkv.astype(BF16), dw1.astype(BF16), dw2.astype(BF16), db.reshape(1, d),
            aw1.astype(BF16), aw2.astype(BF16), ab.reshape(1, d), gw1.astype(BF16),
            gw2.astype(BF16), k_k.reshape(1, d), k_a.reshape(1, d), r_k.reshape(1, d), blockdiag]


def _rwkv_proj_call(x, prev, params, time_groups):
    n, d = x.shape
    tm = _row_tile(n)
    assert d == 8 * LANES and tm % 8 == 0
    row = pl.BlockSpec((tm, d), lambda i: (i, 0))
    scan = _scan_spec(tm, time_groups)
    prev_spec = pl.BlockSpec((8, d), lambda i: (jnp.maximum(i * (tm // 8) - 1, 0), 0)) if time_groups else row

    def full(x):
        nd = x.ndim
        return pl.BlockSpec(x.shape, lambda i: (0,) * nd)

    return pl.pallas_call(
        functools.partial(_rwkv_proj_kernel, time_groups=time_groups), grid=(pl.cdiv(n, tm),),
        in_specs=[row, prev_spec] + [full(p) for p in params],
        out_specs=[scan] * 6 + [row] * 2,
        out_shape=[jax.ShapeDtypeStruct(_scan_shape(n, time_groups), F32)] * 6
        + [jax.ShapeDtypeStruct((n, d), F32)] * 2,
        compiler_params=_cparams(("parallel",)), name="rwkv_proj")(x, prev, *params)


def _wkv_kernel(r_ref, d_ref, k_ref, v_ref, a_ref, b_ref, anext_ref, s0_ref, y_ref, s_ref, sa_ref, *, steps):
    nk = s_ref.shape[0]

    @pl.when(pl.program_id(1) == 0)
    def _():
        s_ref[...] = s0_ref[...]
        sa0 = None
        for kk in range(nk):
            term = s0_ref[kk] * a_ref[0, kk:kk + 1, :]
            sa0 = term if sa0 is None else sa0 + term
        sa_ref[...] = sa0

    def step(t, sa, next_row):
        vv = v_ref[t]
        y = None
        sa_next = None
        for kk in range(nk):
            sk = (s_ref[kk] * d_ref[t, kk:kk + 1, :] + sa * b_ref[t, kk:kk + 1, :]
                  + vv * k_ref[t, kk:kk + 1, :])
            s_ref[kk] = sk
            ty = sk * r_ref[t, kk:kk + 1, :]
            ta = sk * next_row(kk)
            y = ty if y is None else y + ty
            sa_next = ta if sa_next is None else sa_next + ta
        y_ref[t] = y
        return sa_next

    sa = lax.fori_loop(0, steps - 1, lambda t, sa: step(t, sa, lambda kk: a_ref[t + 1, kk:kk + 1, :]),
                       sa_ref[...])
    sa_ref[...] = step(steps - 1, sa, lambda kk: anext_ref[0, kk:kk + 1, :])


def _wkv_call(r, dec, k, v, aa, bb, s0, steps):
    t_len, hd, lanes = r.shape
    seq = pl.BlockSpec((steps, hd, LANES), lambda l, t: (t, 0, l))
    nxt = pl.BlockSpec((1, hd, LANES), lambda l, t: (jnp.minimum((t + 1) * steps, t_len - 1), 0, l))
    st = pl.BlockSpec((hd, hd, LANES), lambda l, t: (0, 0, l))
    return pl.pallas_call(
        functools.partial(_wkv_kernel, steps=steps),
        grid=(lanes // LANES, t_len // steps),
        in_specs=[seq] * 6 + [nxt, st],
        out_specs=[seq, st],
        out_shape=[jax.ShapeDtypeStruct((t_len, hd, lanes), F32),
                   jax.ShapeDtypeStruct((hd, hd, lanes), F32)],
        scratch_shapes=[pltpu.VMEM((hd, LANES), F32)],
        compiler_params=_cparams(("parallel", "arbitrary")), name="wkv_scan")(r, dec, k, v, aa, bb, aa, s0)


def _wkv_group(scan_inputs, s0, steps):
    y, s_l = _wkv_call(*scan_inputs, _state_to_scan(s0), steps)
    return y, _state_from_scan(s_l, s0.shape[0])


def _rwkv_out_kernel(y_ref, bonus_ref, g_ref, x_ref, lng_ref, lnb_ref, wo_ref, bd_ref, o_ref, *, time_groups):
    bd = bd_ref[...]
    y = _load_scan(y_ref, time_groups)
    inv_n = 1.0 / HEAD_DIM
    mu = _head_sum(y, bd) * inv_n
    yc = y - mu
    var = _head_sum(yc * yc, bd) * inv_n
    yn = yc * lax.rsqrt(var + GN_EPS) * lng_ref[...] + lnb_ref[...]
    o_ref[...] = x_ref[...] + _bdot((yn + bonus_ref[...]) * g_ref[...], wo_ref[...])


def _rwkv_out_call(y, bonus, g, x, ln_g, ln_b, w_o, blockdiag, time_groups):
    n, d = x.shape
    tm = _row_tile(n)
    row = pl.BlockSpec((tm, d), lambda i: (i, 0))
    vec = pl.BlockSpec((1, d), lambda i: (0, 0))
    return pl.pallas_call(
        functools.partial(_rwkv_out_kernel, time_groups=time_groups), grid=(pl.cdiv(n, tm),),
        in_specs=[_scan_spec(tm, time_groups), row, row, row, vec, vec,
                  pl.BlockSpec((d, d), lambda i: (0, 0)), pl.BlockSpec((LANES, LANES), lambda i: (0, 0))],
        out_specs=row, out_shape=jax.ShapeDtypeStruct((n, d), F32),
        compiler_params=_cparams(("parallel",)), name="rwkv_out")(
            y, bonus, g, x, ln_g.reshape(1, d), ln_b.reshape(1, d), w_o, blockdiag)


def _peer_scores_kernel(x_ref, g_ref, wq_ref, keys_ref, s_ref, ht_ref):
    h = _rms(x_ref[...], g_ref[...])
    ht_ref[...] = h.T.astype(BF16)
    q = _bdot(h, wq_ref[...]).astype(BF16)
    for hc in range(keys_ref.shape[0]):
        s_ref[hc] = lax.dot_general(keys_ref[hc], q[:, hc * LANES:(hc + 1) * LANES],
                                    (((1,), (1,)), ((), ())), preferred_element_type=F32)


def _peer_scores_call(x, g, wq, keys):
    n, d = x.shape
    tm = _row_tile(n)
    nhc = keys.shape[0]
    return pl.pallas_call(
        _peer_scores_kernel, grid=(pl.cdiv(n, tm),),
        in_specs=[pl.BlockSpec((tm, d), lambda i: (i, 0)), pl.BlockSpec((1, d), lambda i: (0, 0)),
                  pl.BlockSpec(wq.shape, lambda i: (0, 0)),
                  pl.BlockSpec(keys.shape, lambda i: (0, 0, 0))],
        out_specs=[pl.BlockSpec((nhc, PEER_NKEYS, tm), lambda i: (0, 0, i)),
                   pl.BlockSpec((d, tm), lambda i: (0, i))],
        out_shape=[jax.ShapeDtypeStruct((nhc, PEER_NKEYS, n), F32),
                   jax.ShapeDtypeStruct((d, n), BF16)],
        compiler_params=_cparams(("parallel",)), name="peer_scores")(x, g.reshape(1, d), wq, keys)


def _top16(s, iota, exact_ties):
    rank = jnp.full(s.shape, float(PEER_TOPK), F32)
    vals = []
    for a in range(PEER_TOPK):
        m = jnp.max(s, axis=0, keepdims=True)
        hit = s == m
        if exact_ties:
            idx = jnp.min(jnp.where(hit, iota, float(PEER_NKEYS)), axis=0, keepdims=True)
            hit = iota == idx
        rank = jnp.where(hit, float(a), rank)
        s = jnp.where(hit, NEG_INF, s)
        vals.append(m)
    ranked = jnp.sum(jnp.where(rank < float(PEER_TOPK), 1.0, 0.0), axis=0, keepdims=True)
    return rank, vals, ranked


def _dup_mask_bits(x):
    bits = lax.bitcast_convert_type(x.astype(MASK_DTYPE).astype(F32), jnp.uint32)
    return bits | (bits >> 16)


def _peer_select_kernel(s_ref, l_ref, p1_ref, r2_ref, p2_ref):
    miscount = _peer_select_body(s_ref, l_ref, p1_ref, r2_ref, p2_ref, exact_ties=False)

    @pl.when(jnp.max(miscount) > 0.0)
    def _():
        _peer_select_body(s_ref, l_ref, p1_ref, r2_ref, p2_ref, exact_ties=True)


def _peer_select_body(s_ref, l_ref, p1_ref, r2_ref, p2_ref, exact_ties):
    lanes = s_ref.shape[-1]
    iota = lax.broadcasted_iota(jnp.int32, (PEER_NKEYS, lanes), 0).astype(F32)
    iota16 = lax.broadcasted_iota(jnp.int32, (PEER_TOPK, lanes), 0).astype(F32)
    miscount = jnp.zeros((1, lanes), F32)
    for h in range(PEER_HEADS):
        s1 = s_ref[2 * h]
        s2 = s_ref[2 * h + 1]
        rank1, v1, n1 = _top16(s1, iota, exact_ties)
        rank2, v2, n2 = _top16(s2, iota, exact_ties)
        miscount = miscount + jnp.abs(n1 - float(PEER_TOPK)) + jnp.abs(n2 - float(PEER_TOPK))
        v1a = jnp.concatenate(v1, axis=0)
        v2a = jnp.concatenate(v2, axis=0)
        top = v1[0] + v2[0]
        taken = jnp.zeros((PEER_TOPK, lanes), F32)
        front = v1a + v2[0]
        zsum = jnp.zeros((1, lanes), F32)
        for _ in range(PEER_TOPK):
            m = jnp.max(front, axis=0, keepdims=True)
            a_star = jnp.min(jnp.where(front == m, iota16, float(PEER_TOPK)), axis=0, keepdims=True)
            hit = iota16 == a_star
            zsum = zsum + jnp.exp(m - top)
            taken = jnp.where(hit, taken + 1.0, taken)
            cnt = jnp.max(jnp.where(hit, taken, -1.0), axis=0, keepdims=True)
            nxt = jnp.max(jnp.where(iota16 == cnt, v2a, NEG_INF), axis=0, keepdims=True)
            front = jnp.where(hit, v1a + nxt, front)
        lim = jnp.zeros((PEER_NKEYS, lanes), F32)
        for a in range(PEER_TOPK):
            lim = jnp.where(rank1 == float(a), taken[a:a + 1, :], lim)
        l_ref[h] = _dup_mask_bits(lim)
        p1_ref[h] = _dup_mask_bits(jnp.exp(s1 - v1[0]))
        p2 = jnp.exp(s2 - v2[0]) / zsum
        for r in range(PEER_NKEYS // MASK_ROWS):
            rows = slice(r * MASK_ROWS, (r + 1) * MASK_ROWS)
            r2_ref[h, r] = rank2[rows].astype(MASK_DTYPE)
            p2_ref[h, r] = p2[rows].astype(MASK_DTYPE)
    return miscount


def _peer_select_call(scores):
    nhc, nk, n = scores.shape
    tl = LANES
    groups = nk // MASK_ROWS
    words = pl.BlockSpec((PEER_HEADS, nk, tl), lambda i: (0, 0, i))
    packed = pl.BlockSpec((PEER_HEADS, groups, MASK_ROWS, tl), lambda i: (0, 0, 0, i))
    return pl.pallas_call(
        _peer_select_kernel, grid=(pl.cdiv(n, tl),),
        in_specs=[pl.BlockSpec((nhc, nk, tl), lambda i: (0, 0, i))],
        out_specs=[words, words, packed, packed],
        out_shape=[jax.ShapeDtypeStruct((PEER_HEADS, nk, n), jnp.uint32)] * 2
        + [jax.ShapeDtypeStruct((PEER_HEADS, groups, MASK_ROWS, n), MASK_DTYPE)] * 2,
        compiler_params=_cparams(("parallel",)), name="peer_select")(scores)


def _gate_weights(cb, ii, heads, w, l_ref, p1_ref, r2_ref, p2_ref):
    blocks = PEER_CHUNK // PEER_NKEYS
    base = pl.multiple_of(cb * blocks, blocks)
    w = list(w)
    for h in heads:
        lw = l_ref[h, pl.ds(base, blocks), :]
        pw = p1_ref[h, pl.ds(base, blocks), :]
        lim = pltpu.bitcast(jnp.broadcast_to(lw[ii:ii + 1, :], lw.shape), MASK_DTYPE)
        p1 = pltpu.bitcast(jnp.broadcast_to(pw[ii:ii + 1, :], pw.shape), MASK_DTYPE)
        for r in range(len(w)):
            term = jnp.where(r2_ref[h, r] < lim, p2_ref[h, r], jnp.zeros_like(p1)) * p1
            w[r] = term if w[r] is None else w[r] + term
    return w


def _gate_store(ii, w, z_ref, g_ref):
    for r in range(len(w)):
        rows = slice(ii * PEER_NKEYS + r * MASK_ROWS, ii * PEER_NKEYS + (r + 1) * MASK_ROWS)
        z = z_ref[rows, :]
        act = 0.5 * z * (1.0 + lax.erf(z * np.float32(np.sqrt(0.5))))
        g_ref[rows, :] = (act.astype(MASK_DTYPE) * w[r]).astype(g_ref.dtype)


def _peer_dense_kernel(ht_ref, u_ref, v_ref, l_ref, p1_ref, r2_ref, p2_ref, o_ref,
                       z0_ref, z1_ref, g0_ref, g1_ref, acc_ref, *, items, chunks):
    s = pl.program_id(0)

    @pl.when(s == 0)
    def _():
        for ref in (z0_ref, z1_ref, g0_ref, g1_ref):
            ref[...] = jnp.zeros_like(ref)

    gate_chunk = jnp.clip(s - 1, 0, items - 1) % chunks

    def stages(z_new, z_old, g_new, g_old):
        blocks = PEER_CHUNK // PEER_NKEYS
        drows = o_ref.shape[0] // blocks
        sel = (l_ref, p1_ref, r2_ref, p2_ref)
        half = PEER_HEADS // 2
        for ii in range(blocks):
            er = slice(ii * PEER_NKEYS, (ii + 1) * PEER_NKEYS)
            dr = slice(ii * drows, (ii + 1) * drows)
            w = _gate_weights(gate_chunk, ii, range(half), [None] * (PEER_NKEYS // MASK_ROWS), *sel)
            z_new[er, :] = jnp.dot(u_ref[er, :], ht_ref[...], preferred_element_type=F32)
            w = _gate_weights(gate_chunk, ii, range(half, PEER_HEADS), w, *sel)
            _gate_store(ii, w, z_old, g_new)
            acc_ref[dr, :] = lax.dot_general(v_ref[:, dr], g_old[...], (((0,), (0,)), ((), ())),
                                             preferred_element_type=F32)

    @pl.when(s % 2 == 0)
    def _():
        stages(z0_ref, z1_ref, g1_ref, g0_ref)

    @pl.when(s % 2 == 1)
    def _():
        stages(z1_ref, z0_ref, g0_ref, g1_ref)

    first = jnp.clip(s - 2, 0, items - 1) % chunks == 0

    @pl.when(first)
    def _():
        o_ref[...] = acc_ref[...]

    @pl.when(jnp.logical_not(first))
    def _():
        o_ref[...] += acc_ref[...]


def _peer_dense_call(ht, u, v, lim, p1, r2, p2):
    d, n = ht.shape
    chunks = u.shape[0] // PEER_CHUNK
    tt = min(PEER_TOKENS, n)
    items = pl.cdiv(n, tt) * chunks

    def item(s, lag):
        return jnp.clip(s - lag, 0, items - 1)

    words = pl.BlockSpec((PEER_HEADS, PEER_NKEYS, tt), lambda s: (0, 0, item(s, 1) // chunks))
    packed = pl.BlockSpec(r2.shape[:3] + (tt,), lambda s: (0, 0, 0, item(s, 1) // chunks))
    return pl.pallas_call(
        functools.partial(_peer_dense_kernel, items=items, chunks=chunks),
        grid=(items + 2,),
        in_specs=[pl.BlockSpec((d, tt), lambda s: (0, item(s, 0) // chunks)),
                  pl.BlockSpec((PEER_CHUNK, d), lambda s: (item(s, 0) % chunks, 0)),
                  pl.BlockSpec((PEER_CHUNK, d), lambda s: (item(s, 2) % chunks, 0)),
                  words, words, packed, packed],
        out_specs=pl.BlockSpec((d, tt), lambda s: (0, item(s, 2) // chunks)),
        out_shape=jax.ShapeDtypeStruct((d, n), F32),
        scratch_shapes=[pltpu.VMEM((PEER_CHUNK, tt), F32)] * 2 + [pltpu.VMEM((PEER_CHUNK, tt), BF16)] * 2
        + [pltpu.VMEM((d, tt), F32)],
        compiler_params=_cparams(("arbitrary",)), name="peer_dense")(ht, u, v, lim, p1, r2, p2)


def _peer_params(g, wq, keys, u, v):
    nh, _, nk, half = keys.shape
    return g, wq.astype(BF16), keys.reshape(nh * 2, nk, half).astype(BF16), u.astype(BF16), v.astype(BF16)


def _peer(x, params):
    g, wq, keys, u, v = params
    scores, ht = _peer_scores_call(x, g, wq, keys)
    lim, p1, r2, p2 = _peer_select_call(scores)
    return _peer_dense_call(ht, u, v, lim, p1, r2, p2)


def _pool_prompt_kernel(halo_ref, cur_ref, x_ref, w_ref, sc_ref, o_ref, *, batch):
    tm = cur_ref.shape[0]
    hr = halo_ref.shape[0]
    cur = cur_ref[...]
    halo = jnp.where(pl.program_id(0) == 0, 0.0, halo_ref[...])
    ext = jnp.concatenate([halo, cur], axis=0)
    row = (pl.program_id(0) * tm + lax.broadcasted_iota(jnp.int32, (tm, 1), 0)).astype(F32)
    pos = jnp.floor(row / float(batch))
    gd = cur.shape[-1] // len(POOL_WINDOWS)
    outs = []
    for gi, w in enumerate(POOL_WINDOWS):
        acc = ext[:, gi * gd:(gi + 1) * gd]
        dropped = 0
        shift = 1
        while shift < w:
            k = shift * batch
            acc = acc[k:] + acc[:-k]
            dropped += k
            shift *= 2
        cnt = jnp.minimum(float(w), pos + 1.0)
        mixed = acc[hr - dropped:] / cnt - cur[:, gi * gd:(gi + 1) * gd]
        outs.append(_bdot(mixed, w_ref[gi]))
    o_ref[...] = x_ref[...] + jnp.concatenate(outs, axis=-1) * sc_ref[...]


def _pool_prompt_call(h, x, pool_w, pool_scale, batch):
    n, d = h.shape
    tm = _row_tile(n)
    hr = POOL_HALO * batch
    assert tm % hr == 0 and hr % 8 == 0
    row = pl.BlockSpec((tm, d), lambda i: (i, 0))
    return pl.pallas_call(
        functools.partial(_pool_prompt_kernel, batch=batch), grid=(pl.cdiv(n, tm),),
        in_specs=[pl.BlockSpec((hr, d), lambda i: (jnp.maximum(i * (tm // hr) - 1, 0), 0)), row, row,
                  pl.BlockSpec(pool_w.shape, lambda i: (0, 0, 0)),
                  pl.BlockSpec((1, d), lambda i: (0, 0))],
        out_specs=row, out_shape=jax.ShapeDtypeStruct((n, d), F32),
        compiler_params=_cparams(("parallel",)), name="pool_prompt")(
            h, h, x, pool_w, pool_scale.reshape(1, d))


def _pool_sample_kernel(hist_ref, h_ref, x_ref, w_ref, sc_ref, o_ref):
    h = h_ref[...]
    nhist = hist_ref.shape[0]
    gd = h.shape[-1] // len(POOL_WINDOWS)
    outs = []
    for gi, w in enumerate(POOL_WINDOWS):
        sl = slice(gi * gd, (gi + 1) * gd)
        acc = h[:, sl]
        for back in range(1, w):
            acc = acc + hist_ref[nhist - back][:, sl]
        mixed = acc / float(w) - h[:, sl]
        outs.append(_bdot(mixed, w_ref[gi]))
    o_ref[...] = x_ref[...] + jnp.concatenate(outs, axis=-1) * sc_ref[...]


def _pool_sample_call(hist, h, x, pool_w, pool_scale):
    b, d = h.shape
    full2 = pl.BlockSpec((b, d), lambda i: (0, 0))
    return pl.pallas_call(
        _pool_sample_kernel, grid=(1,),
        in_specs=[pl.BlockSpec(hist.shape, lambda i: (0, 0, 0)), full2, full2,
                  pl.BlockSpec(pool_w.shape, lambda i: (0, 0, 0)),
                  pl.BlockSpec((1, d), lambda i: (0, 0))],
        out_specs=full2, out_shape=jax.ShapeDtypeStruct((b, d), F32),
        compiler_params=_cparams(("arbitrary",)), name="pool_sample")(
            hist, h, x, pool_w, pool_scale.reshape(1, d))


def _time_block(t, pref):
    for c in range(min(pref, t), 0, -1):
        if t % c == 0 and (c % 8 == 0 or c == t):
            return c
    return t


def kernel(x_prompt, x_sample, state_wkv, state_shift, state_pool, meta_tokens, norm_mix, norm_ffn, norm_final, rwkv_mix, rwkv_w_rkv, rwkv_decay_w1, rwkv_decay_w2, rwkv_decay_b, rwkv_iclr_w1, rwkv_iclr_w2, rwkv_iclr_b, rwkv_gate_w1, rwkv_gate_w2, rwkv_k_k, rwkv_k_a, rwkv_r_k, rwkv_ln_g, rwkv_ln_b, rwkv_w_o, pool_w, pool_scale, peer_wq, peer_keys, peer_u, peer_v):
    bp, seq, d = x_prompt.shape
    bs = x_sample.shape[0]
    tp = N_META + seq
    n_p = bp * tp
    heads = d // HEAD_DIM
    assert x_sample.shape[1] == 1 and bp == 8 and bs % 8 == 0 and heads * 8 == LANES
    assert rwkv_w_rkv.shape[0] == 1 and pool_w.shape[0] == 1 and norm_mix.shape[0] == 2

    blockdiag = jnp.asarray(np.kron(np.eye(LANES // HEAD_DIM), np.ones((HEAD_DIM, HEAD_DIM))), BF16)
    proj_params = _rwkv_proj_params(
        norm_mix[0], rwkv_mix[0], rwkv_w_rkv[0], rwkv_decay_w1[0], rwkv_decay_w2[0], rwkv_decay_b[0], rwkv_iclr_w1[0],
        rwkv_iclr_w2[0], rwkv_iclr_b[0], rwkv_gate_w1[0], rwkv_gate_w2[0], rwkv_k_k[0], rwkv_k_a[0],
        rwkv_r_k[0], blockdiag)
    w_o = rwkv_w_o[0].astype(BF16)
    peer_params = [_peer_params(norm_ffn[i], peer_wq[i], peer_keys[i], peer_u[i], peer_v[i]) for i in range(2)]
    pool_wb = pool_w[0].astype(BF16)
    meta = jnp.broadcast_to(meta_tokens[:, None], (N_META, bp, d))
    x0p = jnp.concatenate([meta, x_prompt.transpose(1, 0, 2)], axis=0).reshape(n_p, d)
    x0s = x_sample.reshape(bs, d)

    def rwkv_layer(x0, prev, s0, steps, time_groups):
        *scan_inputs, g, bonus = _rwkv_proj_call(x0, prev, proj_params, time_groups)
        y, s_out = _wkv_group(scan_inputs, s0, steps)
        x1 = _rwkv_out_call(y, bonus, g, x0, rwkv_ln_g[0], rwkv_ln_b[0], w_o, blockdiag, time_groups)
        return s_out, x1

    wkv_p, x1p = rwkv_layer(x0p, x0p, jnp.zeros((bp, heads, HEAD_DIM, HEAD_DIM), F32),
                            _time_block(tp, 48), True)
    wkv_s, x1s = rwkv_layer(x0s, state_shift[0], state_wkv[0], 1, False)
    f0p = _peer(x1p, peer_params[0])
    f0s = _peer(x1s, peer_params[0])

    x2p, h1p = _add_norm_call(x1p, f0p, norm_mix[1])
    x2s, h1s = _add_norm_call(x1s, f0s, norm_mix[1])
    x3p = _pool_prompt_call(h1p, x2p, pool_wb, pool_scale[0], bp)
    x3s = _pool_sample_call(state_pool[0].transpose(1, 0, 2), h1s, x2s, pool_wb, pool_scale[0])
    f1p = _peer(x3p, peer_params[1])
    f1s = _peer(x3s, peer_params[1])
    y_prompt = _final_norm_prompt_call(x3p, f1p, norm_final, bp, N_META)
    _, yfs = _add_norm_call(x3s, f1s, norm_final)

    nbuf = state_pool.shape[2]
    y_sample = yfs.reshape(bs, 1, d)
    shift_p = _norm_call(x0p[n_p - bp:], norm_mix[0])[None]
    shift_s = _norm_call(x0s, norm_mix[0])[None]
    if tp >= nbuf:
        tail = h1p[n_p - nbuf * bp:]
    else:
        tail = jnp.concatenate([jnp.zeros(((nbuf - tp) * bp, d), F32), h1p], axis=0)
    pool_p = tail.reshape(nbuf, bp, d).transpose(1, 0, 2)[None]
    pool_s = jnp.concatenate([state_pool[0][:, 1:], h1s[:, None, :]], axis=1)[None]
    return (y_prompt, y_sample, wkv_p[None], shift_p, pool_p, wkv_s[None], shift_s, pool_s)
```

```python
import functools

import jax
import jax.numpy as jnp
import numpy as np
from jax import lax
from jax.experimental import pallas as pl
from jax.experimental.pallas import tpu as pltpu

F32 = jnp.float32
BF16 = jnp.bfloat16

N_META = 16
RMS_EPS = 1e-6
GN_EPS = 64e-5
HEAD_DIM = 64
LANES = 128
POOL_WINDOWS = (2, 4, 8, 16)
POOL_HALO = 16
PEER_HEADS = 8
PEER_NKEYS = 128
PEER_TOPK = 16
PEER_CHUNK = 2048
PEER_TOKENS = 512
MASK_DTYPE = jnp.bfloat16
MASK_ROWS = 16
ROW_TILE = 256
VMEM_LIMIT = 56 * 1024 * 1024

NEG_INF = float("-inf")


def _cparams(sem):
    return pltpu.CompilerParams(dimension_semantics=sem, vmem_limit_bytes=VMEM_LIMIT)


def _row_tile(n):
    return min(ROW_TILE, n)


def _rms(x, g):
    return x * lax.rsqrt(jnp.mean(x * x, axis=-1, keepdims=True) + RMS_EPS) * g


def _bdot(a, b):
    return jnp.dot(a.astype(BF16), b.astype(BF16), preferred_element_type=F32)


def _head_sum(x, blockdiag):
    outs = []
    for blk in range(x.shape[-1] // LANES):
        xb = x[:, blk * LANES:(blk + 1) * LANES]
        hi = xb.astype(BF16)
        lo = (xb - hi.astype(F32)).astype(BF16)
        outs.append(jnp.dot(hi, blockdiag, preferred_element_type=F32)
                    + jnp.dot(lo, blockdiag, preferred_element_type=F32))
    return jnp.concatenate(outs, axis=-1)


def _norm_kernel(x_ref, g_ref, h_ref):
    h_ref[...] = _rms(x_ref[...], g_ref[...])


def _norm_call(x, g):
    n, d = x.shape
    tm = _row_tile(n)
    row = pl.BlockSpec((tm, d), lambda i: (i, 0))
    return pl.pallas_call(
        _norm_kernel, grid=(pl.cdiv(n, tm),),
        in_specs=[row, pl.BlockSpec((1, d), lambda i: (0, 0))],
        out_specs=row, out_shape=jax.ShapeDtypeStruct((n, d), F32),
        compiler_params=_cparams(("parallel",)), name="rmsnorm")(x, g.reshape(1, d))


def _add_norm_kernel(a_ref, bt_ref, g_ref, x_ref, h_ref):
    x = a_ref[...] + bt_ref[...].T
    x_ref[...] = x
    h_ref[...] = _rms(x, g_ref[...])


def _add_norm_call(a, bt, g):
    n, d = a.shape
    tm = _row_tile(n)
    row = pl.BlockSpec((tm, d), lambda i: (i, 0))
    return pl.pallas_call(
        _add_norm_kernel, grid=(pl.cdiv(n, tm),),
        in_specs=[row, pl.BlockSpec((d, tm), lambda i: (0, i)), pl.BlockSpec((1, d), lambda i: (0, 0))],
        out_specs=[row, row],
        out_shape=[jax.ShapeDtypeStruct((n, d), F32)] * 2,
        compiler_params=_cparams(("parallel",)), name="add_rmsnorm")(a, bt, g.reshape(1, d))


def _final_norm_prompt_kernel(a_ref, bt_ref, g_ref, o_ref, h_ref, *, batch):
    h = _rms(a_ref[...] + bt_ref[...].T, g_ref[...])
    for c in range(h_ref.shape[0]):
        h_ref[c] = h[:, c * LANES:(c + 1) * LANES]

    @pl.when(pl.program_id(0) > 0)
    def _():
        steps = h_ref.shape[1] // batch
        for b in range(batch):
            for c in range(h_ref.shape[0]):
                o_ref[b, :, c * LANES:(c + 1) * LANES] = h_ref[c, pl.ds(b, steps, stride=batch), :]


def _final_norm_prompt_call(a, bt, g, batch, meta_steps):
    n, d = a.shape
    tm = meta_steps * batch
    assert n % tm == 0 and tm % LANES == 0
    return pl.pallas_call(
        functools.partial(_final_norm_prompt_kernel, batch=batch), grid=(n // tm,),
        in_specs=[pl.BlockSpec((tm, d), lambda i: (i, 0)), pl.BlockSpec((d, tm), lambda i: (0, i)),
                  pl.BlockSpec((1, d), lambda i: (0, 0))],
        out_specs=pl.BlockSpec((batch, meta_steps, d), lambda i: (0, jnp.maximum(i - 1, 0), 0)),
        out_shape=jax.ShapeDtypeStruct((batch, n // batch - meta_steps, d), F32),
        scratch_shapes=[pltpu.VMEM((d // LANES, tm, LANES), F32)],
        compiler_params=_cparams(("arbitrary",)), name="final_norm_prompt")(a, bt, g.reshape(1, d))


def _rows_to_scan(x8):
    pairs = x8.shape[-1] // LANES
    a = jnp.concatenate([x8[:, p * LANES:(p + 1) * LANES] for p in range(pairs)], axis=0)
    at = a.T
    return jnp.concatenate([at[:HEAD_DIM], at[HEAD_DIM:]], axis=1)


def _scan_to_rows(tile):
    at = jnp.concatenate([tile[:, :HEAD_DIM], tile[:, HEAD_DIM:]], axis=0)
    a = at.T
    return jnp.concatenate([a[p * 8:(p + 1) * 8, :] for p in range(a.shape[0] // 8)], axis=1)


def _store_scan(ref, x, time_groups):
    for j in range(x.shape[0] // 8):
        tile = _rows_to_scan(x[j * 8:(j + 1) * 8, :])
        if time_groups:
            ref[j] = tile
        else:
            ref[0, :, j * LANES:(j + 1) * LANES] = tile


def _load_scan(ref, time_groups):
    groups = ref.shape[0] if time_groups else ref.shape[2] // LANES
    tiles = [ref[j] if time_groups else ref[0, :, j * LANES:(j + 1) * LANES] for j in range(groups)]
    return jnp.concatenate([_scan_to_rows(t) for t in tiles], axis=0)


def _scan_spec(tm, time_groups):
    if time_groups:
        return pl.BlockSpec((tm // 8, HEAD_DIM, LANES), lambda i: (i, 0, 0))
    return pl.BlockSpec((1, HEAD_DIM, tm // 8 * LANES), lambda i: (0, 0, i))


def _scan_shape(n, time_groups):
    return (n // 8, HEAD_DIM, LANES) if time_groups else (1, HEAD_DIM, n // 8 * LANES)


def _state_to_scan(s):
    b, heads, nv, nk = s.shape
    s = s.reshape(b // 8, 8, heads // 2, 2, nv, nk).transpose(5, 4, 0, 3, 2, 1)
    return s.reshape(nk, nv, b * heads)


def _state_from_scan(s, b):
    nk, nv, lanes = s.shape
    heads = lanes // b
    return s.reshape(nk, nv, b // 8, 2, heads // 2, 8).transpose(2, 5, 4, 3, 1, 0).reshape(b, heads, nv, nk)


def _rwkv_proj_kernel(x_ref, prev_ref, gn_ref, mix_ref, wrkv_ref, dw1_ref, dw2_ref, db_ref,
                      aw1_ref, aw2_ref, ab_ref, gw1_ref, gw2_ref, kk_ref, ka_ref, rk_ref, bd_ref,
                      r_ref, dec_ref, k_ref, v_ref, aa_ref, bb_ref, g_ref, bonus_ref, *, time_groups):
    h = _rms(x_ref[...], gn_ref[...])
    if time_groups:
        before = jnp.where(pl.program_id(0) == 0, 0.0, _rms(prev_ref[...], gn_ref[...]))
        prev = jnp.concatenate([before, h[:h.shape[0] - 8]], axis=0)
    else:
        prev = prev_ref[...]
    xx = prev - h

    def xs(c):
        return h + xx * mix_ref[c:c + 1, :]

    r = _bdot(xs(0), wrkv_ref[0])
    k = _bdot(xs(1), wrkv_ref[1])
    v = _bdot(xs(2), wrkv_ref[2])
    w = -jax.nn.softplus(-(db_ref[...] + _bdot(jnp.tanh(_bdot(xs(3), dw1_ref[...])), dw2_ref[...]))) - 0.5
    a = jax.nn.sigmoid(ab_ref[...] + _bdot(_bdot(xs(4), aw1_ref[...]), aw2_ref[...]))
    g = _bdot(jax.nn.sigmoid(_bdot(xs(5), gw1_ref[...])), gw2_ref[...])
    kk = k * kk_ref[...]
    nrm = jnp.sqrt(_head_sum(kk * kk, bd_ref[...]))
    kk = kk / jnp.maximum(nrm, 1e-12)
    kf = k * (1.0 + (a - 1.0) * ka_ref[...])
    _store_scan(r_ref, r, time_groups)
    _store_scan(dec_ref, jnp.exp(-jnp.exp(w)), time_groups)
    _store_scan(k_ref, kf, time_groups)
    _store_scan(v_ref, v, time_groups)
    _store_scan(aa_ref, -kk, time_groups)
    _store_scan(bb_ref, kk * a, time_groups)
    g_ref[...] = g
    bonus_ref[...] = _head_sum(r * kf * rk_ref[...], bd_ref[...]) * v


def _rwkv_proj_params(gn, mix, w_rkv, dw1, dw2, db, aw1, aw2, ab, gw1, gw2, k_k, k_a, r_k, blockdiag):
    d = mix.shape[-1]
    return [gn.reshape(1, d), mix, w_rkv.astype(BF16), dw1.astype(BF16), dw2.astype(BF16), db.reshape(1, d),
            aw1.astype(BF16), aw2.astype(BF16), ab.reshape(1, d), gw1.astype(BF16),
            gw2.astype(BF16), k_k.reshape(1, d), k_a.reshape(1, d), r_k.reshape(1, d), blockdiag]


def _rwkv_proj_call(x, prev, params, time_groups):
    n, d = x.shape
    tm = _row_tile(n)
    assert d == 8 * LANES and tm % 8 == 0
    row = pl.BlockSpec((tm, d), lambda i: (i, 0))
    scan = _scan_spec(tm, time_groups)
    prev_spec = pl.BlockSpec((8, d), lambda i: (jnp.maximum(i * (tm // 8) - 1, 0), 0)) if time_groups else row

    def full(x):
        nd = x.ndim
        return pl.BlockSpec(x.shape, lambda i: (0,) * nd)

    return pl.pallas_call(
        functools.partial(_rwkv_proj_kernel, time_groups=time_groups), grid=(pl.cdiv(n, tm),),
        in_specs=[row, prev_spec] + [full(p) for p in params],
        out_specs=[scan] * 6 + [row] * 2,
        out_shape=[jax.ShapeDtypeStruct(_scan_shape(n, time_groups), F32)] * 6
        + [jax.ShapeDtypeStruct((n, d), F32)] * 2,
        compiler_params=_cparams(("parallel",)), name="rwkv_proj")(x, prev, *params)


def _wkv_kernel(r_ref, d_ref, k_ref, v_ref, a_ref, b_ref, anext_ref, s0_ref, y_ref, s_ref, sa_ref, *, steps):
    nk = s_ref.shape[0]

    @pl.when(pl.program_id(1) == 0)
    def _():
        s_ref[...] = s0_ref[...]
        sa0 = None
        for kk in range(nk):
            term = s0_ref[kk] * a_ref[0, kk:kk + 1, :]
            sa0 = term if sa0 is None else sa0 + term
        sa_ref[...] = sa0

    def step(t, sa, next_row):
        vv = v_ref[t]
        y = None
        sa_next = None
        for kk in range(nk):
            sk = (s_ref[kk] * d_ref[t, kk:kk + 1, :] + sa * b_ref[t, kk:kk + 1, :]
                  + vv * k_ref[t, kk:kk + 1, :])
            s_ref[kk] = sk
            ty = sk * r_ref[t, kk:kk + 1, :]
            ta = sk * next_row(kk)
            y = ty if y is None else y + ty
            sa_next = ta if sa_next is None else sa_next + ta
        y_ref[t] = y
        return sa_next

    sa = lax.fori_loop(0, steps - 1, lambda t, sa: step(t, sa, lambda kk: a_ref[t + 1, kk:kk + 1, :]),
                       sa_ref[...])
    sa_ref[...] = step(steps - 1, sa, lambda kk: anext_ref[0, kk:kk + 1, :])


def _wkv_call(r, dec, k, v, aa, bb, s0, steps):
    t_len, hd, lanes = r.shape
    seq = pl.BlockSpec((steps, hd, LANES), lambda l, t: (t, 0, l))
    nxt = pl.BlockSpec((1, hd, LANES), lambda l, t: (jnp.minimum((t + 1) * steps, t_len - 1), 0, l))
    st = pl.BlockSpec((hd, hd, LANES), lambda l, t: (0, 0, l))
    return pl.pallas_call(
        functools.partial(_wkv_kernel, steps=steps),
        grid=(lanes // LANES, t_len // steps),
        in_specs=[seq] * 6 + [nxt, st],
        out_specs=[seq, st],
        out_shape=[jax.ShapeDtypeStruct((t_len, hd, lanes), F32),
                   jax.ShapeDtypeStruct((hd, hd, lanes), F32)],
        scratch_shapes=[pltpu.VMEM((hd, LANES), F32)],
        compiler_params=_cparams(("parallel", "arbitrary")), name="wkv_scan")(r, dec, k, v, aa, bb, aa, s0)


def _wkv_group(scan_inputs, s0, steps):
    y, s_l = _wkv_call(*scan_inputs, _state_to_scan(s0), steps)
    return y, _state_from_scan(s_l, s0.shape[0])


def _rwkv_out_kernel(y_ref, bonus_ref, g_ref, x_ref, lng_ref, lnb_ref, wo_ref, bd_ref, o_ref, *, time_groups):
    bd = bd_ref[...]
    y = _load_scan(y_ref, time_groups)
    inv_n = 1.0 / HEAD_DIM
    mu = _head_sum(y, bd) * inv_n
    yc = y - mu
    var = _head_sum(yc * yc, bd) * inv_n
    yn = yc * lax.rsqrt(var + GN_EPS) * lng_ref[...] + lnb_ref[...]
    o_ref[...] = x_ref[...] + _bdot((yn + bonus_ref[...]) * g_ref[...], wo_ref[...])


def _rwkv_out_call(y, bonus, g, x, ln_g, ln_b, w_o, blockdiag, time_groups):
    n, d = x.shape
    tm = _row_tile(n)
    row = pl.BlockSpec((tm, d), lambda i: (i, 0))
    vec = pl.BlockSpec((1, d), lambda i: (0, 0))
    return pl.pallas_call(
        functools.partial(_rwkv_out_kernel, time_groups=time_groups), grid=(pl.cdiv(n, tm),),
        in_specs=[_scan_spec(tm, time_groups), row, row, row, vec, vec,
                  pl.BlockSpec((d, d), lambda i: (0, 0)), pl.BlockSpec((LANES, LANES), lambda i: (0, 0))],
        out_specs=row, out_shape=jax.ShapeDtypeStruct((n, d), F32),
        compiler_params=_cparams(("parallel",)), name="rwkv_out")(
            y, bonus, g, x, ln_g.reshape(1, d), ln_b.reshape(1, d), w_o, blockdiag)


def _peer_scores_kernel(x_ref, g_ref, wq_ref, keys_ref, s_ref, ht_ref):
    h = _rms(x_ref[...], g_ref[...])
    ht_ref[...] = h.T.astype(BF16)
    q = _bdot(h, wq_ref[...]).astype(BF16)
    for hc in range(keys_ref.shape[0]):
        s_ref[hc] = lax.dot_general(keys_ref[hc], q[:, hc * LANES:(hc + 1) * LANES],
                                    (((1,), (1,)), ((), ())), preferred_element_type=F32)


def _peer_scores_call(x, g, wq, keys):
    n, d = x.shape
    tm = _row_tile(n)
    nhc = keys.shape[0]
    return pl.pallas_call(
        _peer_scores_kernel, grid=(pl.cdiv(n, tm),),
        in_specs=[pl.BlockSpec((tm, d), lambda i: (i, 0)), pl.BlockSpec((1, d), lambda i: (0, 0)),
                  pl.BlockSpec(wq.shape, lambda i: (0, 0)),
                  pl.BlockSpec(keys.shape, lambda i: (0, 0, 0))],
        out_specs=[pl.BlockSpec((nhc, PEER_NKEYS, tm), lambda i: (0, 0, i)),
                   pl.BlockSpec((d, tm), lambda i: (0, i))],
        out_shape=[jax.ShapeDtypeStruct((nhc, PEER_NKEYS, n), F32),
                   jax.ShapeDtypeStruct((d, n), BF16)],
        compiler_params=_cparams(("parallel",)), name="peer_scores")(x, g.reshape(1, d), wq, keys)


def _top16(s, iota, exact_ties):
    rank = jnp.full(s.shape, float(PEER_TOPK), F32)
    vals = []
    for a in range(PEER_TOPK):
        m = jnp.max(s, axis=0, keepdims=True)
        hit = s == m
        if exact_ties:
            idx = jnp.min(jnp.where(hit, iota, float(PEER_NKEYS)), axis=0, keepdims=True)
            hit = iota == idx
        rank = jnp.where(hit, float(a), rank)
        s = jnp.where(hit, NEG_INF, s)
        vals.append(m)
    ranked = jnp.sum(jnp.where(rank < float(PEER_TOPK), 1.0, 0.0), axis=0, keepdims=True)
    return rank, vals, ranked


def _dup_mask_bits(x):
    bits = lax.bitcast_convert_type(x.astype(MASK_DTYPE).astype(F32), jnp.uint32)
    return bits | (bits >> 16)


def _peer_select_kernel(s_ref, l_ref, p1_ref, r2_ref, p2_ref):
    miscount = _peer_select_body(s_ref, l_ref, p1_ref, r2_ref, p2_ref, exact_ties=False)

    @pl.when(jnp.max(miscount) > 0.0)
    def _():
        _peer_select_body(s_ref, l_ref, p1_ref, r2_ref, p2_ref, exact_ties=True)


def _peer_select_body(s_ref, l_ref, p1_ref, r2_ref, p2_ref, exact_ties):
    lanes = s_ref.shape[-1]
    iota = lax.broadcasted_iota(jnp.int32, (PEER_NKEYS, lanes), 0).astype(F32)
    iota16 = lax.broadcasted_iota(jnp.int32, (PEER_TOPK, lanes), 0).astype(F32)
    miscount = jnp.zeros((1, lanes), F32)
    for h in range(PEER_HEADS):
        s1 = s_ref[2 * h]
        s2 = s_ref[2 * h + 1]
        rank1, v1, n1 = _top16(s1, iota, exact_ties)
        rank2, v2, n2 = _top16(s2, iota, exact_ties)
        miscount = miscount + jnp.abs(n1 - float(PEER_TOPK)) + jnp.abs(n2 - float(PEER_TOPK))
        v1a = jnp.concatenate(v1, axis=0)
        v2a = jnp.concatenate(v2, axis=0)
        top = v1[0] + v2[0]
        taken = jnp.zeros((PEER_TOPK, lanes), F32)
        front = v1a + v2[0]
        zsum = jnp.zeros((1, lanes), F32)
        for _ in range(PEER_TOPK):
            m = jnp.max(front, axis=0, keepdims=True)
            a_star = jnp.min(jnp.where(front == m, iota16, float(PEER_TOPK)), axis=0, keepdims=True)
            hit = iota16 == a_star
            zsum = zsum + jnp.exp(m - top)
            taken = jnp.where(hit, taken + 1.0, taken)
            cnt = jnp.max(jnp.where(hit, taken, -1.0), axis=0, keepdims=True)
            nxt = jnp.max(jnp.where(iota16 == cnt, v2a, NEG_INF), axis=0, keepdims=True)
            front = jnp.where(hit, v1a + nxt, front)
        lim = jnp.zeros((PEER_NKEYS, lanes), F32)
        for a in range(PEER_TOPK):
            lim = jnp.where(rank1 == float(a), taken[a:a + 1, :], lim)
        l_ref[h] = _dup_mask_bits(lim)
        p1_ref[h] = _dup_mask_bits(jnp.exp(s1 - v1[0]))
        p2 = jnp.exp(s2 - v2[0]) / zsum
        for r in range(PEER_NKEYS // MASK_ROWS):
            rows = slice(r * MASK_ROWS, (r + 1) * MASK_ROWS)
            r2_ref[h, r] = rank2[rows].astype(MASK_DTYPE)
            p2_ref[h, r] = p2[rows].astype(MASK_DTYPE)
    return miscount


def _peer_select_call(scores):
    nhc, nk, n = scores.shape
    tl = LANES
    groups = nk // MASK_ROWS
    words = pl.BlockSpec((PEER_HEADS, nk, tl), lambda i: (0, 0, i))
    packed = pl.BlockSpec((PEER_HEADS, groups, MASK_ROWS, tl), lambda i: (0, 0, 0, i))
    return pl.pallas_call(
        _peer_select_kernel, grid=(pl.cdiv(n, tl),),
        in_specs=[pl.BlockSpec((nhc, nk, tl), lambda i: (0, 0, i))],
        out_specs=[words, words, packed, packed],
        out_shape=[jax.ShapeDtypeStruct((PEER_HEADS, nk, n), jnp.uint32)] * 2
        + [jax.ShapeDtypeStruct((PEER_HEADS, groups, MASK_ROWS, n), MASK_DTYPE)] * 2,
        compiler_params=_cparams(("parallel",)), name="peer_select")(scores)


def _gate_weights(cb, ii, heads, w, l_ref, p1_ref, r2_ref, p2_ref):
    blocks = PEER_CHUNK // PEER_NKEYS
    base = pl.multiple_of(cb * blocks + (ii // 8) * 8, 8)
    sub = ii % 8
    w = list(w)
    for h in heads:
        lw = l_ref[h, pl.ds(base, 8), :]
        pw = p1_ref[h, pl.ds(base, 8), :]
        lim = pltpu.bitcast(jnp.broadcast_to(lw[sub:sub + 1, :], lw.shape), MASK_DTYPE)
        p1 = pltpu.bitcast(jnp.broadcast_to(pw[sub:sub + 1, :], pw.shape), MASK_DTYPE)
        for r in range(len(w)):
            term = jnp.where(r2_ref[h, r] < lim, p2_ref[h, r], jnp.zeros_like(p1)) * p1
            w[r] = term if w[r] is None else w[r] + term
    return w


def _gate_store(ii, w, z_ref, g_ref):
    for r in range(len(w)):
        rows = slice(ii * PEER_NKEYS + r * MASK_ROWS, ii * PEER_NKEYS + (r + 1) * MASK_ROWS)
        z = z_ref[rows, :]
        act = 0.5 * z * (1.0 + lax.erf(z * np.float32(np.sqrt(0.5))))
        g_ref[rows, :] = (act.astype(MASK_DTYPE) * w[r]).astype(g_ref.dtype)


def _peer_dense_kernel(ht_ref, u_ref, v_ref, l_ref, p1_ref, r2_ref, p2_ref, o_ref,
                       z0_ref, z1_ref, g0_ref, g1_ref, acc_ref, *, items, chunks):
    s = pl.program_id(0)

    @pl.when(s == 0)
    def _():
        for ref in (z0_ref, z1_ref, g0_ref, g1_ref):
            ref[...] = jnp.zeros_like(ref)

    gate_chunk = jnp.clip(s - 1, 0, items - 1) % chunks

    def stages(z_new, z_old, g_new, g_old):
        blocks = PEER_CHUNK // PEER_NKEYS
        sel = (l_ref, p1_ref, r2_ref, p2_ref)
        half = PEER_HEADS // 2
        per_v = blocks // (o_ref.shape[0] // PEER_NKEYS)
        for ii in range(blocks):
            er = slice(ii * PEER_NKEYS, (ii + 1) * PEER_NKEYS)
            w = _gate_weights(gate_chunk, ii, range(half), [None] * (PEER_NKEYS // MASK_ROWS), *sel)
            z_new[er, :] = jnp.dot(u_ref[er, :], ht_ref[...], preferred_element_type=F32)
            w = _gate_weights(gate_chunk, ii, range(half, PEER_HEADS), w, *sel)
            _gate_store(ii, w, z_old, g_new)
            if ii % per_v == per_v - 1:
                dr = slice((ii // per_v) * PEER_NKEYS, (ii // per_v + 1) * PEER_NKEYS)
                acc_ref[dr, :] = lax.dot_general(v_ref[:, dr], g_old[...], (((0,), (0,)), ((), ())),
                                                 preferred_element_type=F32)

    @pl.when(s % 2 == 0)
    def _():
        stages(z0_ref, z1_ref, g1_ref, g0_ref)

    @pl.when(s % 2 == 1)
    def _():
        stages(z1_ref, z0_ref, g0_ref, g1_ref)

    first = jnp.clip(s - 2, 0, items - 1) % chunks == 0

    @pl.when(first)
    def _():
        o_ref[...] = acc_ref[...]

    @pl.when(jnp.logical_not(first))
    def _():
        o_ref[...] += acc_ref[...]


def _peer_dense_call(ht, u, v, lim, p1, r2, p2):
    d, n = ht.shape
    chunks = u.shape[0] // PEER_CHUNK
    tt = min(PEER_TOKENS, n)
    items = pl.cdiv(n, tt) * chunks

    def item(s, lag):
        return jnp.clip(s - lag, 0, items - 1)

    words = pl.BlockSpec((PEER_HEADS, PEER_NKEYS, tt), lambda s: (0, 0, item(s, 1) // chunks))
    packed = pl.BlockSpec(r2.shape[:3] + (tt,), lambda s: (0, 0, 0, item(s, 1) // chunks))
    return pl.pallas_call(
        functools.partial(_peer_dense_kernel, items=items, chunks=chunks),
        grid=(items + 2,),
        in_specs=[pl.BlockSpec((d, tt), lambda s: (0, item(s, 0) // chunks)),
                  pl.BlockSpec((PEER_CHUNK, d), lambda s: (item(s, 0) % chunks, 0)),
                  pl.BlockSpec((PEER_CHUNK, d), lambda s: (item(s, 2) % chunks, 0)),
                  words, words, packed, packed],
        out_specs=pl.BlockSpec((d, tt), lambda s: (0, item(s, 2) // chunks)),
        out_shape=jax.ShapeDtypeStruct((d, n), F32),
        scratch_shapes=[pltpu.VMEM((PEER_CHUNK, tt), F32)] * 2 + [pltpu.VMEM((PEER_CHUNK, tt), BF16)] * 2
        + [pltpu.VMEM((d, tt), F32)],
        compiler_params=_cparams(("arbitrary",)), name="peer_dense")(ht, u, v, lim, p1, r2, p2)


def _peer_params(g, wq, keys, u, v):
    nh, _, nk, half = keys.shape
    return g, wq.astype(BF16), keys.reshape(nh * 2, nk, half).astype(BF16), u.astype(BF16), v.astype(BF16)


def _peer(x, params):
    g, wq, keys, u, v = params
    scores, ht = _peer_scores_call(x, g, wq, keys)
    lim, p1, r2, p2 = _peer_select_call(scores)
    return _peer_dense_call(ht, u, v, lim, p1, r2, p2)


def _pool_prompt_kernel(xh_ref, fth_ref, x_ref, ft_ref, gn_ref, w_ref, sc_ref, o_ref, *, batch):
    tm = x_ref.shape[0]
    hr = xh_ref.shape[0]
    x = x_ref[...] + ft_ref[...].T
    cur = _rms(x, gn_ref[...])
    halo = jnp.where(pl.program_id(0) == 0, 0.0,
                     _rms(xh_ref[...] + fth_ref[...].T, gn_ref[...]))
    ext = jnp.concatenate([halo, cur], axis=0)
    row = (pl.program_id(0) * tm + lax.broadcasted_iota(jnp.int32, (tm, 1), 0)).astype(F32)
    pos = jnp.floor(row / float(batch))
    gd = cur.shape[-1] // len(POOL_WINDOWS)
    outs = []
    for gi, w in enumerate(POOL_WINDOWS):
        acc = ext[:, gi * gd:(gi + 1) * gd]
        dropped = 0
        shift = 1
        while shift < w:
            k = shift * batch
            acc = acc[k:] + acc[:-k]
            dropped += k
            shift *= 2
        cnt = jnp.minimum(float(w), pos + 1.0)
        mixed = acc[hr - dropped:] / cnt - cur[:, gi * gd:(gi + 1) * gd]
        outs.append(_bdot(mixed, w_ref[gi]))
    o_ref[...] = x + jnp.concatenate(outs, axis=-1) * sc_ref[...]


def _pool_prompt_call(x, ft, gn, pool_w, pool_scale, batch):
    n, d = x.shape
    tm = _row_tile(n)
    hr = POOL_HALO * batch
    assert tm % hr == 0 and hr % LANES == 0

    def before(i):
        return jnp.maximum(i * (tm // hr) - 1, 0)

    vec = pl.BlockSpec((1, d), lambda i: (0, 0))
    return pl.pallas_call(
        functools.partial(_pool_prompt_kernel, batch=batch), grid=(pl.cdiv(n, tm),),
        in_specs=[pl.BlockSpec((hr, d), lambda i: (before(i), 0)), pl.BlockSpec((d, hr), lambda i: (0, before(i))),
                  pl.BlockSpec((tm, d), lambda i: (i, 0)), pl.BlockSpec((d, tm), lambda i: (0, i)), vec,
                  pl.BlockSpec(pool_w.shape, lambda i: (0, 0, 0)), vec],
        out_specs=pl.BlockSpec((tm, d), lambda i: (i, 0)), out_shape=jax.ShapeDtypeStruct((n, d), F32),
        compiler_params=_cparams(("parallel",)), name="pool_prompt")(
            x, ft, x, ft, gn.reshape(1, d), pool_w, pool_scale.reshape(1, d))


def _pool_sample_kernel(hist_ref, h_ref, x_ref, w_ref, sc_ref, o_ref):
    h = h_ref[...]
    nhist = hist_ref.shape[0]
    gd = h.shape[-1] // len(POOL_WINDOWS)
    outs = []
    for gi, w in enumerate(POOL_WINDOWS):
        sl = slice(gi * gd, (gi + 1) * gd)
        acc = h[:, sl]
        for back in range(1, w):
            acc = acc + hist_ref[nhist - back][:, sl]
        mixed = acc / float(w) - h[:, sl]
        outs.append(_bdot(mixed, w_ref[gi]))
    o_ref[...] = x_ref[...] + jnp.concatenate(outs, axis=-1) * sc_ref[...]


def _pool_sample_call(hist, h, x, pool_w, pool_scale):
    b, d = h.shape
    full2 = pl.BlockSpec((b, d), lambda i: (0, 0))
    return pl.pallas_call(
        _pool_sample_kernel, grid=(1,),
        in_specs=[pl.BlockSpec(hist.shape, lambda i: (0, 0, 0)), full2, full2,
                  pl.BlockSpec(pool_w.shape, lambda i: (0, 0, 0)),
                  pl.BlockSpec((1, d), lambda i: (0, 0))],
        out_specs=full2, out_shape=jax.ShapeDtypeStruct((b, d), F32),
        compiler_params=_cparams(("arbitrary",)), name="pool_sample")(
            hist, h, x, pool_w, pool_scale.reshape(1, d))


def _time_block(t, pref):
    for c in range(min(pref, t), 0, -1):
        if t % c == 0 and (c % 8 == 0 or c == t):
            return c
    return t


def kernel(x_prompt, x_sample, state_wkv, state_shift, state_pool, meta_tokens, norm_mix, norm_ffn, norm_final, rwkv_mix, rwkv_w_rkv, rwkv_decay_w1, rwkv_decay_w2, rwkv_decay_b, rwkv_iclr_w1, rwkv_iclr_w2, rwkv_iclr_b, rwkv_gate_w1, rwkv_gate_w2, rwkv_k_k, rwkv_k_a, rwkv_r_k, rwkv_ln_g, rwkv_ln_b, rwkv_w_o, pool_w, pool_scale, peer_wq, peer_keys, peer_u, peer_v):
    bp, seq, d = x_prompt.shape
    bs = x_sample.shape[0]
    tp = N_META + seq
    n_p = bp * tp
    heads = d // HEAD_DIM
    assert x_sample.shape[1] == 1 and bp == 8 and bs % 8 == 0 and heads * 8 == LANES
    assert rwkv_w_rkv.shape[0] == 1 and pool_w.shape[0] == 1 and norm_mix.shape[0] == 2

    blockdiag = jnp.asarray(np.kron(np.eye(LANES // HEAD_DIM), np.ones((HEAD_DIM, HEAD_DIM))), BF16)
    proj_params = _rwkv_proj_params(
        norm_mix[0], rwkv_mix[0], rwkv_w_rkv[0], rwkv_decay_w1[0], rwkv_decay_w2[0], rwkv_decay_b[0], rwkv_iclr_w1[0],
        rwkv_iclr_w2[0], rwkv_iclr_b[0], rwkv_gate_w1[0], rwkv_gate_w2[0], rwkv_k_k[0], rwkv_k_a[0],
        rwkv_r_k[0], blockdiag)
    w_o = rwkv_w_o[0].astype(BF16)
    peer_params = [_peer_params(norm_ffn[i], peer_wq[i], peer_keys[i], peer_u[i], peer_v[i]) for i in range(2)]
    pool_wb = pool_w[0].astype(BF16)
    meta = jnp.broadcast_to(meta_tokens[:, None], (N_META, bp, d))
    x0p = jnp.concatenate([meta, x_prompt.transpose(1, 0, 2)], axis=0).reshape(n_p, d)
    x0s = x_sample.reshape(bs, d)

    def rwkv_layer(x0, prev, s0, steps, time_groups):
        *scan_inputs, g, bonus = _rwkv_proj_call(x0, prev, proj_params, time_groups)
        y, s_out = _wkv_group(scan_inputs, s0, steps)
        x1 = _rwkv_out_call(y, bonus, g, x0, rwkv_ln_g[0], rwkv_ln_b[0], w_o, blockdiag, time_groups)
        return s_out, x1

    wkv_p, x1p = rwkv_layer(x0p, x0p, jnp.zeros((bp, heads, HEAD_DIM, HEAD_DIM), F32),
                            _time_block(tp, 48), True)
    wkv_s, x1s = rwkv_layer(x0s, state_shift[0], state_wkv[0], 1, False)
    f0p = _peer(x1p, peer_params[0])
    f0s = _peer(x1s, peer_params[0])

    x2s, h1s = _add_norm_call(x1s, f0s, norm_mix[1])
    x3p = _pool_prompt_call(x1p, f0p, norm_mix[1], pool_wb, pool_scale[0], bp)
    x3s = _pool_sample_call(state_pool[0].transpose(1, 0, 2), h1s, x2s, pool_wb, pool_scale[0])
    f1p = _peer(x3p, peer_params[1])
    f1s = _peer(x3s, peer_params[1])
    y_prompt = _final_norm_prompt_call(x3p, f1p, norm_final, bp, N_META)
    _, yfs = _add_norm_call(x3s, f1s, norm_final)

    nbuf = state_pool.shape[2]
    y_sample = yfs.reshape(bs, 1, d)
    shift_p = _norm_call(x0p[n_p - bp:], norm_mix[0])[None]
    shift_s = _norm_call(x0s, norm_mix[0])[None]
    keep = min(tp, nbuf) * bp
    last = -(-keep // LANES) * LANES
    tail = _add_norm_call(x1p[n_p - last:], f0p[:, n_p - last:], norm_mix[1])[1][last - keep:]
    tail = jnp.concatenate([jnp.zeros((nbuf * bp - keep, d), F32), tail], axis=0)
    pool_p = tail.reshape(nbuf, bp, d).transpose(1, 0, 2)[None]
    pool_s = jnp.concatenate([state_pool[0][:, 1:], h1s[:, None, :]], axis=1)[None]
    return (y_prompt, y_sample, wkv_p[None], shift_p, pool_p, wkv_s[None], shift_s, pool_s)
```

```python
import functools

import jax
import jax.numpy as jnp
import numpy as np
from jax import lax
from jax.experimental import pallas as pl
from jax.experimental.pallas import tpu as pltpu

F32 = jnp.float32
BF16 = jnp.bfloat16

N_META = 16
RMS_EPS = 1e-6
GN_EPS = 64e-5
HEAD_DIM = 64
LANES = 128
POOL_WINDOWS = (2, 4, 8, 16)
POOL_HALO = 16
PEER_HEADS = 8
PEER_NKEYS = 128
PEER_TOPK = 16
PEER_CHUNK = 2048
PEER_TOKENS = 512
MASK_DTYPE = jnp.bfloat16
MASK_ROWS = 16
ROW_TILE = 256
VMEM_LIMIT = 56 * 1024 * 1024

NEG_INF = float("-inf")


def _cparams(sem):
    return pltpu.CompilerParams(dimension_semantics=sem, vmem_limit_bytes=VMEM_LIMIT)


def _row_tile(n):
    return min(ROW_TILE, n)


def _rms(x, g):
    return x * lax.rsqrt(jnp.mean(x * x, axis=-1, keepdims=True) + RMS_EPS) * g


def _bdot(a, b):
    return jnp.dot(a.astype(BF16), b.astype(BF16), preferred_element_type=F32)


def _head_sum(x, blockdiag):
    outs = []
    for blk in range(x.shape[-1] // LANES):
        xb = x[:, blk * LANES:(blk + 1) * LANES]
        hi = xb.astype(BF16)
        lo = (xb - hi.astype(F32)).astype(BF16)
        outs.append(jnp.dot(hi, blockdiag, preferred_element_type=F32)
                    + jnp.dot(lo, blockdiag, preferred_element_type=F32))
    return jnp.concatenate(outs, axis=-1)


def _norm_kernel(x_ref, g_ref, h_ref):
    h_ref[...] = _rms(x_ref[...], g_ref[...])


def _norm_call(x, g):
    n, d = x.shape
    tm = _row_tile(n)
    row = pl.BlockSpec((tm, d), lambda i: (i, 0))
    return pl.pallas_call(
        _norm_kernel, grid=(pl.cdiv(n, tm),),
        in_specs=[row, pl.BlockSpec((1, d), lambda i: (0, 0))],
        out_specs=row, out_shape=jax.ShapeDtypeStruct((n, d), F32),
        compiler_params=_cparams(("parallel",)), name="rmsnorm")(x, g.reshape(1, d))


def _add_norm_kernel(a_ref, bt_ref, g_ref, x_ref, h_ref):
    x = a_ref[...] + bt_ref[...].T
    x_ref[...] = x
    h_ref[...] = _rms(x, g_ref[...])


def _add_norm_call(a, bt, g, col0=0):
    n, d = a.shape
    tm = _row_tile(n)
    assert col0 % tm == 0
    off = col0 // tm
    row = pl.BlockSpec((tm, d), lambda i: (i, 0))
    return pl.pallas_call(
        _add_norm_kernel, grid=(pl.cdiv(n, tm),),
        in_specs=[row, pl.BlockSpec((d, tm), lambda i: (0, i + off)), pl.BlockSpec((1, d), lambda i: (0, 0))],
        out_specs=[row, row],
        out_shape=[jax.ShapeDtypeStruct((n, d), F32)] * 2,
        compiler_params=_cparams(("parallel",)), name="add_rmsnorm")(a, bt, g.reshape(1, d))


def _final_norm_prompt_kernel(a_ref, bt_ref, g_ref, o_ref, h_ref, *, batch):
    h = _rms(a_ref[...] + bt_ref[...].T, g_ref[...])
    for c in range(h_ref.shape[0]):
        h_ref[c] = h[:, c * LANES:(c + 1) * LANES]

    @pl.when(pl.program_id(0) > 0)
    def _():
        steps = h_ref.shape[1] // batch
        for b in range(batch):
            for c in range(h_ref.shape[0]):
                o_ref[b, :, c * LANES:(c + 1) * LANES] = h_ref[c, pl.ds(b, steps, stride=batch), :]


def _final_norm_prompt_call(a, bt, g, batch, meta_steps):
    n, d = a.shape
    tm = meta_steps * batch
    assert n % tm == 0 and tm % LANES == 0
    return pl.pallas_call(
        functools.partial(_final_norm_prompt_kernel, batch=batch), grid=(n // tm,),
        in_specs=[pl.BlockSpec((tm, d), lambda i: (i, 0)), pl.BlockSpec((d, tm), lambda i: (0, i)),
                  pl.BlockSpec((1, d), lambda i: (0, 0))],
        out_specs=pl.BlockSpec((batch, meta_steps, d), lambda i: (0, jnp.maximum(i - 1, 0), 0)),
        out_shape=jax.ShapeDtypeStruct((batch, n // batch - meta_steps, d), F32),
        scratch_shapes=[pltpu.VMEM((d // LANES, tm, LANES), F32)],
        compiler_params=_cparams(("arbitrary",)), name="final_norm_prompt")(a, bt, g.reshape(1, d))


def _rows_to_scan(x8):
    pairs = x8.shape[-1] // LANES
    a = jnp.concatenate([x8[:, p * LANES:(p + 1) * LANES] for p in range(pairs)], axis=0)
    at = a.T
    return jnp.concatenate([at[:HEAD_DIM], at[HEAD_DIM:]], axis=1)


def _scan_to_rows(tile):
    at = jnp.concatenate([tile[:, :HEAD_DIM], tile[:, HEAD_DIM:]], axis=0)
    a = at.T
    return jnp.concatenate([a[p * 8:(p + 1) * 8, :] for p in range(a.shape[0] // 8)], axis=1)


def _store_scan(ref, x, time_groups):
    for j in range(x.shape[0] // 8):
        tile = _rows_to_scan(x[j * 8:(j + 1) * 8, :])
        if time_groups:
            ref[j] = tile
        else:
            ref[0, :, j * LANES:(j + 1) * LANES] = tile


def _load_scan(ref, time_groups):
    groups = ref.shape[0] if time_groups else ref.shape[2] // LANES
    tiles = [ref[j] if time_groups else ref[0, :, j * LANES:(j + 1) * LANES] for j in range(groups)]
    return jnp.concatenate([_scan_to_rows(t) for t in tiles], axis=0)


def _scan_spec(tm, time_groups):
    if time_groups:
        return pl.BlockSpec((tm // 8, HEAD_DIM, LANES), lambda i: (i, 0, 0))
    return pl.BlockSpec((1, HEAD_DIM, tm // 8 * LANES), lambda i: (0, 0, i))


def _scan_shape(n, time_groups):
    return (n // 8, HEAD_DIM, LANES) if time_groups else (1, HEAD_DIM, n // 8 * LANES)


def _state_to_scan(s):
    b, heads, nv, nk = s.shape
    s = s.reshape(b // 8, 8, heads // 2, 2, nv, nk).transpose(5, 4, 0, 3, 2, 1)
    return s.reshape(nk, nv, b * heads)


def _state_from_scan(s, b):
    nk, nv, lanes = s.shape
    heads = lanes // b
    return s.reshape(nk, nv, b // 8, 2, heads // 2, 8).transpose(2, 5, 4, 3, 1, 0).reshape(b, heads, nv, nk)


def _rwkv_proj_kernel(x_ref, prev_ref, gn_ref, mix_ref, wrkv_ref, dw1_ref, dw2_ref, db_ref,
                      aw1_ref, aw2_ref, ab_ref, gw1_ref, gw2_ref, kk_ref, ka_ref, rk_ref, bd_ref,
                      r_ref, dec_ref, k_ref, v_ref, aa_ref, bb_ref, g_ref, bonus_ref, *, time_groups):
    h = _rms(x_ref[...], gn_ref[...])
    if time_groups:
        before = jnp.where(pl.program_id(0) == 0, 0.0, _rms(prev_ref[...], gn_ref[...]))
        prev = jnp.concatenate([before, h[:h.shape[0] - 8]], axis=0)
    else:
        prev = prev_ref[...]
    xx = prev - h

    def xs(c):
        return h + xx * mix_ref[c:c + 1, :]

    r = _bdot(xs(0), wrkv_ref[0])
    k = _bdot(xs(1), wrkv_ref[1])
    v = _bdot(xs(2), wrkv_ref[2])
    w = -jax.nn.softplus(-(db_ref[...] + _bdot(jnp.tanh(_bdot(xs(3), dw1_ref[...])), dw2_ref[...]))) - 0.5
    a = jax.nn.sigmoid(ab_ref[...] + _bdot(_bdot(xs(4), aw1_ref[...]), aw2_ref[...]))
    g = _bdot(jax.nn.sigmoid(_bdot(xs(5), gw1_ref[...])), gw2_ref[...])
    kk = k * kk_ref[...]
    nrm = jnp.sqrt(_head_sum(kk * kk, bd_ref[...]))
    kk = kk / jnp.maximum(nrm, 1e-12)
    kf = k * (1.0 + (a - 1.0) * ka_ref[...])
    _store_scan(r_ref, r, time_groups)
    _store_scan(dec_ref, jnp.exp(-jnp.exp(w)), time_groups)
    _store_scan(k_ref, kf, time_groups)
    _store_scan(v_ref, v, time_groups)
    _store_scan(aa_ref, -kk, time_groups)
    _store_scan(bb_ref, kk * a, time_groups)
    g_ref[...] = g
    bonus_ref[...] = _head_sum(r * kf * rk_ref[...], bd_ref[...]) * v


def _rwkv_proj_params(gn, mix, w_rkv, dw1, dw2, db, aw1, aw2, ab, gw1, gw2, k_k, k_a, r_k, blockdiag):
    d = mix.shape[-1]
    return [gn.reshape(1, d), mix, w_rkv.astype(BF16), dw1.astype(BF16), dw2.astype(BF16), db.reshape(1, d),
            aw1.astype(BF16), aw2.astype(BF16), ab.reshape(1, d), gw1.astype(BF16),
            gw2.astype(BF16), k_k.reshape(1, d), k_a.reshape(1, d), r_k.reshape(1, d), blockdiag]


def _rwkv_proj_call(x, prev, params, time_groups):
    n, d = x.shape
    tm = _row_tile(n)
    assert d == 8 * LANES and tm % 8 == 0
    row = pl.BlockSpec((tm, d), lambda i: (i, 0))
    scan = _scan_spec(tm, time_groups)
    prev_spec = pl.BlockSpec((8, d), lambda i: (jnp.maximum(i * (tm // 8) - 1, 0), 0)) if time_groups else row

    def full(x):
        nd = x.ndim
        return pl.BlockSpec(x.shape, lambda i: (0,) * nd)

    return pl.pallas_call(
        functools.partial(_rwkv_proj_kernel, time_groups=time_groups), grid=(pl.cdiv(n, tm),),
        in_specs=[row, prev_spec] + [full(p) for p in params],
        out_specs=[scan] * 6 + [row] * 2,
        out_shape=[jax.ShapeDtypeStruct(_scan_shape(n, time_groups), F32)] * 6
        + [jax.ShapeDtypeStruct((n, d), F32)] * 2,
        compiler_params=_cparams(("parallel",)), name="rwkv_proj")(x, prev, *params)


def _wkv_kernel(r_ref, d_ref, k_ref, v_ref, a_ref, b_ref, anext_ref, s0_ref, y_ref, s_ref, sa_ref, *, steps):
    nk = s_ref.shape[0]

    @pl.when(pl.program_id(1) == 0)
    def _():
        s_ref[...] = s0_ref[...]
        sa0 = None
        for kk in range(nk):
            term = s0_ref[kk] * a_ref[0, kk:kk + 1, :]
            sa0 = term if sa0 is None else sa0 + term
        sa_ref[...] = sa0

    def step(t, sa, next_row):
        vv = v_ref[t]
        y = None
        sa_next = None
        for kk in range(nk):
            sk = (s_ref[kk] * d_ref[t, kk:kk + 1, :] + sa * b_ref[t, kk:kk + 1, :]
                  + vv * k_ref[t, kk:kk + 1, :])
            s_ref[kk] = sk
            ty = sk * r_ref[t, kk:kk + 1, :]
            ta = sk * next_row(kk)
            y = ty if y is None else y + ty
            sa_next = ta if sa_next is None else sa_next + ta
        y_ref[t] = y
        return sa_next

    sa = lax.fori_loop(0, steps - 1, lambda t, sa: step(t, sa, lambda kk: a_ref[t + 1, kk:kk + 1, :]),
                       sa_ref[...])
    sa_ref[...] = step(steps - 1, sa, lambda kk: anext_ref[0, kk:kk + 1, :])


def _wkv_call(r, dec, k, v, aa, bb, s0, steps):
    t_len, hd, lanes = r.shape
    seq = pl.BlockSpec((steps, hd, LANES), lambda l, t: (t, 0, l))
    nxt = pl.BlockSpec((1, hd, LANES), lambda l, t: (jnp.minimum((t + 1) * steps, t_len - 1), 0, l))
    st = pl.BlockSpec((hd, hd, LANES), lambda l, t: (0, 0, l))
    return pl.pallas_call(
        functools.partial(_wkv_kernel, steps=steps),
        grid=(lanes // LANES, t_len // steps),
        in_specs=[seq] * 6 + [nxt, st],
        out_specs=[seq, st],
        out_shape=[jax.ShapeDtypeStruct((t_len, hd, lanes), F32),
                   jax.ShapeDtypeStruct((hd, hd, lanes), F32)],
        scratch_shapes=[pltpu.VMEM((hd, LANES), F32)],
        compiler_params=_cparams(("parallel", "arbitrary")), name="wkv_scan")(r, dec, k, v, aa, bb, aa, s0)


def _wkv_group(scan_inputs, s0, steps):
    y, s_l = _wkv_call(*scan_inputs, _state_to_scan(s0), steps)
    return y, _state_from_scan(s_l, s0.shape[0])


def _rwkv_out_kernel(y_ref, bonus_ref, g_ref, x_ref, lng_ref, lnb_ref, wo_ref, bd_ref, o_ref, *, time_groups):
    bd = bd_ref[...]
    y = _load_scan(y_ref, time_groups)
    inv_n = 1.0 / HEAD_DIM
    mu = _head_sum(y, bd) * inv_n
    yc = y - mu
    var = _head_sum(yc * yc, bd) * inv_n
    yn = yc * lax.rsqrt(var + GN_EPS) * lng_ref[...] + lnb_ref[...]
    o_ref[...] = x_ref[...] + _bdot((yn + bonus_ref[...]) * g_ref[...], wo_ref[...])


def _rwkv_out_call(y, bonus, g, x, ln_g, ln_b, w_o, blockdiag, time_groups):
    n, d = x.shape
    tm = _row_tile(n)
    row = pl.BlockSpec((tm, d), lambda i: (i, 0))
    vec = pl.BlockSpec((1, d), lambda i: (0, 0))
    return pl.pallas_call(
        functools.partial(_rwkv_out_kernel, time_groups=time_groups), grid=(pl.cdiv(n, tm),),
        in_specs=[_scan_spec(tm, time_groups), row, row, row, vec, vec,
                  pl.BlockSpec((d, d), lambda i: (0, 0)), pl.BlockSpec((LANES, LANES), lambda i: (0, 0))],
        out_specs=row, out_shape=jax.ShapeDtypeStruct((n, d), F32),
        compiler_params=_cparams(("parallel",)), name="rwkv_out")(
            y, bonus, g, x, ln_g.reshape(1, d), ln_b.reshape(1, d), w_o, blockdiag)


def _peer_scores_kernel(x_ref, g_ref, wq_ref, keys_ref, s_ref, ht_ref):
    h = _rms(x_ref[...], g_ref[...])
    ht_ref[...] = h.T.astype(BF16)
    q = _bdot(h, wq_ref[...]).astype(BF16)
    for hc in range(keys_ref.shape[0]):
        s_ref[hc] = lax.dot_general(keys_ref[hc], q[:, hc * LANES:(hc + 1) * LANES],
                                    (((1,), (1,)), ((), ())), preferred_element_type=F32)


def _peer_scores_into_kernel(x_ref, g_ref, wq_ref, keys_ref, s_in_ref, ht_in_ref, s_ref, ht_ref):
    del s_in_ref, ht_in_ref
    _peer_scores_kernel(x_ref, g_ref, wq_ref, keys_ref, s_ref, ht_ref)


def _peer_scores_call(x, g, wq, keys, n_total, col0, into=None):
    n, d = x.shape
    tm = _row_tile(n)
    nhc = keys.shape[0]
    assert col0 % tm == 0
    off = col0 // tm
    in_specs = [pl.BlockSpec((tm, d), lambda i: (i, 0)), pl.BlockSpec((1, d), lambda i: (0, 0)),
                pl.BlockSpec(wq.shape, lambda i: (0, 0)), pl.BlockSpec(keys.shape, lambda i: (0, 0, 0))]
    args = [x, g.reshape(1, d), wq, keys]
    body, aliases = _peer_scores_kernel, {}
    if into is not None:
        in_specs += [pl.BlockSpec(memory_space=pl.ANY)] * 2
        args += list(into)
        body, aliases = _peer_scores_into_kernel, {4: 0, 5: 1}
    return pl.pallas_call(
        body, grid=(pl.cdiv(n, tm),), in_specs=in_specs,
        out_specs=[pl.BlockSpec((nhc, PEER_NKEYS, tm), lambda i: (0, 0, i + off)),
                   pl.BlockSpec((d, tm), lambda i: (0, i + off))],
        out_shape=[jax.ShapeDtypeStruct((nhc, PEER_NKEYS, n_total), F32),
                   jax.ShapeDtypeStruct((d, n_total), BF16)],
        input_output_aliases=aliases,
        compiler_params=_cparams(("parallel",)), name="peer_scores")(*args)


def _top16(s, iota, exact_ties):
    rank = jnp.full(s.shape, float(PEER_TOPK), F32)
    vals = []
    for a in range(PEER_TOPK):
        m = jnp.max(s, axis=0, keepdims=True)
        hit = s == m
        if exact_ties:
            idx = jnp.min(jnp.where(hit, iota, float(PEER_NKEYS)), axis=0, keepdims=True)
            hit = iota == idx
        rank = jnp.where(hit, float(a), rank)
        s = jnp.where(hit, NEG_INF, s)
        vals.append(m)
    ranked = jnp.sum(jnp.where(rank < float(PEER_TOPK), 1.0, 0.0), axis=0, keepdims=True)
    return rank, vals, ranked


def _dup_mask_bits(x):
    bits = lax.bitcast_convert_type(x.astype(MASK_DTYPE).astype(F32), jnp.uint32)
    return bits | (bits >> 16)


def _peer_select_kernel(s_ref, l_ref, p1_ref, r2_ref, p2_ref):
    miscount = _peer_select_body(s_ref, l_ref, p1_ref, r2_ref, p2_ref, exact_ties=False)

    @pl.when(jnp.max(miscount) > 0.0)
    def _():
        _peer_select_body(s_ref, l_ref, p1_ref, r2_ref, p2_ref, exact_ties=True)


def _peer_select_body(s_ref, l_ref, p1_ref, r2_ref, p2_ref, exact_ties):
    lanes = s_ref.shape[-1]
    iota = lax.broadcasted_iota(jnp.int32, (PEER_NKEYS, lanes), 0).astype(F32)
    iota16 = lax.broadcasted_iota(jnp.int32, (PEER_TOPK, lanes), 0).astype(F32)
    miscount = jnp.zeros((1, lanes), F32)
    for h in range(PEER_HEADS):
        s1 = s_ref[2 * h]
        s2 = s_ref[2 * h + 1]
        rank1, v1, n1 = _top16(s1, iota, exact_ties)
        rank2, v2, n2 = _top16(s2, iota, exact_ties)
        miscount = miscount + jnp.abs(n1 - float(PEER_TOPK)) + jnp.abs(n2 - float(PEER_TOPK))
        v1a = jnp.concatenate(v1, axis=0)
        v2a = jnp.concatenate(v2, axis=0)
        top = v1[0] + v2[0]
        taken = jnp.zeros((PEER_TOPK, lanes), F32)
        front = v1a + v2[0]
        zsum = jnp.zeros((1, lanes), F32)
        for _ in range(PEER_TOPK):
            m = jnp.max(front, axis=0, keepdims=True)
            a_star = jnp.min(jnp.where(front == m, iota16, float(PEER_TOPK)), axis=0, keepdims=True)
            hit = iota16 == a_star
            zsum = zsum + jnp.exp(m - top)
            taken = jnp.where(hit, taken + 1.0, taken)
            cnt = jnp.max(jnp.where(hit, taken, -1.0), axis=0, keepdims=True)
            nxt = jnp.max(jnp.where(iota16 == cnt, v2a, NEG_INF), axis=0, keepdims=True)
            front = jnp.where(hit, v1a + nxt, front)
        lim = jnp.zeros((PEER_NKEYS, lanes), F32)
        for a in range(PEER_TOPK):
            lim = jnp.where(rank1 == float(a), taken[a:a + 1, :], lim)
        l_ref[h] = _dup_mask_bits(lim)
        p1_ref[h] = _dup_mask_bits(jnp.exp(s1 - v1[0]))
        p2 = jnp.exp(s2 - v2[0]) / zsum
        for r in range(PEER_NKEYS // MASK_ROWS):
            rows = slice(r * MASK_ROWS, (r + 1) * MASK_ROWS)
            r2_ref[h, r] = rank2[rows].astype(MASK_DTYPE)
            p2_ref[h, r] = p2[rows].astype(MASK_DTYPE)
    return miscount


def _peer_select_call(scores):
    nhc, nk, n = scores.shape
    tl = LANES
    groups = nk // MASK_ROWS
    words = pl.BlockSpec((PEER_HEADS, nk, tl), lambda i: (0, 0, i))
    packed = pl.BlockSpec((PEER_HEADS, groups, MASK_ROWS, tl), lambda i: (0, 0, 0, i))
    return pl.pallas_call(
        _peer_select_kernel, grid=(pl.cdiv(n, tl),),
        in_specs=[pl.BlockSpec((nhc, nk, tl), lambda i: (0, 0, i))],
        out_specs=[words, words, packed, packed],
        out_shape=[jax.ShapeDtypeStruct((PEER_HEADS, nk, n), jnp.uint32)] * 2
        + [jax.ShapeDtypeStruct((PEER_HEADS, groups, MASK_ROWS, n), MASK_DTYPE)] * 2,
        compiler_params=_cparams(("parallel",)), name="peer_select")(scores)


def _gate_weights(cb, ii, heads, w, l_ref, p1_ref, r2_ref, p2_ref):
    blocks = PEER_CHUNK // PEER_NKEYS
    base = pl.multiple_of(cb * blocks + (ii // 8) * 8, 8)
    sub = ii % 8
    w = list(w)
    for h in heads:
        lw = l_ref[h, pl.ds(base, 8), :]
        pw = p1_ref[h, pl.ds(base, 8), :]
        lim = pltpu.bitcast(jnp.broadcast_to(lw[sub:sub + 1, :], lw.shape), MASK_DTYPE)
        p1 = pltpu.bitcast(jnp.broadcast_to(pw[sub:sub + 1, :], pw.shape), MASK_DTYPE)
        for r in range(len(w)):
            term = jnp.where(r2_ref[h, r] < lim, p2_ref[h, r], jnp.zeros_like(p1)) * p1
            w[r] = term if w[r] is None else w[r] + term
    return w


def _gate_store(ii, w, z_ref, g_ref):
    for r in range(len(w)):
        rows = slice(ii * PEER_NKEYS + r * MASK_ROWS, ii * PEER_NKEYS + (r + 1) * MASK_ROWS)
        z = z_ref[rows, :]
        act = 0.5 * z * (1.0 + lax.erf(z * np.float32(np.sqrt(0.5))))
        g_ref[rows, :] = (act.astype(MASK_DTYPE) * w[r]).astype(g_ref.dtype)


def _peer_dense_kernel(ht_ref, u_ref, v_ref, l_ref, p1_ref, r2_ref, p2_ref, o_ref,
                       z0_ref, z1_ref, g0_ref, g1_ref, acc_ref, *, items, chunks):
    s = pl.program_id(0)

    @pl.when(s == 0)
    def _():
        for ref in (z0_ref, z1_ref, g0_ref, g1_ref):
            ref[...] = jnp.zeros_like(ref)

    gate_chunk = jnp.clip(s - 1, 0, items - 1) % chunks

    def stages(z_new, z_old, g_new, g_old):
        blocks = PEER_CHUNK // PEER_NKEYS
        sel = (l_ref, p1_ref, r2_ref, p2_ref)
        half = PEER_HEADS // 2
        per_v = blocks // (o_ref.shape[0] // PEER_NKEYS)
        for ii in range(blocks):
            er = slice(ii * PEER_NKEYS, (ii + 1) * PEER_NKEYS)
            w = _gate_weights(gate_chunk, ii, range(half), [None] * (PEER_NKEYS // MASK_ROWS), *sel)
            z_new[er, :] = jnp.dot(u_ref[er, :], ht_ref[...], preferred_element_type=F32)
            w = _gate_weights(gate_chunk, ii, range(half, PEER_HEADS), w, *sel)
            _gate_store(ii, w, z_old, g_new)
            if ii % per_v == per_v - 1:
                dr = slice((ii // per_v) * PEER_NKEYS, (ii // per_v + 1) * PEER_NKEYS)
                acc_ref[dr, :] = lax.dot_general(v_ref[:, dr], g_old[...], (((0,), (0,)), ((), ())),
                                                 preferred_element_type=F32)

    @pl.when(s % 2 == 0)
    def _():
        stages(z0_ref, z1_ref, g1_ref, g0_ref)

    @pl.when(s % 2 == 1)
    def _():
        stages(z1_ref, z0_ref, g0_ref, g1_ref)

    first = jnp.clip(s - 2, 0, items - 1) % chunks == 0

    @pl.when(first)
    def _():
        o_ref[...] = acc_ref[...]

    @pl.when(jnp.logical_not(first))
    def _():
        o_ref[...] += acc_ref[...]


def _peer_dense_call(ht, u, v, lim, p1, r2, p2):
    d, n = ht.shape
    chunks = u.shape[0] // PEER_CHUNK
    tt = min(PEER_TOKENS, n)
    items = pl.cdiv(n, tt) * chunks

    def item(s, lag):
        return jnp.clip(s - lag, 0, items - 1)

    words = pl.BlockSpec((PEER_HEADS, PEER_NKEYS, tt), lambda s: (0, 0, item(s, 1) // chunks))
    packed = pl.BlockSpec(r2.shape[:3] + (tt,), lambda s: (0, 0, 0, item(s, 1) // chunks))
    return pl.pallas_call(
        functools.partial(_peer_dense_kernel, items=items, chunks=chunks),
        grid=(items + 2,),
        in_specs=[pl.BlockSpec((d, tt), lambda s: (0, item(s, 0) // chunks)),
                  pl.BlockSpec((PEER_CHUNK, d), lambda s: (item(s, 0) % chunks, 0)),
                  pl.BlockSpec((PEER_CHUNK, d), lambda s: (item(s, 2) % chunks, 0)),
                  words, words, packed, packed],
        out_specs=pl.BlockSpec((d, tt), lambda s: (0, item(s, 2) // chunks)),
        out_shape=jax.ShapeDtypeStruct((d, n), F32),
        scratch_shapes=[pltpu.VMEM((PEER_CHUNK, tt), F32)] * 2 + [pltpu.VMEM((PEER_CHUNK, tt), BF16)] * 2
        + [pltpu.VMEM((d, tt), F32)],
        compiler_params=_cparams(("arbitrary",)), name="peer_dense")(ht, u, v, lim, p1, r2, p2)


def _peer_params(g, wq, keys, u, v):
    nh, _, nk, half = keys.shape
    return g, wq.astype(BF16), keys.reshape(nh * 2, nk, half).astype(BF16), u.astype(BF16), v.astype(BF16)


def _peer(x_a, x_b, params):
    g, wq, keys, u, v = params
    n_a, n_b = x_a.shape[0], x_b.shape[0]
    first = _peer_scores_call(x_a, g, wq, keys, n_a + n_b, 0)
    scores, ht = _peer_scores_call(x_b, g, wq, keys, n_a + n_b, n_a, into=first)
    lim, p1, r2, p2 = _peer_select_call(scores)
    return _peer_dense_call(ht, u, v, lim, p1, r2, p2)


def _pool_prompt_kernel(xh_ref, fth_ref, x_ref, ft_ref, gn_ref, w_ref, sc_ref, o_ref, *, batch):
    tm = x_ref.shape[0]
    hr = xh_ref.shape[0]
    x = x_ref[...] + ft_ref[...].T
    cur = _rms(x, gn_ref[...])
    halo = jnp.where(pl.program_id(0) == 0, 0.0,
                     _rms(xh_ref[...] + fth_ref[...].T, gn_ref[...]))
    ext = jnp.concatenate([halo, cur], axis=0)
    row = (pl.program_id(0) * tm + lax.broadcasted_iota(jnp.int32, (tm, 1), 0)).astype(F32)
    pos = jnp.floor(row / float(batch))
    gd = cur.shape[-1] // len(POOL_WINDOWS)
    outs = []
    for gi, w in enumerate(POOL_WINDOWS):
        acc = ext[:, gi * gd:(gi + 1) * gd]
        dropped = 0
        shift = 1
        while shift < w:
            k = shift * batch
            acc = acc[k:] + acc[:-k]
            dropped += k
            shift *= 2
        cnt = jnp.minimum(float(w), pos + 1.0)
        mixed = acc[hr - dropped:] / cnt - cur[:, gi * gd:(gi + 1) * gd]
        outs.append(_bdot(mixed, w_ref[gi]))
    o_ref[...] = x + jnp.concatenate(outs, axis=-1) * sc_ref[...]


def _pool_prompt_call(x, ft, gn, pool_w, pool_scale, batch):
    n, d = x.shape
    tm = _row_tile(n)
    hr = POOL_HALO * batch
    assert tm % hr == 0 and hr % LANES == 0

    def before(i):
        return jnp.maximum(i * (tm // hr) - 1, 0)

    vec = pl.BlockSpec((1, d), lambda i: (0, 0))
    return pl.pallas_call(
        functools.partial(_pool_prompt_kernel, batch=batch), grid=(pl.cdiv(n, tm),),
        in_specs=[pl.BlockSpec((hr, d), lambda i: (before(i), 0)), pl.BlockSpec((d, hr), lambda i: (0, before(i))),
                  pl.BlockSpec((tm, d), lambda i: (i, 0)), pl.BlockSpec((d, tm), lambda i: (0, i)), vec,
                  pl.BlockSpec(pool_w.shape, lambda i: (0, 0, 0)), vec],
        out_specs=pl.BlockSpec((tm, d), lambda i: (i, 0)), out_shape=jax.ShapeDtypeStruct((n, d), F32),
        compiler_params=_cparams(("parallel",)), name="pool_prompt")(
            x, ft, x, ft, gn.reshape(1, d), pool_w, pool_scale.reshape(1, d))


def _pool_sample_kernel(hist_ref, h_ref, x_ref, w_ref, sc_ref, o_ref):
    h = h_ref[...]
    nhist = hist_ref.shape[0]
    gd = h.shape[-1] // len(POOL_WINDOWS)
    outs = []
    for gi, w in enumerate(POOL_WINDOWS):
        sl = slice(gi * gd, (gi + 1) * gd)
        acc = h[:, sl]
        for back in range(1, w):
            acc = acc + hist_ref[nhist - back][:, sl]
        mixed = acc / float(w) - h[:, sl]
        outs.append(_bdot(mixed, w_ref[gi]))
    o_ref[...] = x_ref[...] + jnp.concatenate(outs, axis=-1) * sc_ref[...]


def _pool_sample_call(hist, h, x, pool_w, pool_scale):
    b, d = h.shape
    full2 = pl.BlockSpec((b, d), lambda i: (0, 0))
    return pl.pallas_call(
        _pool_sample_kernel, grid=(1,),
        in_specs=[pl.BlockSpec(hist.shape, lambda i: (0, 0, 0)), full2, full2,
                  pl.BlockSpec(pool_w.shape, lambda i: (0, 0, 0)),
                  pl.BlockSpec((1, d), lambda i: (0, 0))],
        out_specs=full2, out_shape=jax.ShapeDtypeStruct((b, d), F32),
        compiler_params=_cparams(("arbitrary",)), name="pool_sample")(
            hist, h, x, pool_w, pool_scale.reshape(1, d))


def _time_block(t, pref):
    for c in range(min(pref, t), 0, -1):
        if t % c == 0 and (c % 8 == 0 or c == t):
            return c
    return t


def kernel(x_prompt, x_sample, state_wkv, state_shift, state_pool, meta_tokens, norm_mix, norm_ffn, norm_final, rwkv_mix, rwkv_w_rkv, rwkv_decay_w1, rwkv_decay_w2, rwkv_decay_b, rwkv_iclr_w1, rwkv_iclr_w2, rwkv_iclr_b, rwkv_gate_w1, rwkv_gate_w2, rwkv_k_k, rwkv_k_a, rwkv_r_k, rwkv_ln_g, rwkv_ln_b, rwkv_w_o, pool_w, pool_scale, peer_wq, peer_keys, peer_u, peer_v):
    bp, seq, d = x_prompt.shape
    bs = x_sample.shape[0]
    tp = N_META + seq
    n_p = bp * tp
    heads = d // HEAD_DIM
    assert x_sample.shape[1] == 1 and bp == 8 and bs % 8 == 0 and heads * 8 == LANES
    assert rwkv_w_rkv.shape[0] == 1 and pool_w.shape[0] == 1 and norm_mix.shape[0] == 2

    blockdiag = jnp.asarray(np.kron(np.eye(LANES // HEAD_DIM), np.ones((HEAD_DIM, HEAD_DIM))), BF16)
    proj_params = _rwkv_proj_params(
        norm_mix[0], rwkv_mix[0], rwkv_w_rkv[0], rwkv_decay_w1[0], rwkv_decay_w2[0], rwkv_decay_b[0], rwkv_iclr_w1[0],
        rwkv_iclr_w2[0], rwkv_iclr_b[0], rwkv_gate_w1[0], rwkv_gate_w2[0], rwkv_k_k[0], rwkv_k_a[0],
        rwkv_r_k[0], blockdiag)
    w_o = rwkv_w_o[0].astype(BF16)
    peer_params = [_peer_params(norm_ffn[i], peer_wq[i], peer_keys[i], peer_u[i], peer_v[i]) for i in range(2)]
    pool_wb = pool_w[0].astype(BF16)
    meta = jnp.broadcast_to(meta_tokens[:, None], (N_META, bp, d))
    x0p = jnp.concatenate([meta, x_prompt.transpose(1, 0, 2)], axis=0).reshape(n_p, d)
    x0s = x_sample.reshape(bs, d)

    def rwkv_layer(x0, prev, s0, steps, time_groups):
        *scan_inputs, g, bonus = _rwkv_proj_call(x0, prev, proj_params, time_groups)
        y, s_out = _wkv_group(scan_inputs, s0, steps)
        x1 = _rwkv_out_call(y, bonus, g, x0, rwkv_ln_g[0], rwkv_ln_b[0], w_o, blockdiag, time_groups)
        return s_out, x1

    wkv_p, x1p = rwkv_layer(x0p, x0p, jnp.zeros((bp, heads, HEAD_DIM, HEAD_DIM), F32),
                            _time_block(tp, 48), True)
    wkv_s, x1s = rwkv_layer(x0s, state_shift[0], state_wkv[0], 1, False)
    f0 = _peer(x1p, x1s, peer_params[0])

    x2s, h1s = _add_norm_call(x1s, f0, norm_mix[1], col0=n_p)
    x3p = _pool_prompt_call(x1p, f0, norm_mix[1], pool_wb, pool_scale[0], bp)
    x3s = _pool_sample_call(state_pool[0].transpose(1, 0, 2), h1s, x2s, pool_wb, pool_scale[0])
    f1 = _peer(x3p, x3s, peer_params[1])
    y_prompt = _final_norm_prompt_call(x3p, f1, norm_final, bp, N_META)
    _, yfs = _add_norm_call(x3s, f1, norm_final, col0=n_p)

    nbuf = state_pool.shape[2]
    y_sample = yfs.reshape(bs, 1, d)
    shift_p = _norm_call(x0p[n_p - bp:], norm_mix[0])[None]
    shift_s = _norm_call(x0s, norm_mix[0])[None]
    keep = min(tp, nbuf) * bp
    last = -(-keep // LANES) * LANES
    tail = _add_norm_call(x1p[n_p - last:], f0[:, n_p - last:n_p], norm_mix[1])[1][last - keep:]
    tail = jnp.concatenate([jnp.zeros((nbuf * bp - keep, d), F32), tail], axis=0)
    pool_p = tail.reshape(nbuf, bp, d).transpose(1, 0, 2)[None]
    pool_s = jnp.concatenate([state_pool[0][:, 1:], h1s[:, None, :]], axis=1)[None]
    return (y_prompt, y_sample, wkv_p[None], shift_p, pool_p, wkv_s[None], shift_s, pool_s)
```

```python
import functools

import jax
import jax.numpy as jnp
import numpy as np
from jax import lax
from jax.experimental import pallas as pl
from jax.experimental.pallas import tpu as pltpu

F32 = jnp.float32
BF16 = jnp.bfloat16

N_META = 16
RMS_EPS = 1e-6
GN_EPS = 64e-5
HEAD_DIM = 64
LANES = 128
POOL_WINDOWS = (2, 4, 8, 16)
POOL_HALO = 16
PEER_HEADS = 8
PEER_NKEYS = 128
PEER_TOPK = 16
PEER_CHUNK = 2048
PEER_TOKENS = 512
MASK_DTYPE = jnp.bfloat16
MASK_ROWS = 16
ROW_TILE = 256
VMEM_LIMIT = 56 * 1024 * 1024

NEG_INF = float("-inf")


def _cparams(sem):
    return pltpu.CompilerParams(dimension_semantics=sem, vmem_limit_bytes=VMEM_LIMIT)


def _row_tile(n):
    return min(ROW_TILE, n)


def _rms(x, g):
    return x * lax.rsqrt(jnp.mean(x * x, axis=-1, keepdims=True) + RMS_EPS) * g


def _bdot(a, b):
    return jnp.dot(a.astype(BF16), b.astype(BF16), preferred_element_type=F32)


def _head_sum(x, blockdiag):
    outs = []
    for blk in range(x.shape[-1] // LANES):
        xb = x[:, blk * LANES:(blk + 1) * LANES]
        hi = xb.astype(BF16)
        lo = (xb - hi.astype(F32)).astype(BF16)
        outs.append(jnp.dot(hi, blockdiag, preferred_element_type=F32)
                    + jnp.dot(lo, blockdiag, preferred_element_type=F32))
    return jnp.concatenate(outs, axis=-1)


def _norm_kernel(x_ref, g_ref, h_ref):
    h_ref[...] = _rms(x_ref[...], g_ref[...])


def _norm_call(x, g):
    n, d = x.shape
    tm = _row_tile(n)
    row = pl.BlockSpec((tm, d), lambda i: (i, 0))
    return pl.pallas_call(
        _norm_kernel, grid=(pl.cdiv(n, tm),),
        in_specs=[row, pl.BlockSpec((1, d), lambda i: (0, 0))],
        out_specs=row, out_shape=jax.ShapeDtypeStruct((n, d), F32),
        compiler_params=_cparams(("parallel",)), name="rmsnorm")(x, g.reshape(1, d))


def _add_norm_kernel(a_ref, bt_ref, g_ref, x_ref, h_ref):
    x = a_ref[...] + bt_ref[...].T
    x_ref[...] = x
    h_ref[...] = _rms(x, g_ref[...])


def _add_norm_call(a, bt, g, col0=0):
    n, d = a.shape
    tm = _row_tile(n)
    assert col0 % tm == 0
    off = col0 // tm
    row = pl.BlockSpec((tm, d), lambda i: (i, 0))
    return pl.pallas_call(
        _add_norm_kernel, grid=(pl.cdiv(n, tm),),
        in_specs=[row, pl.BlockSpec((d, tm), lambda i: (0, i + off)), pl.BlockSpec((1, d), lambda i: (0, 0))],
        out_specs=[row, row],
        out_shape=[jax.ShapeDtypeStruct((n, d), F32)] * 2,
        compiler_params=_cparams(("parallel",)), name="add_rmsnorm")(a, bt, g.reshape(1, d))


def _final_norm_prompt_kernel(a_ref, bt_ref, g_ref, o_ref, h_ref, *, batch):
    h = _rms(a_ref[...] + bt_ref[...].T, g_ref[...])
    for c in range(h_ref.shape[0]):
        h_ref[c] = h[:, c * LANES:(c + 1) * LANES]

    @pl.when(pl.program_id(0) > 0)
    def _():
        steps = h_ref.shape[1] // batch
        for b in range(batch):
            for c in range(h_ref.shape[0]):
                o_ref[b, :, c * LANES:(c + 1) * LANES] = h_ref[c, pl.ds(b, steps, stride=batch), :]


def _final_norm_prompt_call(a, bt, g, batch, meta_steps):
    n, d = a.shape
    tm = meta_steps * batch
    assert n % tm == 0 and tm % LANES == 0
    return pl.pallas_call(
        functools.partial(_final_norm_prompt_kernel, batch=batch), grid=(n // tm,),
        in_specs=[pl.BlockSpec((tm, d), lambda i: (i, 0)), pl.BlockSpec((d, tm), lambda i: (0, i)),
                  pl.BlockSpec((1, d), lambda i: (0, 0))],
        out_specs=pl.BlockSpec((batch, meta_steps, d), lambda i: (0, jnp.maximum(i - 1, 0), 0)),
        out_shape=jax.ShapeDtypeStruct((batch, n // batch - meta_steps, d), F32),
        scratch_shapes=[pltpu.VMEM((d // LANES, tm, LANES), F32)],
        compiler_params=_cparams(("arbitrary",)), name="final_norm_prompt")(a, bt, g.reshape(1, d))


def _rows_to_scan(x8):
    pairs = x8.shape[-1] // LANES
    a = jnp.concatenate([x8[:, p * LANES:(p + 1) * LANES] for p in range(pairs)], axis=0)
    at = a.T
    return jnp.concatenate([at[:HEAD_DIM], at[HEAD_DIM:]], axis=1)


def _scan_to_rows(tile):
    at = jnp.concatenate([tile[:, :HEAD_DIM], tile[:, HEAD_DIM:]], axis=0)
    a = at.T
    return jnp.concatenate([a[p * 8:(p + 1) * 8, :] for p in range(a.shape[0] // 8)], axis=1)


def _store_scan(ref, x, time_groups):
    if not time_groups:
        ref[...] = x
        return
    for j in range(x.shape[0] // 8):
        ref[j] = _rows_to_scan(x[j * 8:(j + 1) * 8, :])


def _load_scan(ref, time_groups):
    if not time_groups:
        return ref[...]
    return jnp.concatenate([_scan_to_rows(ref[j]) for j in range(ref.shape[0])], axis=0)


def _scan_spec(tm, d, time_groups):
    if time_groups:
        return pl.BlockSpec((tm // 8, HEAD_DIM, LANES), lambda i: (i, 0, 0))
    return pl.BlockSpec((tm, d), lambda i: (i, 0))


def _scan_shape(n, d, time_groups):
    return (n // 8, HEAD_DIM, LANES) if time_groups else (n, d)


def _state_to_scan(s):
    b, heads, nv, nk = s.shape
    s = s.reshape(b // 8, 8, heads // 2, 2, nv, nk).transpose(5, 4, 0, 3, 2, 1)
    return s.reshape(nk, nv, b * heads)


def _state_from_scan(s, b):
    nk, nv, lanes = s.shape
    heads = lanes // b
    return s.reshape(nk, nv, b // 8, 2, heads // 2, 8).transpose(2, 5, 4, 3, 1, 0).reshape(b, heads, nv, nk)


def _rwkv_proj_kernel(x_ref, prev_ref, gn_ref, mix_ref, wrkv_ref, dw1_ref, dw2_ref, db_ref,
                      aw1_ref, aw2_ref, ab_ref, gw1_ref, gw2_ref, kk_ref, ka_ref, rk_ref, bd_ref,
                      r_ref, dec_ref, k_ref, v_ref, aa_ref, bb_ref, g_ref, bonus_ref, *, time_groups):
    h = _rms(x_ref[...], gn_ref[...])
    if time_groups:
        before = jnp.where(pl.program_id(0) == 0, 0.0, _rms(prev_ref[...], gn_ref[...]))
        prev = jnp.concatenate([before, h[:h.shape[0] - 8]], axis=0)
    else:
        prev = prev_ref[...]
    xx = prev - h

    def xs(c):
        return h + xx * mix_ref[c:c + 1, :]

    r = _bdot(xs(0), wrkv_ref[0])
    k = _bdot(xs(1), wrkv_ref[1])
    v = _bdot(xs(2), wrkv_ref[2])
    w = -jax.nn.softplus(-(db_ref[...] + _bdot(jnp.tanh(_bdot(xs(3), dw1_ref[...])), dw2_ref[...]))) - 0.5
    a = jax.nn.sigmoid(ab_ref[...] + _bdot(_bdot(xs(4), aw1_ref[...]), aw2_ref[...]))
    g = _bdot(jax.nn.sigmoid(_bdot(xs(5), gw1_ref[...])), gw2_ref[...])
    kk = k * kk_ref[...]
    nrm = jnp.sqrt(_head_sum(kk * kk, bd_ref[...]))
    kk = kk / jnp.maximum(nrm, 1e-12)
    kf = k * (1.0 + (a - 1.0) * ka_ref[...])
    _store_scan(r_ref, r, time_groups)
    _store_scan(dec_ref, jnp.exp(-jnp.exp(w)), time_groups)
    _store_scan(k_ref, kf, time_groups)
    _store_scan(v_ref, v, time_groups)
    _store_scan(aa_ref, -kk, time_groups)
    _store_scan(bb_ref, kk * a, time_groups)
    g_ref[...] = g
    bonus_ref[...] = _head_sum(r * kf * rk_ref[...], bd_ref[...]) * v


def _rwkv_proj_params(gn, mix, w_rkv, dw1, dw2, db, aw1, aw2, ab, gw1, gw2, k_k, k_a, r_k, blockdiag):
    d = mix.shape[-1]
    return [gn.reshape(1, d), mix, w_rkv.astype(BF16), dw1.astype(BF16), dw2.astype(BF16), db.reshape(1, d),
            aw1.astype(BF16), aw2.astype(BF16), ab.reshape(1, d), gw1.astype(BF16),
            gw2.astype(BF16), k_k.reshape(1, d), k_a.reshape(1, d), r_k.reshape(1, d), blockdiag]


def _rwkv_proj_call(x, prev, params, time_groups):
    n, d = x.shape
    tm = _row_tile(n)
    assert d == 8 * LANES and tm % 8 == 0
    row = pl.BlockSpec((tm, d), lambda i: (i, 0))
    scan = _scan_spec(tm, d, time_groups)
    prev_spec = pl.BlockSpec((8, d), lambda i: (jnp.maximum(i * (tm // 8) - 1, 0), 0)) if time_groups else row

    def full(x):
        nd = x.ndim
        return pl.BlockSpec(x.shape, lambda i: (0,) * nd)

    return pl.pallas_call(
        functools.partial(_rwkv_proj_kernel, time_groups=time_groups), grid=(pl.cdiv(n, tm),),
        in_specs=[row, prev_spec] + [full(p) for p in params],
        out_specs=[scan] * 6 + [row] * 2,
        out_shape=[jax.ShapeDtypeStruct(_scan_shape(n, d, time_groups), F32)] * 6
        + [jax.ShapeDtypeStruct((n, d), F32)] * 2,
        compiler_params=_cparams(("parallel",)), name="rwkv_proj")(x, prev, *params)


def _wkv_kernel(r_ref, d_ref, k_ref, v_ref, a_ref, b_ref, anext_ref, s0_ref, y_ref, s_ref, sa_ref, *, steps):
    nk = s_ref.shape[0]

    @pl.when(pl.program_id(1) == 0)
    def _():
        s_ref[...] = s0_ref[...]
        sa0 = None
        for kk in range(nk):
            term = s0_ref[kk] * a_ref[0, kk:kk + 1, :]
            sa0 = term if sa0 is None else sa0 + term
        sa_ref[...] = sa0

    def step(t, sa, next_row):
        vv = v_ref[t]
        y = None
        sa_next = None
        for kk in range(nk):
            sk = (s_ref[kk] * d_ref[t, kk:kk + 1, :] + sa * b_ref[t, kk:kk + 1, :]
                  + vv * k_ref[t, kk:kk + 1, :])
            s_ref[kk] = sk
            ty = sk * r_ref[t, kk:kk + 1, :]
            ta = sk * next_row(kk)
            y = ty if y is None else y + ty
            sa_next = ta if sa_next is None else sa_next + ta
        y_ref[t] = y
        return sa_next

    sa = lax.fori_loop(0, steps - 1, lambda t, sa: step(t, sa, lambda kk: a_ref[t + 1, kk:kk + 1, :]),
                       sa_ref[...])
    sa_ref[...] = step(steps - 1, sa, lambda kk: anext_ref[0, kk:kk + 1, :])


def _wkv_call(r, dec, k, v, aa, bb, s0, steps):
    t_len, hd, lanes = r.shape
    seq = pl.BlockSpec((steps, hd, LANES), lambda l, t: (t, 0, l))
    nxt = pl.BlockSpec((1, hd, LANES), lambda l, t: (jnp.minimum((t + 1) * steps, t_len - 1), 0, l))
    st = pl.BlockSpec((hd, hd, LANES), lambda l, t: (0, 0, l))
    return pl.pallas_call(
        functools.partial(_wkv_kernel, steps=steps),
        grid=(lanes // LANES, t_len // steps),
        in_specs=[seq] * 6 + [nxt, st],
        out_specs=[seq, st],
        out_shape=[jax.ShapeDtypeStruct((t_len, hd, lanes), F32),
                   jax.ShapeDtypeStruct((hd, hd, lanes), F32)],
        scratch_shapes=[pltpu.VMEM((hd, LANES), F32)],
        compiler_params=_cparams(("parallel", "arbitrary")), name="wkv_scan")(r, dec, k, v, aa, bb, aa, s0)


def _wkv_group(scan_inputs, s0, steps):
    y, s_l = _wkv_call(*scan_inputs, _state_to_scan(s0), steps)
    return y, _state_from_scan(s_l, s0.shape[0])


def _wkv_step_kernel(r_ref, d_ref, k_ref, v_ref, a_ref, b_ref, s0_ref, y_ref, s_ref):
    nb, heads, nv, nk = s0_ref.shape
    eye = (lax.broadcasted_iota(jnp.int32, (nv, nk), 0)
           == lax.broadcasted_iota(jnp.int32, (nv, nk), 1)).astype(F32)
    group = 8
    for b in range(nb):
        for h0 in range(0, heads, group):
            hs = range(h0, min(h0 + group, heads))
            sa = [jnp.sum(s0_ref[b, h] * a_ref[b, h:h + 1, :], axis=-1, keepdims=True) for h in hs]
            v_col = [jnp.sum(eye * v_ref[b, h:h + 1, :], axis=-1, keepdims=True) for h in hs]
            y_col = []
            for i, h in enumerate(hs):
                s_new = (s0_ref[b, h] * d_ref[b, h:h + 1, :] + sa[i] * b_ref[b, h:h + 1, :]
                         + v_col[i] * k_ref[b, h:h + 1, :])
                s_ref[b, h] = s_new
                y_col.append(jnp.sum(s_new * r_ref[b, h:h + 1, :], axis=-1, keepdims=True))
            for i, h in enumerate(hs):
                y_ref[b, h:h + 1, :] = jnp.sum(eye * y_col[i], axis=0, keepdims=True)


def _wkv_step_call(rows, s0):
    b, heads, nv, nk = s0.shape
    assert nv == nk
    nb = 8
    vec = pl.BlockSpec((nb, heads, nk), lambda i: (i, 0, 0))
    st = pl.BlockSpec((nb, heads, nv, nk), lambda i: (i, 0, 0, 0))
    y, s_out = pl.pallas_call(
        _wkv_step_kernel, grid=(b // nb,),
        in_specs=[vec] * 6 + [st], out_specs=[vec, st],
        out_shape=[jax.ShapeDtypeStruct((b, heads, nv), F32), jax.ShapeDtypeStruct(s0.shape, F32)],
        compiler_params=_cparams(("parallel",)), name="wkv_step")(
            *[x.reshape(b, heads, nk) for x in rows], s0)
    return y.reshape(b, heads * nv), s_out


def _rwkv_out_kernel(y_ref, bonus_ref, g_ref, x_ref, lng_ref, lnb_ref, wo_ref, bd_ref, o_ref, *, time_groups):
    bd = bd_ref[...]
    y = _load_scan(y_ref, time_groups)
    inv_n = 1.0 / HEAD_DIM
    mu = _head_sum(y, bd) * inv_n
    yc = y - mu
    var = _head_sum(yc * yc, bd) * inv_n
    yn = yc * lax.rsqrt(var + GN_EPS) * lng_ref[...] + lnb_ref[...]
    o_ref[...] = x_ref[...] + _bdot((yn + bonus_ref[...]) * g_ref[...], wo_ref[...])


def _rwkv_out_call(y, bonus, g, x, ln_g, ln_b, w_o, blockdiag, time_groups):
    n, d = x.shape
    tm = _row_tile(n)
    row = pl.BlockSpec((tm, d), lambda i: (i, 0))
    vec = pl.BlockSpec((1, d), lambda i: (0, 0))
    return pl.pallas_call(
        functools.partial(_rwkv_out_kernel, time_groups=time_groups), grid=(pl.cdiv(n, tm),),
        in_specs=[_scan_spec(tm, d, time_groups), row, row, row, vec, vec,
                  pl.BlockSpec((d, d), lambda i: (0, 0)), pl.BlockSpec((LANES, LANES), lambda i: (0, 0))],
        out_specs=row, out_shape=jax.ShapeDtypeStruct((n, d), F32),
        compiler_params=_cparams(("parallel",)), name="rwkv_out")(
            y, bonus, g, x, ln_g.reshape(1, d), ln_b.reshape(1, d), w_o, blockdiag)


def _peer_scores_kernel(x_ref, g_ref, wq_ref, keys_ref, s_ref, ht_ref):
    h = _rms(x_ref[...], g_ref[...])
    ht_ref[...] = h.T.astype(BF16)
    q = _bdot(h, wq_ref[...]).astype(BF16)
    for hc in range(keys_ref.shape[0]):
        s_ref[hc] = lax.dot_general(keys_ref[hc], q[:, hc * LANES:(hc + 1) * LANES],
                                    (((1,), (1,)), ((), ())), preferred_element_type=F32)


def _peer_scores_into_kernel(x_ref, g_ref, wq_ref, keys_ref, s_in_ref, ht_in_ref, s_ref, ht_ref):
    del s_in_ref, ht_in_ref
    _peer_scores_kernel(x_ref, g_ref, wq_ref, keys_ref, s_ref, ht_ref)


def _peer_scores_call(x, g, wq, keys, n_total, col0, into=None):
    n, d = x.shape
    tm = _row_tile(n)
    nhc = keys.shape[0]
    assert col0 % tm == 0
    off = col0 // tm
    in_specs = [pl.BlockSpec((tm, d), lambda i: (i, 0)), pl.BlockSpec((1, d), lambda i: (0, 0)),
                pl.BlockSpec(wq.shape, lambda i: (0, 0)), pl.BlockSpec(keys.shape, lambda i: (0, 0, 0))]
    args = [x, g.reshape(1, d), wq, keys]
    body, aliases = _peer_scores_kernel, {}
    if into is not None:
        in_specs += [pl.BlockSpec(memory_space=pl.ANY)] * 2
        args += list(into)
        body, aliases = _peer_scores_into_kernel, {4: 0, 5: 1}
    return pl.pallas_call(
        body, grid=(pl.cdiv(n, tm),), in_specs=in_specs,
        out_specs=[pl.BlockSpec((nhc, PEER_NKEYS, tm), lambda i: (0, 0, i + off)),
                   pl.BlockSpec((d, tm), lambda i: (0, i + off))],
        out_shape=[jax.ShapeDtypeStruct((nhc, PEER_NKEYS, n_total), F32),
                   jax.ShapeDtypeStruct((d, n_total), BF16)],
        input_output_aliases=aliases,
        compiler_params=_cparams(("parallel",)), name="peer_scores")(*args)


def _top16(s, iota, exact_ties):
    rank = jnp.full(s.shape, float(PEER_TOPK), F32)
    vals = []
    for a in range(PEER_TOPK):
        m = jnp.max(s, axis=0, keepdims=True)
        hit = s == m
        if exact_ties:
            idx = jnp.min(jnp.where(hit, iota, float(PEER_NKEYS)), axis=0, keepdims=True)
            hit = iota == idx
        rank = jnp.where(hit, float(a), rank)
        s = jnp.where(hit, NEG_INF, s)
        vals.append(m)
    ranked = jnp.sum(jnp.where(rank < float(PEER_TOPK), 1.0, 0.0), axis=0, keepdims=True)
    return rank, vals, ranked


def _dup_mask_bits(x):
    bits = lax.bitcast_convert_type(x.astype(MASK_DTYPE).astype(F32), jnp.uint32)
    return bits | (bits >> 16)


def _peer_select_kernel(s_ref, l_ref, p1_ref, r2_ref, p2_ref):
    miscount = _peer_select_body(s_ref, l_ref, p1_ref, r2_ref, p2_ref, exact_ties=False)

    @pl.when(jnp.max(miscount) > 0.0)
    def _():
        _peer_select_body(s_ref, l_ref, p1_ref, r2_ref, p2_ref, exact_ties=True)


def _peer_select_body(s_ref, l_ref, p1_ref, r2_ref, p2_ref, exact_ties):
    lanes = s_ref.shape[-1]
    iota = lax.broadcasted_iota(jnp.int32, (PEER_NKEYS, lanes), 0).astype(F32)
    iota16 = lax.broadcasted_iota(jnp.int32, (PEER_TOPK, lanes), 0).astype(F32)
    miscount = jnp.zeros((1, lanes), F32)
    for h in range(PEER_HEADS):
        s1 = s_ref[2 * h]
        s2 = s_ref[2 * h + 1]
        rank1, v1, n1 = _top16(s1, iota, exact_ties)
        rank2, v2, n2 = _top16(s2, iota, exact_ties)
        miscount = miscount + jnp.abs(n1 - float(PEER_TOPK)) + jnp.abs(n2 - float(PEER_TOPK))
        v1a = jnp.concatenate(v1, axis=0)
        v2a = jnp.concatenate(v2, axis=0)
        top = v1[0] + v2[0]
        taken = jnp.zeros((PEER_TOPK, lanes), F32)
        front = v1a + v2[0]
        zsum = jnp.zeros((1, lanes), F32)
        for _ in range(PEER_TOPK):
            m = jnp.max(front, axis=0, keepdims=True)
            a_star = jnp.min(jnp.where(front == m, iota16, float(PEER_TOPK)), axis=0, keepdims=True)
            hit = iota16 == a_star
            zsum = zsum + jnp.exp(m - top)
            taken = jnp.where(hit, taken + 1.0, taken)
            cnt = jnp.max(jnp.where(hit, taken, -1.0), axis=0, keepdims=True)
            nxt = jnp.max(jnp.where(iota16 == cnt, v2a, NEG_INF), axis=0, keepdims=True)
            front = jnp.where(hit, v1a + nxt, front)
        lim = jnp.zeros((PEER_NKEYS, lanes), F32)
        for a in range(PEER_TOPK):
            lim = jnp.where(rank1 == float(a), taken[a:a + 1, :], lim)
        l_ref[h] = _dup_mask_bits(lim)
        p1_ref[h] = _dup_mask_bits(jnp.exp(s1 - v1[0]))
        p2 = jnp.exp(s2 - v2[0]) / zsum
        for r in range(PEER_NKEYS // MASK_ROWS):
            rows = slice(r * MASK_ROWS, (r + 1) * MASK_ROWS)
            r2_ref[h, r] = rank2[rows].astype(MASK_DTYPE)
            p2_ref[h, r] = p2[rows].astype(MASK_DTYPE)
    return miscount


def _peer_select_call(scores):
    nhc, nk, n = scores.shape
    tl = LANES
    groups = nk // MASK_ROWS
    words = pl.BlockSpec((PEER_HEADS, nk, tl), lambda i: (0, 0, i))
    packed = pl.BlockSpec((PEER_HEADS, groups, MASK_ROWS, tl), lambda i: (0, 0, 0, i))
    return pl.pallas_call(
        _peer_select_kernel, grid=(pl.cdiv(n, tl),),
        in_specs=[pl.BlockSpec((nhc, nk, tl), lambda i: (0, 0, i))],
        out_specs=[words, words, packed, packed],
        out_shape=[jax.ShapeDtypeStruct((PEER_HEADS, nk, n), jnp.uint32)] * 2
        + [jax.ShapeDtypeStruct((PEER_HEADS, groups, MASK_ROWS, n), MASK_DTYPE)] * 2,
        compiler_params=_cparams(("parallel",)), name="peer_select")(scores)


def _gate_weights(cb, ii, heads, w, l_ref, p1_ref, r2_ref, p2_ref):
    blocks = PEER_CHUNK // PEER_NKEYS
    base = pl.multiple_of(cb * blocks + (ii // 8) * 8, 8)
    sub = ii % 8
    w = list(w)
    for h in heads:
        lw = l_ref[h, pl.ds(base, 8), :]
        pw = p1_ref[h, pl.ds(base, 8), :]
        lim = pltpu.bitcast(jnp.broadcast_to(lw[sub:sub + 1, :], lw.shape), MASK_DTYPE)
        p1 = pltpu.bitcast(jnp.broadcast_to(pw[sub:sub + 1, :], pw.shape), MASK_DTYPE)
        for r in range(len(w)):
            term = jnp.where(r2_ref[h, r] < lim, p2_ref[h, r], jnp.zeros_like(p1)) * p1
            w[r] = term if w[r] is None else w[r] + term
    return w


def _gate_store(ii, w, z_ref, g_ref):
    for r in range(len(w)):
        rows = slice(ii * PEER_NKEYS + r * MASK_ROWS, ii * PEER_NKEYS + (r + 1) * MASK_ROWS)
        z = z_ref[rows, :]
        act = 0.5 * z * (1.0 + lax.erf(z * np.float32(np.sqrt(0.5))))
        g_ref[rows, :] = (act.astype(MASK_DTYPE) * w[r]).astype(g_ref.dtype)


def _peer_dense_kernel(ht_ref, u_ref, v_ref, l_ref, p1_ref, r2_ref, p2_ref, o_ref,
                       z0_ref, z1_ref, g0_ref, g1_ref, acc_ref, *, items, chunks):
    s = pl.program_id(0)

    @pl.when(s == 0)
    def _():
        for ref in (z0_ref, z1_ref, g0_ref, g1_ref):
            ref[...] = jnp.zeros_like(ref)

    gate_chunk = jnp.clip(s - 1, 0, items - 1) % chunks

    def stages(z_new, z_old, g_new, g_old):
        blocks = PEER_CHUNK // PEER_NKEYS
        sel = (l_ref, p1_ref, r2_ref, p2_ref)
        half = PEER_HEADS // 2
        per_v = blocks // (o_ref.shape[0] // PEER_NKEYS)
        for ii in range(blocks):
            er = slice(ii * PEER_NKEYS, (ii + 1) * PEER_NKEYS)
            w = _gate_weights(gate_chunk, ii, range(half), [None] * (PEER_NKEYS // MASK_ROWS), *sel)
            z_new[er, :] = jnp.dot(u_ref[er, :], ht_ref[...], preferred_element_type=F32)
            w = _gate_weights(gate_chunk, ii, range(half, PEER_HEADS), w, *sel)
            _gate_store(ii, w, z_old, g_new)
            if ii % per_v == per_v - 1:
                dr = slice((ii // per_v) * PEER_NKEYS, (ii // per_v + 1) * PEER_NKEYS)
                acc_ref[dr, :] = lax.dot_general(v_ref[:, dr], g_old[...], (((0,), (0,)), ((), ())),
                                                 preferred_element_type=F32)

    @pl.when(s % 2 == 0)
    def _():
        stages(z0_ref, z1_ref, g1_ref, g0_ref)

    @pl.when(s % 2 == 1)
    def _():
        stages(z1_ref, z0_ref, g0_ref, g1_ref)

    first = jnp.clip(s - 2, 0, items - 1) % chunks == 0

    @pl.when(first)
    def _():
        o_ref[...] = acc_ref[...]

    @pl.when(jnp.logical_not(first))
    def _():
        o_ref[...] += acc_ref[...]


def _peer_dense_call(ht, u, v, lim, p1, r2, p2):
    d, n = ht.shape
    chunks = u.shape[0] // PEER_CHUNK
    tt = min(PEER_TOKENS, n)
    items = pl.cdiv(n, tt) * chunks

    def item(s, lag):
        return jnp.clip(s - lag, 0, items - 1)

    words = pl.BlockSpec((PEER_HEADS, PEER_NKEYS, tt), lambda s: (0, 0, item(s, 1) // chunks))
    packed = pl.BlockSpec(r2.shape[:3] + (tt,), lambda s: (0, 0, 0, item(s, 1) // chunks))
    return pl.pallas_call(
        functools.partial(_peer_dense_kernel, items=items, chunks=chunks),
        grid=(items + 2,),
        in_specs=[pl.BlockSpec((d, tt), lambda s: (0, item(s, 0) // chunks)),
                  pl.BlockSpec((PEER_CHUNK, d), lambda s: (item(s, 0) % chunks, 0)),
                  pl.BlockSpec((PEER_CHUNK, d), lambda s: (item(s, 2) % chunks, 0)),
                  words, words, packed, packed],
        out_specs=pl.BlockSpec((d, tt), lambda s: (0, item(s, 2) // chunks)),
        out_shape=jax.ShapeDtypeStruct((d, n), F32),
        scratch_shapes=[pltpu.VMEM((PEER_CHUNK, tt), F32)] * 2 + [pltpu.VMEM((PEER_CHUNK, tt), BF16)] * 2
        + [pltpu.VMEM((d, tt), F32)],
        compiler_params=_cparams(("arbitrary",)), name="peer_dense")(ht, u, v, lim, p1, r2, p2)


def _peer_params(g, wq, keys, u, v):
    nh, _, nk, half = keys.shape
    return g, wq.astype(BF16), keys.reshape(nh * 2, nk, half).astype(BF16), u.astype(BF16), v.astype(BF16)


def _peer(x_a, x_b, params):
    g, wq, keys, u, v = params
    n_a, n_b = x_a.shape[0], x_b.shape[0]
    first = _peer_scores_call(x_a, g, wq, keys, n_a + n_b, 0)
    scores, ht = _peer_scores_call(x_b, g, wq, keys, n_a + n_b, n_a, into=first)
    lim, p1, r2, p2 = _peer_select_call(scores)
    return _peer_dense_call(ht, u, v, lim, p1, r2, p2)


def _pool_prompt_kernel(xh_ref, fth_ref, x_ref, ft_ref, gn_ref, w_ref, sc_ref, o_ref, *, batch):
    tm = x_ref.shape[0]
    hr = xh_ref.shape[0]
    x = x_ref[...] + ft_ref[...].T
    cur = _rms(x, gn_ref[...])
    halo = jnp.where(pl.program_id(0) == 0, 0.0,
                     _rms(xh_ref[...] + fth_ref[...].T, gn_ref[...]))
    ext = jnp.concatenate([halo, cur], axis=0)
    row = (pl.program_id(0) * tm + lax.broadcasted_iota(jnp.int32, (tm, 1), 0)).astype(F32)
    pos = jnp.floor(row / float(batch))
    gd = cur.shape[-1] // len(POOL_WINDOWS)
    outs = []
    for gi, w in enumerate(POOL_WINDOWS):
        acc = ext[:, gi * gd:(gi + 1) * gd]
        dropped = 0
        shift = 1
        while shift < w:
            k = shift * batch
            acc = acc[k:] + acc[:-k]
            dropped += k
            shift *= 2
        cnt = jnp.minimum(float(w), pos + 1.0)
        mixed = acc[hr - dropped:] / cnt - cur[:, gi * gd:(gi + 1) * gd]
        outs.append(_bdot(mixed, w_ref[gi]))
    o_ref[...] = x + jnp.concatenate(outs, axis=-1) * sc_ref[...]


def _pool_prompt_call(x, ft, gn, pool_w, pool_scale, batch):
    n, d = x.shape
    tm = _row_tile(n)
    hr = POOL_HALO * batch
    assert tm % hr == 0 and hr % LANES == 0

    def before(i):
        return jnp.maximum(i * (tm // hr) - 1, 0)

    vec = pl.BlockSpec((1, d), lambda i: (0, 0))
    return pl.pallas_call(
        functools.partial(_pool_prompt_kernel, batch=batch), grid=(pl.cdiv(n, tm),),
        in_specs=[pl.BlockSpec((hr, d), lambda i: (before(i), 0)), pl.BlockSpec((d, hr), lambda i: (0, before(i))),
                  pl.BlockSpec((tm, d), lambda i: (i, 0)), pl.BlockSpec((d, tm), lambda i: (0, i)), vec,
                  pl.BlockSpec(pool_w.shape, lambda i: (0, 0, 0)), vec],
        out_specs=pl.BlockSpec((tm, d), lambda i: (i, 0)), out_shape=jax.ShapeDtypeStruct((n, d), F32),
        compiler_params=_cparams(("parallel",)), name="pool_prompt")(
            x, ft, x, ft, gn.reshape(1, d), pool_w, pool_scale.reshape(1, d))


def _pool_sample_kernel(hist_ref, h_ref, x_ref, w_ref, sc_ref, o_ref):
    h = h_ref[...]
    nhist = hist_ref.shape[0]
    gd = h.shape[-1] // len(POOL_WINDOWS)
    outs = []
    for gi, w in enumerate(POOL_WINDOWS):
        sl = slice(gi * gd, (gi + 1) * gd)
        acc = h[:, sl]
        for back in range(1, w):
            acc = acc + hist_ref[nhist - back][:, sl]
        mixed = acc / float(w) - h[:, sl]
        outs.append(_bdot(mixed, w_ref[gi]))
    o_ref[...] = x_ref[...] + jnp.concatenate(outs, axis=-1) * sc_ref[...]


def _pool_sample_call(hist, h, x, pool_w, pool_scale):
    b, d = h.shape
    full2 = pl.BlockSpec((b, d), lambda i: (0, 0))
    return pl.pallas_call(
        _pool_sample_kernel, grid=(1,),
        in_specs=[pl.BlockSpec(hist.shape, lambda i: (0, 0, 0)), full2, full2,
                  pl.BlockSpec(pool_w.shape, lambda i: (0, 0, 0)),
                  pl.BlockSpec((1, d), lambda i: (0, 0))],
        out_specs=full2, out_shape=jax.ShapeDtypeStruct((b, d), F32),
        compiler_params=_cparams(("arbitrary",)), name="pool_sample")(
            hist, h, x, pool_w, pool_scale.reshape(1, d))


def _time_block(t, pref):
    for c in range(min(pref, t), 0, -1):
        if t % c == 0 and (c % 8 == 0 or c == t):
            return c
    return t


def kernel(x_prompt, x_sample, state_wkv, state_shift, state_pool, meta_tokens, norm_mix, norm_ffn, norm_final, rwkv_mix, rwkv_w_rkv, rwkv_decay_w1, rwkv_decay_w2, rwkv_decay_b, rwkv_iclr_w1, rwkv_iclr_w2, rwkv_iclr_b, rwkv_gate_w1, rwkv_gate_w2, rwkv_k_k, rwkv_k_a, rwkv_r_k, rwkv_ln_g, rwkv_ln_b, rwkv_w_o, pool_w, pool_scale, peer_wq, peer_keys, peer_u, peer_v):
    bp, seq, d = x_prompt.shape
    bs = x_sample.shape[0]
    tp = N_META + seq
    n_p = bp * tp
    heads = d // HEAD_DIM
    assert x_sample.shape[1] == 1 and bp == 8 and bs % 8 == 0 and heads * 8 == LANES
    assert rwkv_w_rkv.shape[0] == 1 and pool_w.shape[0] == 1 and norm_mix.shape[0] == 2

    blockdiag = jnp.asarray(np.kron(np.eye(LANES // HEAD_DIM), np.ones((HEAD_DIM, HEAD_DIM))), BF16)
    proj_params = _rwkv_proj_params(
        norm_mix[0], rwkv_mix[0], rwkv_w_rkv[0], rwkv_decay_w1[0], rwkv_decay_w2[0], rwkv_decay_b[0], rwkv_iclr_w1[0],
        rwkv_iclr_w2[0], rwkv_iclr_b[0], rwkv_gate_w1[0], rwkv_gate_w2[0], rwkv_k_k[0], rwkv_k_a[0],
        rwkv_r_k[0], blockdiag)
    w_o = rwkv_w_o[0].astype(BF16)
    peer_params = [_peer_params(norm_ffn[i], peer_wq[i], peer_keys[i], peer_u[i], peer_v[i]) for i in range(2)]
    pool_wb = pool_w[0].astype(BF16)
    meta = jnp.broadcast_to(meta_tokens[:, None], (N_META, bp, d))
    x0p = jnp.concatenate([meta, x_prompt.transpose(1, 0, 2)], axis=0).reshape(n_p, d)
    x0s = x_sample.reshape(bs, d)

    def rwkv_layer(x0, prev, s0, time_groups):
        *scan_inputs, g, bonus = _rwkv_proj_call(x0, prev, proj_params, time_groups)
        if time_groups:
            y, s_out = _wkv_group(scan_inputs, s0, _time_block(tp, 48))
        else:
            y, s_out = _wkv_step_call(scan_inputs, s0)
        x1 = _rwkv_out_call(y, bonus, g, x0, rwkv_ln_g[0], rwkv_ln_b[0], w_o, blockdiag, time_groups)
        return s_out, x1

    wkv_p, x1p = rwkv_layer(x0p, x0p, jnp.zeros((bp, heads, HEAD_DIM, HEAD_DIM), F32), True)
    wkv_s, x1s = rwkv_layer(x0s, state_shift[0], state_wkv[0], False)
    f0 = _peer(x1p, x1s, peer_params[0])

    x2s, h1s = _add_norm_call(x1s, f0, norm_mix[1], col0=n_p)
    x3p = _pool_prompt_call(x1p, f0, norm_mix[1], pool_wb, pool_scale[0], bp)
    x3s = _pool_sample_call(state_pool[0].transpose(1, 0, 2), h1s, x2s, pool_wb, pool_scale[0])
    f1 = _peer(x3p, x3s, peer_params[1])
    y_prompt = _final_norm_prompt_call(x3p, f1, norm_final, bp, N_META)
    _, yfs = _add_norm_call(x3s, f1, norm_final, col0=n_p)

    nbuf = state_pool.shape[2]
    y_sample = yfs.reshape(bs, 1, d)
    shift_p = _norm_call(x0p[n_p - bp:], norm_mix[0])[None]
    shift_s = _norm_call(x0s, norm_mix[0])[None]
    keep = min(tp, nbuf) * bp
    last = -(-keep // LANES) * LANES
    tail = _add_norm_call(x1p[n_p - last:], f0[:, n_p - last:n_p], norm_mix[1])[1][last - keep:]
    tail = jnp.concatenate([jnp.zeros((nbuf * bp - keep, d), F32), tail], axis=0)
    pool_p = tail.reshape(nbuf, bp, d).transpose(1, 0, 2)[None]
    pool_s = jnp.concatenate([state_pool[0][:, 1:], h1s[:, None, :]], axis=1)[None]
    return (y_prompt, y_sample, wkv_p[None], shift_p, pool_p, wkv_s[None], shift_s, pool_s)
```

```python
import functools

import jax
import jax.numpy as jnp
import numpy as np
from jax import lax
from jax.experimental import pallas as pl
from jax.experimental.pallas import tpu as pltpu

F32 = jnp.float32
BF16 = jnp.bfloat16

N_META = 16
RMS_EPS = 1e-6
GN_EPS = 64e-5
HEAD_DIM = 64
LANES = 128
POOL_WINDOWS = (2, 4, 8, 16)
POOL_HALO = 16
PEER_HEADS = 8
PEER_NKEYS = 128
PEER_TOPK = 16
PEER_CHUNK = 2048
PEER_TOKENS = 512
MASK_DTYPE = jnp.bfloat16
MASK_ROWS = 16
ROW_TILE = 256
VMEM_LIMIT = 56 * 1024 * 1024

NEG_INF = float("-inf")


def _cparams(sem):
    return pltpu.CompilerParams(dimension_semantics=sem, vmem_limit_bytes=VMEM_LIMIT)


def _row_tile(n):
    return min(ROW_TILE, n)


def _rms(x, g):
    return x * lax.rsqrt(jnp.mean(x * x, axis=-1, keepdims=True) + RMS_EPS) * g


def _bdot(a, b):
    return jnp.dot(a.astype(BF16), b.astype(BF16), preferred_element_type=F32)


def _head_sum(x, blockdiag):
    outs = []
    for blk in range(x.shape[-1] // LANES):
        xb = x[:, blk * LANES:(blk + 1) * LANES]
        hi = xb.astype(BF16)
        lo = (xb - hi.astype(F32)).astype(BF16)
        outs.append(jnp.dot(hi, blockdiag, preferred_element_type=F32)
                    + jnp.dot(lo, blockdiag, preferred_element_type=F32))
    return jnp.concatenate(outs, axis=-1)


def _norm_kernel(x_ref, g_ref, h_ref):
    h_ref[...] = _rms(x_ref[...], g_ref[...])


def _norm_call(x, g):
    n, d = x.shape
    tm = _row_tile(n)
    row = pl.BlockSpec((tm, d), lambda i: (i, 0))
    return pl.pallas_call(
        _norm_kernel, grid=(pl.cdiv(n, tm),),
        in_specs=[row, pl.BlockSpec((1, d), lambda i: (0, 0))],
        out_specs=row, out_shape=jax.ShapeDtypeStruct((n, d), F32),
        compiler_params=_cparams(("parallel",)), name="rmsnorm")(x, g.reshape(1, d))


def _add_norm_kernel(a_ref, bt_ref, g_ref, x_ref, h_ref):
    x = a_ref[...] + bt_ref[...].T
    x_ref[...] = x
    h_ref[...] = _rms(x, g_ref[...])


def _add_norm_call(a, bt, g, col0=0):
    n, d = a.shape
    tm = _row_tile(n)
    assert col0 % tm == 0
    off = col0 // tm
    row = pl.BlockSpec((tm, d), lambda i: (i, 0))
    return pl.pallas_call(
        _add_norm_kernel, grid=(pl.cdiv(n, tm),),
        in_specs=[row, pl.BlockSpec((d, tm), lambda i: (0, i + off)), pl.BlockSpec((1, d), lambda i: (0, 0))],
        out_specs=[row, row],
        out_shape=[jax.ShapeDtypeStruct((n, d), F32)] * 2,
        compiler_params=_cparams(("parallel",)), name="add_rmsnorm")(a, bt, g.reshape(1, d))


def _final_norm_prompt_kernel(a_ref, bt_ref, g_ref, o_ref, h_ref, *, batch):
    h = _rms(a_ref[...] + bt_ref[...].T, g_ref[...])
    for c in range(h_ref.shape[0]):
        h_ref[c] = h[:, c * LANES:(c + 1) * LANES]

    @pl.when(pl.program_id(0) > 0)
    def _():
        steps = h_ref.shape[1] // batch
        for b in range(batch):
            for c in range(h_ref.shape[0]):
                o_ref[b, :, c * LANES:(c + 1) * LANES] = h_ref[c, pl.ds(b, steps, stride=batch), :]


def _final_norm_prompt_call(a, bt, g, batch, meta_steps):
    n, d = a.shape
    tm = meta_steps * batch
    assert n % tm == 0 and tm % LANES == 0
    return pl.pallas_call(
        functools.partial(_final_norm_prompt_kernel, batch=batch), grid=(n // tm,),
        in_specs=[pl.BlockSpec((tm, d), lambda i: (i, 0)), pl.BlockSpec((d, tm), lambda i: (0, i)),
                  pl.BlockSpec((1, d), lambda i: (0, 0))],
        out_specs=pl.BlockSpec((batch, meta_steps, d), lambda i: (0, jnp.maximum(i - 1, 0), 0)),
        out_shape=jax.ShapeDtypeStruct((batch, n // batch - meta_steps, d), F32),
        scratch_shapes=[pltpu.VMEM((d // LANES, tm, LANES), F32)],
        compiler_params=_cparams(("arbitrary",)), name="final_norm_prompt")(a, bt, g.reshape(1, d))


def _rows_to_scan(x8):
    pairs = x8.shape[-1] // LANES
    a = jnp.concatenate([x8[:, p * LANES:(p + 1) * LANES] for p in range(pairs)], axis=0)
    at = a.T
    return jnp.concatenate([at[:HEAD_DIM], at[HEAD_DIM:]], axis=1)


def _scan_to_rows(tile):
    at = jnp.concatenate([tile[:, :HEAD_DIM], tile[:, HEAD_DIM:]], axis=0)
    a = at.T
    return jnp.concatenate([a[p * 8:(p + 1) * 8, :] for p in range(a.shape[0] // 8)], axis=1)


def _store_scan(ref, x, time_groups):
    if not time_groups:
        ref[...] = x
        return
    for j in range(x.shape[0] // 8):
        ref[j] = _rows_to_scan(x[j * 8:(j + 1) * 8, :])


def _load_scan(ref, time_groups):
    if not time_groups:
        return ref[...]
    return jnp.concatenate([_scan_to_rows(ref[j]) for j in range(ref.shape[0])], axis=0)


def _scan_spec(tm, d, time_groups):
    if time_groups:
        return pl.BlockSpec((tm // 8, HEAD_DIM, LANES), lambda i: (i, 0, 0))
    return pl.BlockSpec((tm, d), lambda i: (i, 0))


def _scan_shape(n, d, time_groups):
    return (n // 8, HEAD_DIM, LANES) if time_groups else (n, d)


def _state_to_scan(s):
    b, heads, nv, nk = s.shape
    s = s.reshape(b // 8, 8, heads // 2, 2, nv, nk).transpose(5, 4, 0, 3, 2, 1)
    return s.reshape(nk, nv, b * heads)


def _state_from_scan(s, b):
    nk, nv, lanes = s.shape
    heads = lanes // b
    return s.reshape(nk, nv, b // 8, 2, heads // 2, 8).transpose(2, 5, 4, 3, 1, 0).reshape(b, heads, nv, nk)


def _rwkv_proj_kernel(x_ref, prev_ref, gn_ref, mix_ref, wrkv_ref, dw1_ref, dw2_ref, db_ref,
                      aw1_ref, aw2_ref, ab_ref, gw1_ref, gw2_ref, kk_ref, ka_ref, rk_ref, bd_ref,
                      r_ref, dec_ref, k_ref, v_ref, aa_ref, bb_ref, g_ref, bonus_ref, *, time_groups):
    h = _rms(x_ref[...], gn_ref[...])
    if time_groups:
        before = jnp.where(pl.program_id(0) == 0, 0.0, _rms(prev_ref[...], gn_ref[...]))
        prev = jnp.concatenate([before, h[:h.shape[0] - 8]], axis=0)
    else:
        prev = prev_ref[...]
    xx = prev - h

    def xs(c):
        return h + xx * mix_ref[c:c + 1, :]

    r = _bdot(xs(0), wrkv_ref[0])
    k = _bdot(xs(1), wrkv_ref[1])
    v = _bdot(xs(2), wrkv_ref[2])
    w = -jax.nn.softplus(-(db_ref[...] + _bdot(jnp.tanh(_bdot(xs(3), dw1_ref[...])), dw2_ref[...]))) - 0.5
    a = jax.nn.sigmoid(ab_ref[...] + _bdot(_bdot(xs(4), aw1_ref[...]), aw2_ref[...]))
    g = _bdot(jax.nn.sigmoid(_bdot(xs(5), gw1_ref[...])), gw2_ref[...])
    kk = k * kk_ref[...]
    nrm = jnp.sqrt(_head_sum(kk * kk, bd_ref[...]))
    kk = kk / jnp.maximum(nrm, 1e-12)
    kf = k * (1.0 + (a - 1.0) * ka_ref[...])
    _store_scan(r_ref, r, time_groups)
    _store_scan(dec_ref, jnp.exp(-jnp.exp(w)), time_groups)
    _store_scan(k_ref, kf, time_groups)
    _store_scan(v_ref, v, time_groups)
    _store_scan(aa_ref, -kk, time_groups)
    _store_scan(bb_ref, kk * a, time_groups)
    g_ref[...] = g
    bonus_ref[...] = _head_sum(r * kf * rk_ref[...], bd_ref[...]) * v


def _rwkv_proj_params(gn, mix, w_rkv, dw1, dw2, db, aw1, aw2, ab, gw1, gw2, k_k, k_a, r_k, blockdiag):
    d = mix.shape[-1]
    return [gn.reshape(1, d), mix, w_rkv.astype(BF16), dw1.astype(BF16), dw2.astype(BF16), db.reshape(1, d),
            aw1.astype(BF16), aw2.astype(BF16), ab.reshape(1, d), gw1.astype(BF16),
            gw2.astype(BF16), k_k.reshape(1, d), k_a.reshape(1, d), r_k.reshape(1, d), blockdiag]


def _rwkv_proj_call(x, prev, params, time_groups):
    n, d = x.shape
    tm = _row_tile(n)
    assert d == 8 * LANES and tm % 8 == 0
    row = pl.BlockSpec((tm, d), lambda i: (i, 0))
    scan = _scan_spec(tm, d, time_groups)
    prev_spec = pl.BlockSpec((8, d), lambda i: (jnp.maximum(i * (tm // 8) - 1, 0), 0)) if time_groups else row

    def full(x):
        nd = x.ndim
        return pl.BlockSpec(x.shape, lambda i: (0,) * nd)

    return pl.pallas_call(
        functools.partial(_rwkv_proj_kernel, time_groups=time_groups), grid=(pl.cdiv(n, tm),),
        in_specs=[row, prev_spec] + [full(p) for p in params],
        out_specs=[scan] * 6 + [row] * 2,
        out_shape=[jax.ShapeDtypeStruct(_scan_shape(n, d, time_groups), F32)] * 6
        + [jax.ShapeDtypeStruct((n, d), F32)] * 2,
        compiler_params=_cparams(("parallel",)), name="rwkv_proj")(x, prev, *params)


def _wkv_kernel(r_ref, d_ref, k_ref, v_ref, a_ref, b_ref, anext_ref, s0_ref, y_ref, s_ref, sa_ref, *, steps):
    nk = s_ref.shape[0]

    @pl.when(pl.program_id(1) == 0)
    def _():
        s_ref[...] = s0_ref[...]
        sa0 = None
        for kk in range(nk):
            term = s0_ref[kk] * a_ref[0, kk:kk + 1, :]
            sa0 = term if sa0 is None else sa0 + term
        sa_ref[...] = sa0

    def step(t, sa, next_row):
        vv = v_ref[t]
        y = None
        sa_next = None
        for kk in range(nk):
            sk = (s_ref[kk] * d_ref[t, kk:kk + 1, :] + sa * b_ref[t, kk:kk + 1, :]
                  + vv * k_ref[t, kk:kk + 1, :])
            s_ref[kk] = sk
            ty = sk * r_ref[t, kk:kk + 1, :]
            ta = sk * next_row(kk)
            y = ty if y is None else y + ty
            sa_next = ta if sa_next is None else sa_next + ta
        y_ref[t] = y
        return sa_next

    sa = lax.fori_loop(0, steps - 1, lambda t, sa: step(t, sa, lambda kk: a_ref[t + 1, kk:kk + 1, :]),
                       sa_ref[...])
    sa_ref[...] = step(steps - 1, sa, lambda kk: anext_ref[0, kk:kk + 1, :])


def _wkv_call(r, dec, k, v, aa, bb, s0, steps):
    t_len, hd, lanes = r.shape
    seq = pl.BlockSpec((steps, hd, LANES), lambda l, t: (t, 0, l))
    nxt = pl.BlockSpec((1, hd, LANES), lambda l, t: (jnp.minimum((t + 1) * steps, t_len - 1), 0, l))
    st = pl.BlockSpec((hd, hd, LANES), lambda l, t: (0, 0, l))
    return pl.pallas_call(
        functools.partial(_wkv_kernel, steps=steps),
        grid=(lanes // LANES, t_len // steps),
        in_specs=[seq] * 6 + [nxt, st],
        out_specs=[seq, st],
        out_shape=[jax.ShapeDtypeStruct((t_len, hd, lanes), F32),
                   jax.ShapeDtypeStruct((hd, hd, lanes), F32)],
        scratch_shapes=[pltpu.VMEM((hd, LANES), F32)],
        compiler_params=_cparams(("parallel", "arbitrary")), name="wkv_scan")(r, dec, k, v, aa, bb, aa, s0)


def _wkv_group(scan_inputs, s0, steps):
    y, s_l = _wkv_call(*scan_inputs, _state_to_scan(s0), steps)
    return y, _state_from_scan(s_l, s0.shape[0])


def _wkv_step_kernel(r_ref, d_ref, k_ref, v_ref, a_ref, b_ref, s0_ref, y_ref, s_ref):
    nb, heads, nv, nk = s0_ref.shape
    eye = (lax.broadcasted_iota(jnp.int32, (nv, nk), 0)
           == lax.broadcasted_iota(jnp.int32, (nv, nk), 1)).astype(F32)
    group = 8
    for b in range(nb):
        for h0 in range(0, heads, group):
            hs = range(h0, min(h0 + group, heads))
            sa = [jnp.sum(s0_ref[b, h] * a_ref[b, h:h + 1, :], axis=-1, keepdims=True) for h in hs]
            v_col = [jnp.sum(eye * v_ref[b, h:h + 1, :], axis=-1, keepdims=True) for h in hs]
            y_col = []
            for i, h in enumerate(hs):
                s_new = (s0_ref[b, h] * d_ref[b, h:h + 1, :] + sa[i] * b_ref[b, h:h + 1, :]
                         + v_col[i] * k_ref[b, h:h + 1, :])
                s_ref[b, h] = s_new
                y_col.append(jnp.sum(s_new * r_ref[b, h:h + 1, :], axis=-1, keepdims=True))
            for i, h in enumerate(hs):
                y_ref[b, h:h + 1, :] = jnp.sum(eye * y_col[i], axis=0, keepdims=True)


def _wkv_step_call(rows, s0):
    b, heads, nv, nk = s0.shape
    assert nv == nk
    nb = 8
    vec = pl.BlockSpec((nb, heads, nk), lambda i: (i, 0, 0))
    st = pl.BlockSpec((nb, heads, nv, nk), lambda i: (i, 0, 0, 0))
    y, s_out = pl.pallas_call(
        _wkv_step_kernel, grid=(b // nb,),
        in_specs=[vec] * 6 + [st], out_specs=[vec, st],
        out_shape=[jax.ShapeDtypeStruct((b, heads, nv), F32), jax.ShapeDtypeStruct(s0.shape, F32)],
        compiler_params=_cparams(("parallel",)), name="wkv_step")(
            *[x.reshape(b, heads, nk) for x in rows], s0)
    return y.reshape(b, heads * nv), s_out


def _rwkv_out_kernel(y_ref, bonus_ref, g_ref, x_ref, lng_ref, lnb_ref, wo_ref, bd_ref, o_ref, *, time_groups):
    bd = bd_ref[...]
    y = _load_scan(y_ref, time_groups)
    inv_n = 1.0 / HEAD_DIM
    mu = _head_sum(y, bd) * inv_n
    yc = y - mu
    var = _head_sum(yc * yc, bd) * inv_n
    yn = yc * lax.rsqrt(var + GN_EPS) * lng_ref[...] + lnb_ref[...]
    o_ref[...] = x_ref[...] + _bdot((yn + bonus_ref[...]) * g_ref[...], wo_ref[...])


def _rwkv_out_call(y, bonus, g, x, ln_g, ln_b, w_o, blockdiag, time_groups):
    n, d = x.shape
    tm = _row_tile(n)
    row = pl.BlockSpec((tm, d), lambda i: (i, 0))
    vec = pl.BlockSpec((1, d), lambda i: (0, 0))
    return pl.pallas_call(
        functools.partial(_rwkv_out_kernel, time_groups=time_groups), grid=(pl.cdiv(n, tm),),
        in_specs=[_scan_spec(tm, d, time_groups), row, row, row, vec, vec,
                  pl.BlockSpec((d, d), lambda i: (0, 0)), pl.BlockSpec((LANES, LANES), lambda i: (0, 0))],
        out_specs=row, out_shape=jax.ShapeDtypeStruct((n, d), F32),
        compiler_params=_cparams(("parallel",)), name="rwkv_out")(
            y, bonus, g, x, ln_g.reshape(1, d), ln_b.reshape(1, d), w_o, blockdiag)


def _peer_scores_kernel(x_ref, g_ref, wq_ref, keys_ref, s_ref, ht_ref):
    h = _rms(x_ref[...], g_ref[...])
    ht_ref[...] = h.T.astype(BF16)
    q = _bdot(h, wq_ref[...]).astype(BF16)
    for hc in range(keys_ref.shape[0]):
        s_ref[hc] = lax.dot_general(keys_ref[hc], q[:, hc * LANES:(hc + 1) * LANES],
                                    (((1,), (1,)), ((), ())), preferred_element_type=F32)


def _peer_scores_into_kernel(x_ref, g_ref, wq_ref, keys_ref, s_in_ref, ht_in_ref, s_ref, ht_ref):
    del s_in_ref, ht_in_ref
    _peer_scores_kernel(x_ref, g_ref, wq_ref, keys_ref, s_ref, ht_ref)


def _peer_scores_call(x, g, wq, keys, n_total, col0, into=None):
    n, d = x.shape
    tm = _row_tile(n)
    nhc = keys.shape[0]
    assert col0 % tm == 0
    off = col0 // tm
    in_specs = [pl.BlockSpec((tm, d), lambda i: (i, 0)), pl.BlockSpec((1, d), lambda i: (0, 0)),
                pl.BlockSpec(wq.shape, lambda i: (0, 0)), pl.BlockSpec(keys.shape, lambda i: (0, 0, 0))]
    args = [x, g.reshape(1, d), wq, keys]
    body, aliases = _peer_scores_kernel, {}
    if into is not None:
        in_specs += [pl.BlockSpec(memory_space=pl.ANY)] * 2
        args += list(into)
        body, aliases = _peer_scores_into_kernel, {4: 0, 5: 1}
    return pl.pallas_call(
        body, grid=(pl.cdiv(n, tm),), in_specs=in_specs,
        out_specs=[pl.BlockSpec((nhc, PEER_NKEYS, tm), lambda i: (0, 0, i + off)),
                   pl.BlockSpec((d, tm), lambda i: (0, i + off))],
        out_shape=[jax.ShapeDtypeStruct((nhc, PEER_NKEYS, n_total), F32),
                   jax.ShapeDtypeStruct((d, n_total), BF16)],
        input_output_aliases=aliases,
        compiler_params=_cparams(("parallel",)), name="peer_scores")(*args)


def _top16(s, iota, exact_ties, want_rank=True):
    keys = s
    rank = jnp.full(s.shape, float(PEER_TOPK), F32) if want_rank else None
    vals = []
    for a in range(PEER_TOPK):
        m = jnp.max(s, axis=0, keepdims=True)
        hit = s == m
        if exact_ties:
            idx = jnp.min(jnp.where(hit, iota, float(PEER_NKEYS)), axis=0, keepdims=True)
            hit = iota == idx
        if want_rank:
            rank = jnp.where(hit, float(a), rank)
        s = jnp.where(hit, NEG_INF, s)
        vals.append(m)
    taken = (rank < float(PEER_TOPK)) if want_rank else (keys >= vals[-1])
    return rank, vals, jnp.sum(jnp.where(taken, 1.0, 0.0), axis=0, keepdims=True)


def _dup_mask_bits(x):
    bits = lax.bitcast_convert_type(x.astype(MASK_DTYPE).astype(F32), jnp.uint32)
    return bits | (bits >> 16)


def _peer_select_kernel(s_ref, l_ref, p1_ref, r2_ref, p2_ref):
    miscount = _peer_select_body(s_ref, l_ref, p1_ref, r2_ref, p2_ref, exact_ties=False)

    @pl.when(jnp.max(miscount) > 0.0)
    def _():
        _peer_select_body(s_ref, l_ref, p1_ref, r2_ref, p2_ref, exact_ties=True)


def _peer_select_body(s_ref, l_ref, p1_ref, r2_ref, p2_ref, exact_ties):
    lanes = s_ref.shape[-1]
    iota = lax.broadcasted_iota(jnp.int32, (PEER_NKEYS, lanes), 0).astype(F32)
    iota16 = lax.broadcasted_iota(jnp.int32, (PEER_TOPK, lanes), 0).astype(F32)
    miscount = jnp.zeros((1, lanes), F32)
    for h in range(PEER_HEADS):
        s1 = s_ref[2 * h]
        s2 = s_ref[2 * h + 1]
        rank1, v1, n1 = _top16(s1, iota, exact_ties, want_rank=exact_ties)
        rank2, v2, n2 = _top16(s2, iota, exact_ties)
        miscount = miscount + jnp.abs(n1 - float(PEER_TOPK)) + jnp.abs(n2 - float(PEER_TOPK))
        v1a = jnp.concatenate(v1, axis=0)
        v2a = jnp.concatenate(v2, axis=0)
        top = v1[0] + v2[0]
        taken = jnp.zeros((PEER_TOPK, lanes), F32)
        front = v1a + v2[0]
        zsum = jnp.zeros((1, lanes), F32)
        for _ in range(PEER_TOPK):
            m = jnp.max(front, axis=0, keepdims=True)
            a_star = jnp.min(jnp.where(front == m, iota16, float(PEER_TOPK)), axis=0, keepdims=True)
            hit = iota16 == a_star
            zsum = zsum + jnp.exp(m - top)
            taken = jnp.where(hit, taken + 1.0, taken)
            cnt = jnp.max(jnp.where(hit, taken, -1.0), axis=0, keepdims=True)
            nxt = jnp.max(jnp.where(iota16 == cnt, v2a, NEG_INF), axis=0, keepdims=True)
            front = jnp.where(hit, v1a + nxt, front)
        lim = jnp.zeros((PEER_NKEYS, lanes), F32)
        for a in range(PEER_TOPK):
            is_a = (rank1 == float(a)) if exact_ties else (s1 == v1[a])
            lim = jnp.where(is_a, taken[a:a + 1, :], lim)
        l_ref[h] = _dup_mask_bits(lim)
        p1_ref[h] = _dup_mask_bits(jnp.exp(s1 - v1[0]))
        p2 = jnp.exp(s2 - v2[0]) / zsum
        for r in range(PEER_NKEYS // MASK_ROWS):
            rows = slice(r * MASK_ROWS, (r + 1) * MASK_ROWS)
            r2_ref[h, r] = rank2[rows].astype(MASK_DTYPE)
            p2_ref[h, r] = p2[rows].astype(MASK_DTYPE)
    return miscount


def _peer_select_call(scores):
    nhc, nk, n = scores.shape
    tl = LANES
    groups = nk // MASK_ROWS
    words = pl.BlockSpec((PEER_HEADS, nk, tl), lambda i: (0, 0, i))
    packed = pl.BlockSpec((PEER_HEADS, groups, MASK_ROWS, tl), lambda i: (0, 0, 0, i))
    return pl.pallas_call(
        _peer_select_kernel, grid=(pl.cdiv(n, tl),),
        in_specs=[pl.BlockSpec((nhc, nk, tl), lambda i: (0, 0, i))],
        out_specs=[words, words, packed, packed],
        out_shape=[jax.ShapeDtypeStruct((PEER_HEADS, nk, n), jnp.uint32)] * 2
        + [jax.ShapeDtypeStruct((PEER_HEADS, groups, MASK_ROWS, n), MASK_DTYPE)] * 2,
        compiler_params=_cparams(("parallel",)), name="peer_select")(scores)


def _gate_weights(cb, ii, heads, w, l_ref, p1_ref, r2_ref, p2_ref):
    blocks = PEER_CHUNK // PEER_NKEYS
    base = pl.multiple_of(cb * blocks + (ii // 8) * 8, 8)
    sub = ii % 8
    w = list(w)
    for h in heads:
        lw = l_ref[h, pl.ds(base, 8), :]
        pw = p1_ref[h, pl.ds(base, 8), :]
        lim = pltpu.bitcast(jnp.broadcast_to(lw[sub:sub + 1, :], lw.shape), MASK_DTYPE)
        p1 = pltpu.bitcast(jnp.broadcast_to(pw[sub:sub + 1, :], pw.shape), MASK_DTYPE)
        for r in range(len(w)):
            term = jnp.where(r2_ref[h, r] < lim, p2_ref[h, r], jnp.zeros_like(p1)) * p1
            w[r] = term if w[r] is None else w[r] + term
    return w


def _gate_store(ii, w, z_ref, g_ref):
    for r in range(len(w)):
        rows = slice(ii * PEER_NKEYS + r * MASK_ROWS, ii * PEER_NKEYS + (r + 1) * MASK_ROWS)
        z = z_ref[rows, :]
        act = 0.5 * z * (1.0 + lax.erf(z * np.float32(np.sqrt(0.5))))
        g_ref[rows, :] = (act.astype(MASK_DTYPE) * w[r]).astype(g_ref.dtype)


def _peer_dense_kernel(ht_ref, u_ref, v_ref, l_ref, p1_ref, r2_ref, p2_ref, o_ref,
                       z0_ref, z1_ref, g0_ref, g1_ref, acc_ref, *, items, chunks):
    s = pl.program_id(0)

    @pl.when(s == 0)
    def _():
        for ref in (z0_ref, z1_ref, g0_ref, g1_ref):
            ref[...] = jnp.zeros_like(ref)

    gate_chunk = jnp.clip(s - 1, 0, items - 1) % chunks

    def stages(z_new, z_old, g_new, g_old):
        blocks = PEER_CHUNK // PEER_NKEYS
        sel = (l_ref, p1_ref, r2_ref, p2_ref)
        half = PEER_HEADS // 2
        per_v = blocks // (o_ref.shape[0] // PEER_NKEYS)
        for ii in range(blocks):
            er = slice(ii * PEER_NKEYS, (ii + 1) * PEER_NKEYS)
            w = _gate_weights(gate_chunk, ii, range(half), [None] * (PEER_NKEYS // MASK_ROWS), *sel)
            z_new[er, :] = jnp.dot(u_ref[er, :], ht_ref[...], preferred_element_type=F32)
            w = _gate_weights(gate_chunk, ii, range(half, PEER_HEADS), w, *sel)
            _gate_store(ii, w, z_old, g_new)
            if ii % per_v == per_v - 1:
                dr = slice((ii // per_v) * PEER_NKEYS, (ii // per_v + 1) * PEER_NKEYS)
                acc_ref[dr, :] = lax.dot_general(v_ref[:, dr], g_old[...], (((0,), (0,)), ((), ())),
                                                 preferred_element_type=F32)

    @pl.when(s % 2 == 0)
    def _():
        stages(z0_ref, z1_ref, g1_ref, g0_ref)

    @pl.when(s % 2 == 1)
    def _():
        stages(z1_ref, z0_ref, g0_ref, g1_ref)

    first = jnp.clip(s - 2, 0, items - 1) % chunks == 0

    @pl.when(first)
    def _():
        o_ref[...] = acc_ref[...]

    @pl.when(jnp.logical_not(first))
    def _():
        o_ref[...] += acc_ref[...]


def _peer_dense_call(ht, u, v, lim, p1, r2, p2):
    d, n = ht.shape
    chunks = u.shape[0] // PEER_CHUNK
    tt = min(PEER_TOKENS, n)
    items = pl.cdiv(n, tt) * chunks

    def item(s, lag):
        return jnp.clip(s - lag, 0, items - 1)

    words = pl.BlockSpec((PEER_HEADS, PEER_NKEYS, tt), lambda s: (0, 0, item(s, 1) // chunks))
    packed = pl.BlockSpec(r2.shape[:3] + (tt,), lambda s: (0, 0, 0, item(s, 1) // chunks))
    return pl.pallas_call(
        functools.partial(_peer_dense_kernel, items=items, chunks=chunks),
        grid=(items + 2,),
        in_specs=[pl.BlockSpec((d, tt), lambda s: (0, item(s, 0) // chunks)),
                  pl.BlockSpec((PEER_CHUNK, d), lambda s: (item(s, 0) % chunks, 0)),
                  pl.BlockSpec((PEER_CHUNK, d), lambda s: (item(s, 2) % chunks, 0)),
                  words, words, packed, packed],
        out_specs=pl.BlockSpec((d, tt), lambda s: (0, item(s, 2) // chunks)),
        out_shape=jax.ShapeDtypeStruct((d, n), F32),
        scratch_shapes=[pltpu.VMEM((PEER_CHUNK, tt), F32)] * 2 + [pltpu.VMEM((PEER_CHUNK, tt), BF16)] * 2
        + [pltpu.VMEM((d, tt), F32)],
        compiler_params=_cparams(("arbitrary",)), name="peer_dense")(ht, u, v, lim, p1, r2, p2)


def _peer_params(g, wq, keys, u, v):
    nh, _, nk, half = keys.shape
    return g, wq.astype(BF16), keys.reshape(nh * 2, nk, half).astype(BF16), u.astype(BF16), v.astype(BF16)


def _peer(x_a, x_b, params):
    g, wq, keys, u, v = params
    n_a, n_b = x_a.shape[0], x_b.shape[0]
    first = _peer_scores_call(x_a, g, wq, keys, n_a + n_b, 0)
    scores, ht = _peer_scores_call(x_b, g, wq, keys, n_a + n_b, n_a, into=first)
    lim, p1, r2, p2 = _peer_select_call(scores)
    return _peer_dense_call(ht, u, v, lim, p1, r2, p2)


def _pool_prompt_kernel(xh_ref, fth_ref, x_ref, ft_ref, gn_ref, w_ref, sc_ref, o_ref, *, batch):
    tm = x_ref.shape[0]
    hr = xh_ref.shape[0]
    x = x_ref[...] + ft_ref[...].T
    cur = _rms(x, gn_ref[...])
    halo = jnp.where(pl.program_id(0) == 0, 0.0,
                     _rms(xh_ref[...] + fth_ref[...].T, gn_ref[...]))
    ext = jnp.concatenate([halo, cur], axis=0)
    row = (pl.program_id(0) * tm + lax.broadcasted_iota(jnp.int32, (tm, 1), 0)).astype(F32)
    pos = jnp.floor(row / float(batch))
    gd = cur.shape[-1] // len(POOL_WINDOWS)
    outs = []
    for gi, w in enumerate(POOL_WINDOWS):
        acc = ext[:, gi * gd:(gi + 1) * gd]
        dropped = 0
        shift = 1
        while shift < w:
            k = shift * batch
            acc = acc[k:] + acc[:-k]
            dropped += k
            shift *= 2
        cnt = jnp.minimum(float(w), pos + 1.0)
        mixed = acc[hr - dropped:] / cnt - cur[:, gi * gd:(gi + 1) * gd]
        outs.append(_bdot(mixed, w_ref[gi]))
    o_ref[...] = x + jnp.concatenate(outs, axis=-1) * sc_ref[...]


def _pool_prompt_call(x, ft, gn, pool_w, pool_scale, batch):
    n, d = x.shape
    tm = _row_tile(n)
    hr = POOL_HALO * batch
    assert tm % hr == 0 and hr % LANES == 0

    def before(i):
        return jnp.maximum(i * (tm // hr) - 1, 0)

    vec = pl.BlockSpec((1, d), lambda i: (0, 0))
    return pl.pallas_call(
        functools.partial(_pool_prompt_kernel, batch=batch), grid=(pl.cdiv(n, tm),),
        in_specs=[pl.BlockSpec((hr, d), lambda i: (before(i), 0)), pl.BlockSpec((d, hr), lambda i: (0, before(i))),
                  pl.BlockSpec((tm, d), lambda i: (i, 0)), pl.BlockSpec((d, tm), lambda i: (0, i)), vec,
                  pl.BlockSpec(pool_w.shape, lambda i: (0, 0, 0)), vec],
        out_specs=pl.BlockSpec((tm, d), lambda i: (i, 0)), out_shape=jax.ShapeDtypeStruct((n, d), F32),
        compiler_params=_cparams(("parallel",)), name="pool_prompt")(
            x, ft, x, ft, gn.reshape(1, d), pool_w, pool_scale.reshape(1, d))


def _pool_sample_kernel(hist_ref, h_ref, x_ref, w_ref, sc_ref, o_ref):
    h = h_ref[...]
    nhist = hist_ref.shape[0]
    gd = h.shape[-1] // len(POOL_WINDOWS)
    outs = []
    for gi, w in enumerate(POOL_WINDOWS):
        sl = slice(gi * gd, (gi + 1) * gd)
        acc = h[:, sl]
        for back in range(1, w):
            acc = acc + hist_ref[nhist - back][:, sl]
        mixed = acc / float(w) - h[:, sl]
        outs.append(_bdot(mixed, w_ref[gi]))
    o_ref[...] = x_ref[...] + jnp.concatenate(outs, axis=-1) * sc_ref[...]


def _pool_sample_call(hist, h, x, pool_w, pool_scale):
    b, d = h.shape
    full2 = pl.BlockSpec((b, d), lambda i: (0, 0))
    return pl.pallas_call(
        _pool_sample_kernel, grid=(1,),
        in_specs=[pl.BlockSpec(hist.shape, lambda i: (0, 0, 0)), full2, full2,
                  pl.BlockSpec(pool_w.shape, lambda i: (0, 0, 0)),
                  pl.BlockSpec((1, d), lambda i: (0, 0))],
        out_specs=full2, out_shape=jax.ShapeDtypeStruct((b, d), F32),
        compiler_params=_cparams(("arbitrary",)), name="pool_sample")(
            hist, h, x, pool_w, pool_scale.reshape(1, d))


def _time_block(t, pref):
    for c in range(min(pref, t), 0, -1):
        if t % c == 0 and (c % 8 == 0 or c == t):
            return c
    return t


def kernel(x_prompt, x_sample, state_wkv, state_shift, state_pool, meta_tokens, norm_mix, norm_ffn, norm_final, rwkv_mix, rwkv_w_rkv, rwkv_decay_w1, rwkv_decay_w2, rwkv_decay_b, rwkv_iclr_w1, rwkv_iclr_w2, rwkv_iclr_b, rwkv_gate_w1, rwkv_gate_w2, rwkv_k_k, rwkv_k_a, rwkv_r_k, rwkv_ln_g, rwkv_ln_b, rwkv_w_o, pool_w, pool_scale, peer_wq, peer_keys, peer_u, peer_v):
    bp, seq, d = x_prompt.shape
    bs = x_sample.shape[0]
    tp = N_META + seq
    n_p = bp * tp
    heads = d // HEAD_DIM
    assert x_sample.shape[1] == 1 and bp == 8 and bs % 8 == 0 and heads * 8 == LANES
    assert rwkv_w_rkv.shape[0] == 1 and pool_w.shape[0] == 1 and norm_mix.shape[0] == 2

    blockdiag = jnp.asarray(np.kron(np.eye(LANES // HEAD_DIM), np.ones((HEAD_DIM, HEAD_DIM))), BF16)
    proj_params = _rwkv_proj_params(
        norm_mix[0], rwkv_mix[0], rwkv_w_rkv[0], rwkv_decay_w1[0], rwkv_decay_w2[0], rwkv_decay_b[0], rwkv_iclr_w1[0],
        rwkv_iclr_w2[0], rwkv_iclr_b[0], rwkv_gate_w1[0], rwkv_gate_w2[0], rwkv_k_k[0], rwkv_k_a[0],
        rwkv_r_k[0], blockdiag)
    w_o = rwkv_w_o[0].astype(BF16)
    peer_params = [_peer_params(norm_ffn[i], peer_wq[i], peer_keys[i], peer_u[i], peer_v[i]) for i in range(2)]
    pool_wb = pool_w[0].astype(BF16)
    meta = jnp.broadcast_to(meta_tokens[:, None], (N_META, bp, d))
    x0p = jnp.concatenate([meta, x_prompt.transpose(1, 0, 2)], axis=0).reshape(n_p, d)
    x0s = x_sample.reshape(bs, d)

    def rwkv_layer(x0, prev, s0, time_groups):
        *scan_inputs, g, bonus = _rwkv_proj_call(x0, prev, proj_params, time_groups)
        if time_groups:
            y, s_out = _wkv_group(scan_inputs, s0, _time_block(tp, 48))
        else:
            y, s_out = _wkv_step_call(scan_inputs, s0)
        x1 = _rwkv_out_call(y, bonus, g, x0, rwkv_ln_g[0], rwkv_ln_b[0], w_o, blockdiag, time_groups)
        return s_out, x1

    wkv_p, x1p = rwkv_layer(x0p, x0p, jnp.zeros((bp, heads, HEAD_DIM, HEAD_DIM), F32), True)
    wkv_s, x1s = rwkv_layer(x0s, state_shift[0], state_wkv[0], False)
    f0 = _peer(x1p, x1s, peer_params[0])

    x2s, h1s = _add_norm_call(x1s, f0, norm_mix[1], col0=n_p)
    x3p = _pool_prompt_call(x1p, f0, norm_mix[1], pool_wb, pool_scale[0], bp)
    x3s = _pool_sample_call(state_pool[0].transpose(1, 0, 2), h1s, x2s, pool_wb, pool_scale[0])
    f1 = _peer(x3p, x3s, peer_params[1])
    y_prompt = _final_norm_prompt_call(x3p, f1, norm_final, bp, N_META)
    _, yfs = _add_norm_call(x3s, f1, norm_final, col0=n_p)

    nbuf = state_pool.shape[2]
    y_sample = yfs.reshape(bs, 1, d)
    shift_p = _norm_call(x0p[n_p - bp:], norm_mix[0])[None]
    shift_s = _norm_call(x0s, norm_mix[0])[None]
    keep = min(tp, nbuf) * bp
    last = -(-keep // LANES) * LANES
    tail = _add_norm_call(x1p[n_p - last:], f0[:, n_p - last:n_p], norm_mix[1])[1][last - keep:]
    tail = jnp.concatenate([jnp.zeros((nbuf * bp - keep, d), F32), tail], axis=0)
    pool_p = tail.reshape(nbuf, bp, d).transpose(1, 0, 2)[None]
    pool_s = jnp.concatenate([state_pool[0][:, 1:], h1s[:, None, :]], axis=1)[None]
    return (y_prompt, y_sample, wkv_p[None], shift_p, pool_p, wkv_s[None], shift_s, pool_s)
```

```python
import functools

import jax
import jax.numpy as jnp
import numpy as np
from jax import lax
from jax.experimental import pallas as pl
from jax.experimental.pallas import tpu as pltpu

F32 = jnp.float32
BF16 = jnp.bfloat16

N_META = 16
RMS_EPS = 1e-6
GN_EPS = 64e-5
HEAD_DIM = 64
LANES = 128
POOL_WINDOWS = (2, 4, 8, 16)
POOL_HALO = 16
PEER_HEADS = 8
PEER_NKEYS = 128
PEER_TOPK = 16
PEER_CHUNK = 2048
PEER_TOKENS = 512
MASK_DTYPE = jnp.bfloat16
MASK_ROWS = 16
ROW_TILE = 256
VMEM_LIMIT = 56 * 1024 * 1024

NEG_INF = float("-inf")


def _cparams(sem):
    return pltpu.CompilerParams(dimension_semantics=sem, vmem_limit_bytes=VMEM_LIMIT)


def _row_tile(n):
    return min(ROW_TILE, n)


def _rms(x, g):
    return x * lax.rsqrt(jnp.mean(x * x, axis=-1, keepdims=True) + RMS_EPS) * g


def _bdot(a, b):
    return jnp.dot(a.astype(BF16), b.astype(BF16), preferred_element_type=F32)


def _head_sum(x, blockdiag):
    outs = []
    for blk in range(x.shape[-1] // LANES):
        xb = x[:, blk * LANES:(blk + 1) * LANES]
        hi = xb.astype(BF16)
        lo = (xb - hi.astype(F32)).astype(BF16)
        outs.append(jnp.dot(hi, blockdiag, preferred_element_type=F32)
                    + jnp.dot(lo, blockdiag, preferred_element_type=F32))
    return jnp.concatenate(outs, axis=-1)


def _norm_kernel(x_ref, g_ref, h_ref):
    h_ref[...] = _rms(x_ref[...], g_ref[...])


def _norm_call(x, g):
    n, d = x.shape
    tm = _row_tile(n)
    row = pl.BlockSpec((tm, d), lambda i: (i, 0))
    return pl.pallas_call(
        _norm_kernel, grid=(pl.cdiv(n, tm),),
        in_specs=[row, pl.BlockSpec((1, d), lambda i: (0, 0))],
        out_specs=row, out_shape=jax.ShapeDtypeStruct((n, d), F32),
        compiler_params=_cparams(("parallel",)), name="rmsnorm")(x, g.reshape(1, d))


def _add_norm_kernel(a_ref, bt_ref, g_ref, x_ref, h_ref):
    x = a_ref[...] + bt_ref[...].T
    x_ref[...] = x
    h_ref[...] = _rms(x, g_ref[...])


def _add_norm_call(a, bt, g, col0=0):
    n, d = a.shape
    tm = _row_tile(n)
    assert col0 % tm == 0
    off = col0 // tm
    row = pl.BlockSpec((tm, d), lambda i: (i, 0))
    return pl.pallas_call(
        _add_norm_kernel, grid=(pl.cdiv(n, tm),),
        in_specs=[row, pl.BlockSpec((d, tm), lambda i: (0, i + off)), pl.BlockSpec((1, d), lambda i: (0, 0))],
        out_specs=[row, row],
        out_shape=[jax.ShapeDtypeStruct((n, d), F32)] * 2,
        compiler_params=_cparams(("parallel",)), name="add_rmsnorm")(a, bt, g.reshape(1, d))


def _final_norm_prompt_kernel(*refs, batch, parts):
    a_refs, bt_refs = refs[:parts], refs[parts:2 * parts]
    g_ref, o_ref, h_ref = refs[2 * parts:]
    tm = a_refs[0].shape[0]
    for p in range(parts):
        h = _rms(a_refs[p][...] + bt_refs[p][...].T, g_ref[...])
        for c in range(h_ref.shape[0]):
            h_ref[c, p * tm:(p + 1) * tm, :] = h[:, c * LANES:(c + 1) * LANES]
    steps = h_ref.shape[1] // batch
    for b in range(batch):
        for c in range(h_ref.shape[0]):
            o_ref[b, :, c * LANES:(c + 1) * LANES] = h_ref[c, pl.ds(b, steps, stride=batch), :]


def _final_norm_prompt_call(a, bt, g, batch, meta_steps):
    n, d = a.shape
    tm = meta_steps * batch
    assert n % tm == 0 and tm % LANES == 0
    blocks = n // tm - 1
    parts = 2 if blocks % 2 == 0 else 1
    rows = [pl.BlockSpec((tm, d), lambda i, p=p: (parts * i + p + 1, 0)) for p in range(parts)]
    cols = [pl.BlockSpec((d, tm), lambda i, p=p: (0, parts * i + p + 1)) for p in range(parts)]
    return pl.pallas_call(
        functools.partial(_final_norm_prompt_kernel, batch=batch, parts=parts), grid=(blocks // parts,),
        in_specs=rows + cols + [pl.BlockSpec((1, d), lambda i: (0, 0))],
        out_specs=pl.BlockSpec((batch, parts * meta_steps, d), lambda i: (0, i, 0)),
        out_shape=jax.ShapeDtypeStruct((batch, n // batch - meta_steps, d), F32),
        scratch_shapes=[pltpu.VMEM((d // LANES, parts * tm, LANES), F32)],
        compiler_params=_cparams(("parallel",)), name="final_norm_prompt")(
            *([a] * parts), *([bt] * parts), g.reshape(1, d))


def _rows_to_scan(x8):
    pairs = x8.shape[-1] // LANES
    a = jnp.concatenate([x8[:, p * LANES:(p + 1) * LANES] for p in range(pairs)], axis=0)
    at = a.T
    return jnp.concatenate([at[:HEAD_DIM], at[HEAD_DIM:]], axis=1)


def _scan_to_rows(tile):
    at = jnp.concatenate([tile[:, :HEAD_DIM], tile[:, HEAD_DIM:]], axis=0)
    a = at.T
    return jnp.concatenate([a[p * 8:(p + 1) * 8, :] for p in range(a.shape[0] // 8)], axis=1)


def _store_scan(ref, x, time_groups):
    if not time_groups:
        ref[...] = x
        return
    for j in range(x.shape[0] // 8):
        ref[j] = _rows_to_scan(x[j * 8:(j + 1) * 8, :])


def _load_scan(ref, time_groups):
    if not time_groups:
        return ref[...]
    return jnp.concatenate([_scan_to_rows(ref[j]) for j in range(ref.shape[0])], axis=0)


def _scan_spec(tm, d, time_groups):
    if time_groups:
        return pl.BlockSpec((tm // 8, HEAD_DIM, LANES), lambda i: (i, 0, 0))
    return pl.BlockSpec((tm, d), lambda i: (i, 0))


def _scan_shape(n, d, time_groups):
    return (n // 8, HEAD_DIM, LANES) if time_groups else (n, d)


def _state_to_scan(s):
    b, heads, nv, nk = s.shape
    s = s.reshape(b // 8, 8, heads // 2, 2, nv, nk).transpose(5, 4, 0, 3, 2, 1)
    return s.reshape(nk, nv, b * heads)


def _state_from_scan(s, b):
    nk, nv, lanes = s.shape
    heads = lanes // b
    return s.reshape(nk, nv, b // 8, 2, heads // 2, 8).transpose(2, 5, 4, 3, 1, 0).reshape(b, heads, nv, nk)


def _rwkv_proj_kernel(x_ref, prev_ref, gn_ref, mix_ref, wrkv_ref, dw1_ref, dw2_ref, db_ref,
                      aw1_ref, aw2_ref, ab_ref, gw1_ref, gw2_ref, kk_ref, ka_ref, rk_ref, bd_ref,
                      r_ref, dec_ref, k_ref, v_ref, aa_ref, bb_ref, g_ref, bonus_ref, *, time_groups):
    h = _rms(x_ref[...], gn_ref[...])
    if time_groups:
        before = jnp.where(pl.program_id(0) == 0, 0.0, _rms(prev_ref[...], gn_ref[...]))
        prev = jnp.concatenate([before, h[:h.shape[0] - 8]], axis=0)
    else:
        prev = prev_ref[...]
    xx = prev - h

    def xs(c):
        return h + xx * mix_ref[c:c + 1, :]

    r = _bdot(xs(0), wrkv_ref[0])
    k = _bdot(xs(1), wrkv_ref[1])
    v = _bdot(xs(2), wrkv_ref[2])
    w = -jax.nn.softplus(-(db_ref[...] + _bdot(jnp.tanh(_bdot(xs(3), dw1_ref[...])), dw2_ref[...]))) - 0.5
    a = jax.nn.sigmoid(ab_ref[...] + _bdot(_bdot(xs(4), aw1_ref[...]), aw2_ref[...]))
    g = _bdot(jax.nn.sigmoid(_bdot(xs(5), gw1_ref[...])), gw2_ref[...])
    kk = k * kk_ref[...]
    nrm = jnp.sqrt(_head_sum(kk * kk, bd_ref[...]))
    kk = kk / jnp.maximum(nrm, 1e-12)
    kf = k * (1.0 + (a - 1.0) * ka_ref[...])
    _store_scan(r_ref, r, time_groups)
    _store_scan(dec_ref, jnp.exp(-jnp.exp(w)), time_groups)
    _store_scan(k_ref, kf, time_groups)
    _store_scan(v_ref, v, time_groups)
    _store_scan(aa_ref, -kk, time_groups)
    _store_scan(bb_ref, kk * a, time_groups)
    g_ref[...] = g
    bonus_ref[...] = _head_sum(r * kf * rk_ref[...], bd_ref[...]) * v


def _rwkv_proj_params(gn, mix, w_rkv, dw1, dw2, db, aw1, aw2, ab, gw1, gw2, k_k, k_a, r_k, blockdiag):
    d = mix.shape[-1]
    return [gn.reshape(1, d), mix, w_rkv.astype(BF16), dw1.astype(BF16), dw2.astype(BF16), db.reshape(1, d),
            aw1.astype(BF16), aw2.astype(BF16), ab.reshape(1, d), gw1.astype(BF16),
            gw2.astype(BF16), k_k.reshape(1, d), k_a.reshape(1, d), r_k.reshape(1, d), blockdiag]


def _rwkv_proj_call(x, prev, params, time_groups):
    n, d = x.shape
    tm = _row_tile(n)
    assert d == 8 * LANES and tm % 8 == 0
    row = pl.BlockSpec((tm, d), lambda i: (i, 0))
    scan = _scan_spec(tm, d, time_groups)
    prev_spec = pl.BlockSpec((8, d), lambda i: (jnp.maximum(i * (tm // 8) - 1, 0), 0)) if time_groups else row

    def full(x):
        nd = x.ndim
        return pl.BlockSpec(x.shape, lambda i: (0,) * nd)

    return pl.pallas_call(
        functools.partial(_rwkv_proj_kernel, time_groups=time_groups), grid=(pl.cdiv(n, tm),),
        in_specs=[row, prev_spec] + [full(p) for p in params],
        out_specs=[scan] * 6 + [row] * 2,
        out_shape=[jax.ShapeDtypeStruct(_scan_shape(n, d, time_groups), F32)] * 6
        + [jax.ShapeDtypeStruct((n, d), F32)] * 2,
        compiler_params=_cparams(("parallel",)), name="rwkv_proj")(x, prev, *params)


def _wkv_kernel(r_ref, d_ref, k_ref, v_ref, a_ref, b_ref, anext_ref, s0_ref, y_ref, s_ref, sa_ref, *, steps):
    nk = s_ref.shape[0]

    @pl.when(pl.program_id(1) == 0)
    def _():
        s_ref[...] = s0_ref[...]
        sa0 = None
        for kk in range(nk):
            term = s0_ref[kk] * a_ref[0, kk:kk + 1, :]
            sa0 = term if sa0 is None else sa0 + term
        sa_ref[...] = sa0

    def step(t, sa, next_row):
        vv = v_ref[t]
        y = None
        sa_next = None
        for kk in range(nk):
            sk = (s_ref[kk] * d_ref[t, kk:kk + 1, :] + sa * b_ref[t, kk:kk + 1, :]
                  + vv * k_ref[t, kk:kk + 1, :])
            s_ref[kk] = sk
            ty = sk * r_ref[t, kk:kk + 1, :]
            ta = sk * next_row(kk)
            y = ty if y is None else y + ty
            sa_next = ta if sa_next is None else sa_next + ta
        y_ref[t] = y
        return sa_next

    sa = lax.fori_loop(0, steps - 1, lambda t, sa: step(t, sa, lambda kk: a_ref[t + 1, kk:kk + 1, :]),
                       sa_ref[...])
    sa_ref[...] = step(steps - 1, sa, lambda kk: anext_ref[0, kk:kk + 1, :])


def _wkv_call(r, dec, k, v, aa, bb, s0, steps):
    t_len, hd, lanes = r.shape
    seq = pl.BlockSpec((steps, hd, LANES), lambda l, t: (t, 0, l))
    nxt = pl.BlockSpec((1, hd, LANES), lambda l, t: (jnp.minimum((t + 1) * steps, t_len - 1), 0, l))
    st = pl.BlockSpec((hd, hd, LANES), lambda l, t: (0, 0, l))
    return pl.pallas_call(
        functools.partial(_wkv_kernel, steps=steps),
        grid=(lanes // LANES, t_len // steps),
        in_specs=[seq] * 6 + [nxt, st],
        out_specs=[seq, st],
        out_shape=[jax.ShapeDtypeStruct((t_len, hd, lanes), F32),
                   jax.ShapeDtypeStruct((hd, hd, lanes), F32)],
        scratch_shapes=[pltpu.VMEM((hd, LANES), F32)],
        compiler_params=_cparams(("parallel", "arbitrary")), name="wkv_scan")(r, dec, k, v, aa, bb, aa, s0)


def _wkv_group(scan_inputs, s0, steps):
    y, s_l = _wkv_call(*scan_inputs, _state_to_scan(s0), steps)
    return y, _state_from_scan(s_l, s0.shape[0])


def _wkv_step_kernel(r_ref, d_ref, k_ref, v_ref, a_ref, b_ref, s0_ref, y_ref, s_ref):
    nb, heads, nv, nk = s0_ref.shape
    eye = (lax.broadcasted_iota(jnp.int32, (nv, nk), 0)
           == lax.broadcasted_iota(jnp.int32, (nv, nk), 1)).astype(F32)
    group = 8
    for b in range(nb):
        for h0 in range(0, heads, group):
            hs = range(h0, min(h0 + group, heads))
            sa = [jnp.sum(s0_ref[b, h] * a_ref[b, h:h + 1, :], axis=-1, keepdims=True) for h in hs]
            v_col = [jnp.sum(eye * v_ref[b, h:h + 1, :], axis=-1, keepdims=True) for h in hs]
            y_col = []
            for i, h in enumerate(hs):
                s_new = (s0_ref[b, h] * d_ref[b, h:h + 1, :] + sa[i] * b_ref[b, h:h + 1, :]
                         + v_col[i] * k_ref[b, h:h + 1, :])
                s_ref[b, h] = s_new
                y_col.append(jnp.sum(s_new * r_ref[b, h:h + 1, :], axis=-1, keepdims=True))
            for i, h in enumerate(hs):
                y_ref[b, h:h + 1, :] = jnp.sum(eye * y_col[i], axis=0, keepdims=True)


def _wkv_step_call(rows, s0):
    b, heads, nv, nk = s0.shape
    assert nv == nk
    nb = 8
    vec = pl.BlockSpec((nb, heads, nk), lambda i: (i, 0, 0))
    st = pl.BlockSpec((nb, heads, nv, nk), lambda i: (i, 0, 0, 0))
    y, s_out = pl.pallas_call(
        _wkv_step_kernel, grid=(b // nb,),
        in_specs=[vec] * 6 + [st], out_specs=[vec, st],
        out_shape=[jax.ShapeDtypeStruct((b, heads, nv), F32), jax.ShapeDtypeStruct(s0.shape, F32)],
        compiler_params=_cparams(("parallel",)), name="wkv_step")(
            *[x.reshape(b, heads, nk) for x in rows], s0)
    return y.reshape(b, heads * nv), s_out


def _rwkv_out_kernel(y_ref, bonus_ref, g_ref, x_ref, lng_ref, lnb_ref, wo_ref, bd_ref, o_ref, *, time_groups):
    bd = bd_ref[...]
    y = _load_scan(y_ref, time_groups)
    inv_n = 1.0 / HEAD_DIM
    mu = _head_sum(y, bd) * inv_n
    yc = y - mu
    var = _head_sum(yc * yc, bd) * inv_n
    yn = yc * lax.rsqrt(var + GN_EPS) * lng_ref[...] + lnb_ref[...]
    o_ref[...] = x_ref[...] + _bdot((yn + bonus_ref[...]) * g_ref[...], wo_ref[...])


def _rwkv_out_call(y, bonus, g, x, ln_g, ln_b, w_o, blockdiag, time_groups):
    n, d = x.shape
    tm = _row_tile(n)
    row = pl.BlockSpec((tm, d), lambda i: (i, 0))
    vec = pl.BlockSpec((1, d), lambda i: (0, 0))
    return pl.pallas_call(
        functools.partial(_rwkv_out_kernel, time_groups=time_groups), grid=(pl.cdiv(n, tm),),
        in_specs=[_scan_spec(tm, d, time_groups), row, row, row, vec, vec,
                  pl.BlockSpec((d, d), lambda i: (0, 0)), pl.BlockSpec((LANES, LANES), lambda i: (0, 0))],
        out_specs=row, out_shape=jax.ShapeDtypeStruct((n, d), F32),
        compiler_params=_cparams(("parallel",)), name="rwkv_out")(
            y, bonus, g, x, ln_g.reshape(1, d), ln_b.reshape(1, d), w_o, blockdiag)


def _peer_scores_kernel(x_ref, g_ref, wq_ref, keys_ref, s_ref, ht_ref):
    h = _rms(x_ref[...], g_ref[...])
    ht_ref[...] = h.T.astype(BF16)
    q = _bdot(h, wq_ref[...]).astype(BF16)
    for hc in range(keys_ref.shape[0]):
        s_ref[hc] = lax.dot_general(keys_ref[hc], q[:, hc * LANES:(hc + 1) * LANES],
                                    (((1,), (1,)), ((), ())), preferred_element_type=F32)


def _peer_scores_into_kernel(x_ref, g_ref, wq_ref, keys_ref, s_in_ref, ht_in_ref, s_ref, ht_ref):
    del s_in_ref, ht_in_ref
    _peer_scores_kernel(x_ref, g_ref, wq_ref, keys_ref, s_ref, ht_ref)


def _peer_scores_call(x, g, wq, keys, n_total, col0, into=None):
    n, d = x.shape
    tm = _row_tile(n)
    nhc = keys.shape[0]
    assert col0 % tm == 0
    off = col0 // tm
    in_specs = [pl.BlockSpec((tm, d), lambda i: (i, 0)), pl.BlockSpec((1, d), lambda i: (0, 0)),
                pl.BlockSpec(wq.shape, lambda i: (0, 0)), pl.BlockSpec(keys.shape, lambda i: (0, 0, 0))]
    args = [x, g.reshape(1, d), wq, keys]
    body, aliases = _peer_scores_kernel, {}
    if into is not None:
        in_specs += [pl.BlockSpec(memory_space=pl.ANY)] * 2
        args += list(into)
        body, aliases = _peer_scores_into_kernel, {4: 0, 5: 1}
    return pl.pallas_call(
        body, grid=(pl.cdiv(n, tm),), in_specs=in_specs,
        out_specs=[pl.BlockSpec((nhc, PEER_NKEYS, tm), lambda i: (0, 0, i + off)),
                   pl.BlockSpec((d, tm), lambda i: (0, i + off))],
        out_shape=[jax.ShapeDtypeStruct((nhc, PEER_NKEYS, n_total), F32),
                   jax.ShapeDtypeStruct((d, n_total), BF16)],
        input_output_aliases=aliases,
        compiler_params=_cparams(("parallel",)), name="peer_scores")(*args)


def _top16(s, iota, exact_ties, want_rank=True):
    keys = s
    rank = jnp.full(s.shape, float(PEER_TOPK), F32) if want_rank else None
    vals = []
    for a in range(PEER_TOPK):
        m = jnp.max(s, axis=0, keepdims=True)
        hit = s == m
        if exact_ties:
            idx = jnp.min(jnp.where(hit, iota, float(PEER_NKEYS)), axis=0, keepdims=True)
            hit = iota == idx
        if want_rank:
            rank = jnp.where(hit, float(a), rank)
        s = jnp.where(hit, NEG_INF, s)
        vals.append(m)
    taken = (rank < float(PEER_TOPK)) if want_rank else (keys >= vals[-1])
    return rank, vals, jnp.sum(jnp.where(taken, 1.0, 0.0), axis=0, keepdims=True)


def _dup_mask_bits(x):
    bits = lax.bitcast_convert_type(x.astype(MASK_DTYPE).astype(F32), jnp.uint32)
    return bits | (bits >> 16)


def _peer_select_kernel(s_ref, l_ref, p1_ref, r2_ref, p2_ref):
    miscount = _peer_select_body(s_ref, l_ref, p1_ref, r2_ref, p2_ref, exact_ties=False)

    @pl.when(jnp.max(miscount) > 0.0)
    def _():
        _peer_select_body(s_ref, l_ref, p1_ref, r2_ref, p2_ref, exact_ties=True)


def _peer_select_body(s_ref, l_ref, p1_ref, r2_ref, p2_ref, exact_ties):
    lanes = s_ref.shape[-1]
    iota = lax.broadcasted_iota(jnp.int32, (PEER_NKEYS, lanes), 0).astype(F32)
    iota16 = lax.broadcasted_iota(jnp.int32, (PEER_TOPK, lanes), 0).astype(F32)
    miscount = jnp.zeros((1, lanes), F32)
    for h in range(PEER_HEADS):
        s1 = s_ref[2 * h]
        s2 = s_ref[2 * h + 1]
        rank1, v1, n1 = _top16(s1, iota, exact_ties, want_rank=exact_ties)
        rank2, v2, n2 = _top16(s2, iota, exact_ties)
        miscount = miscount + jnp.abs(n1 - float(PEER_TOPK)) + jnp.abs(n2 - float(PEER_TOPK))
        v1a = jnp.concatenate(v1, axis=0)
        v2a = jnp.concatenate(v2, axis=0)
        top = v1[0] + v2[0]
        taken = jnp.zeros((PEER_TOPK, lanes), F32)
        front = v1a + v2[0]
        zsum = jnp.zeros((1, lanes), F32)
        for _ in range(PEER_TOPK):
            m = jnp.max(front, axis=0, keepdims=True)
            a_star = jnp.min(jnp.where(front == m, iota16, float(PEER_TOPK)), axis=0, keepdims=True)
            hit = iota16 == a_star
            zsum = zsum + jnp.exp(m - top)
            taken = jnp.where(hit, taken + 1.0, taken)
            cnt = jnp.max(jnp.where(hit, taken, -1.0), axis=0, keepdims=True)
            nxt = jnp.max(jnp.where(iota16 == cnt, v2a, NEG_INF), axis=0, keepdims=True)
            front = jnp.where(hit, v1a + nxt, front)
        lim = jnp.zeros((PEER_NKEYS, lanes), F32)
        for a in range(PEER_TOPK):
            is_a = (rank1 == float(a)) if exact_ties else (s1 == v1[a])
            lim = jnp.where(is_a, taken[a:a + 1, :], lim)
        l_ref[h] = _dup_mask_bits(lim)
        p1_ref[h] = _dup_mask_bits(jnp.exp(s1 - v1[0]))
        p2 = jnp.exp(s2 - v2[0]) / zsum
        for r in range(PEER_NKEYS // MASK_ROWS):
            rows = slice(r * MASK_ROWS, (r + 1) * MASK_ROWS)
            r2_ref[h, r] = rank2[rows].astype(MASK_DTYPE)
            p2_ref[h, r] = p2[rows].astype(MASK_DTYPE)
    return miscount


def _peer_select_call(scores):
    nhc, nk, n = scores.shape
    tl = LANES
    groups = nk // MASK_ROWS
    words = pl.BlockSpec((PEER_HEADS, nk, tl), lambda i: (0, 0, i))
    packed = pl.BlockSpec((PEER_HEADS, groups, MASK_ROWS, tl), lambda i: (0, 0, 0, i))
    return pl.pallas_call(
        _peer_select_kernel, grid=(pl.cdiv(n, tl),),
        in_specs=[pl.BlockSpec((nhc, nk, tl), lambda i: (0, 0, i))],
        out_specs=[words, words, packed, packed],
        out_shape=[jax.ShapeDtypeStruct((PEER_HEADS, nk, n), jnp.uint32)] * 2
        + [jax.ShapeDtypeStruct((PEER_HEADS, groups, MASK_ROWS, n), MASK_DTYPE)] * 2,
        compiler_params=_cparams(("parallel",)), name="peer_select")(scores)


def _gate_weights(cb, ii, heads, w, l_ref, p1_ref, r2_ref, p2_ref):
    blocks = PEER_CHUNK // PEER_NKEYS
    base = pl.multiple_of(cb * blocks + (ii // 8) * 8, 8)
    sub = ii % 8
    w = list(w)
    for h in heads:
        lw = l_ref[h, pl.ds(base, 8), :]
        pw = p1_ref[h, pl.ds(base, 8), :]
        lim = pltpu.bitcast(jnp.broadcast_to(lw[sub:sub + 1, :], lw.shape), MASK_DTYPE)
        p1 = pltpu.bitcast(jnp.broadcast_to(pw[sub:sub + 1, :], pw.shape), MASK_DTYPE)
        for r in range(len(w)):
            term = jnp.where(r2_ref[h, r] < lim, p2_ref[h, r], jnp.zeros_like(p1)) * p1
            w[r] = term if w[r] is None else w[r] + term
    return w


def _gate_store(ii, w, z_ref, g_ref):
    for r in range(len(w)):
        rows = slice(ii * PEER_NKEYS + r * MASK_ROWS, ii * PEER_NKEYS + (r + 1) * MASK_ROWS)
        z = z_ref[rows, :]
        act = 0.5 * z * (1.0 + lax.erf(z * np.float32(np.sqrt(0.5))))
        g_ref[rows, :] = (act.astype(MASK_DTYPE) * w[r]).astype(g_ref.dtype)


def _peer_dense_kernel(ht_ref, u_ref, v_ref, l_ref, p1_ref, r2_ref, p2_ref, o_ref,
                       z0_ref, z1_ref, g0_ref, g1_ref, acc_ref, *, items, chunks):
    s = pl.program_id(0)

    @pl.when(s == 0)
    def _():
        for ref in (z0_ref, z1_ref, g0_ref, g1_ref):
            ref[...] = jnp.zeros_like(ref)

    gate_chunk = jnp.clip(s - 1, 0, items - 1) % chunks

    def stages(z_new, z_old, g_new, g_old):
        blocks = PEER_CHUNK // PEER_NKEYS
        sel = (l_ref, p1_ref, r2_ref, p2_ref)
        half = PEER_HEADS // 2
        per_v = blocks // (o_ref.shape[0] // PEER_NKEYS)
        for ii in range(blocks):
            er = slice(ii * PEER_NKEYS, (ii + 1) * PEER_NKEYS)
            w = _gate_weights(gate_chunk, ii, range(half), [None] * (PEER_NKEYS // MASK_ROWS), *sel)
            z_new[er, :] = jnp.dot(u_ref[er, :], ht_ref[...], preferred_element_type=F32)
            w = _gate_weights(gate_chunk, ii, range(half, PEER_HEADS), w, *sel)
            _gate_store(ii, w, z_old, g_new)
            if ii % per_v == per_v - 1:
                dr = slice((ii // per_v) * PEER_NKEYS, (ii // per_v + 1) * PEER_NKEYS)
                acc_ref[dr, :] = lax.dot_general(v_ref[:, dr], g_old[...], (((0,), (0,)), ((), ())),
                                                 preferred_element_type=F32)

    @pl.when(s % 2 == 0)
    def _():
        stages(z0_ref, z1_ref, g1_ref, g0_ref)

    @pl.when(s % 2 == 1)
    def _():
        stages(z1_ref, z0_ref, g0_ref, g1_ref)

    first = jnp.clip(s - 2, 0, items - 1) % chunks == 0

    @pl.when(first)
    def _():
        o_ref[...] = acc_ref[...]

    @pl.when(jnp.logical_not(first))
    def _():
        o_ref[...] += acc_ref[...]


def _peer_dense_call(ht, u, v, lim, p1, r2, p2):
    d, n = ht.shape
    chunks = u.shape[0] // PEER_CHUNK
    tt = min(PEER_TOKENS, n)
    items = pl.cdiv(n, tt) * chunks

    def item(s, lag):
        return jnp.clip(s - lag, 0, items - 1)

    words = pl.BlockSpec((PEER_HEADS, PEER_NKEYS, tt), lambda s: (0, 0, item(s, 1) // chunks))
    packed = pl.BlockSpec(r2.shape[:3] + (tt,), lambda s: (0, 0, 0, item(s, 1) // chunks))
    return pl.pallas_call(
        functools.partial(_peer_dense_kernel, items=items, chunks=chunks),
        grid=(items + 2,),
        in_specs=[pl.BlockSpec((d, tt), lambda s: (0, item(s, 0) // chunks)),
                  pl.BlockSpec((PEER_CHUNK, d), lambda s: (item(s, 0) % chunks, 0)),
                  pl.BlockSpec((PEER_CHUNK, d), lambda s: (item(s, 2) % chunks, 0)),
                  words, words, packed, packed],
        out_specs=pl.BlockSpec((d, tt), lambda s: (0, item(s, 2) // chunks)),
        out_shape=jax.ShapeDtypeStruct((d, n), F32),
        scratch_shapes=[pltpu.VMEM((PEER_CHUNK, tt), F32)] * 2 + [pltpu.VMEM((PEER_CHUNK, tt), BF16)] * 2
        + [pltpu.VMEM((d, tt), F32)],
        compiler_params=_cparams(("arbitrary",)), name="peer_dense")(ht, u, v, lim, p1, r2, p2)


def _peer_params(g, wq, keys, u, v):
    nh, _, nk, half = keys.shape
    return g, wq.astype(BF16), keys.reshape(nh * 2, nk, half).astype(BF16), u.astype(BF16), v.astype(BF16)


def _peer(x_a, x_b, params):
    g, wq, keys, u, v = params
    n_a, n_b = x_a.shape[0], x_b.shape[0]
    first = _peer_scores_call(x_a, g, wq, keys, n_a + n_b, 0)
    scores, ht = _peer_scores_call(x_b, g, wq, keys, n_a + n_b, n_a, into=first)
    lim, p1, r2, p2 = _peer_select_call(scores)
    return _peer_dense_call(ht, u, v, lim, p1, r2, p2)


def _pool_prompt_kernel(xh_ref, fth_ref, x_ref, ft_ref, gn_ref, w_ref, sc_ref, o_ref, *, batch):
    tm = x_ref.shape[0]
    hr = xh_ref.shape[0]
    x = x_ref[...] + ft_ref[...].T
    cur = _rms(x, gn_ref[...])
    halo = jnp.where(pl.program_id(0) == 0, 0.0,
                     _rms(xh_ref[...] + fth_ref[...].T, gn_ref[...]))
    ext = jnp.concatenate([halo, cur], axis=0)
    row = (pl.program_id(0) * tm + lax.broadcasted_iota(jnp.int32, (tm, 1), 0)).astype(F32)
    pos = jnp.floor(row / float(batch))
    gd = cur.shape[-1] // len(POOL_WINDOWS)
    outs = []
    for gi, w in enumerate(POOL_WINDOWS):
        acc = ext[:, gi * gd:(gi + 1) * gd]
        dropped = 0
        shift = 1
        while shift < w:
            k = shift * batch
            acc = acc[k:] + acc[:-k]
            dropped += k
            shift *= 2
        cnt = jnp.minimum(float(w), pos + 1.0)
        mixed = acc[hr - dropped:] / cnt - cur[:, gi * gd:(gi + 1) * gd]
        outs.append(_bdot(mixed, w_ref[gi]))
    o_ref[...] = x + jnp.concatenate(outs, axis=-1) * sc_ref[...]


def _pool_prompt_call(x, ft, gn, pool_w, pool_scale, batch):
    n, d = x.shape
    tm = _row_tile(n)
    hr = POOL_HALO * batch
    assert tm % hr == 0 and hr % LANES == 0

    def before(i):
        return jnp.maximum(i * (tm // hr) - 1, 0)

    vec = pl.BlockSpec((1, d), lambda i: (0, 0))
    return pl.pallas_call(
        functools.partial(_pool_prompt_kernel, batch=batch), grid=(pl.cdiv(n, tm),),
        in_specs=[pl.BlockSpec((hr, d), lambda i: (before(i), 0)), pl.BlockSpec((d, hr), lambda i: (0, before(i))),
                  pl.BlockSpec((tm, d), lambda i: (i, 0)), pl.BlockSpec((d, tm), lambda i: (0, i)), vec,
                  pl.BlockSpec(pool_w.shape, lambda i: (0, 0, 0)), vec],
        out_specs=pl.BlockSpec((tm, d), lambda i: (i, 0)), out_shape=jax.ShapeDtypeStruct((n, d), F32),
        compiler_params=_cparams(("parallel",)), name="pool_prompt")(
            x, ft, x, ft, gn.reshape(1, d), pool_w, pool_scale.reshape(1, d))


def _pool_sample_kernel(hist_ref, h_ref, x_ref, w_ref, sc_ref, o_ref):
    h = h_ref[...]
    nhist = hist_ref.shape[0]
    gd = h.shape[-1] // len(POOL_WINDOWS)
    outs = []
    for gi, w in enumerate(POOL_WINDOWS):
        sl = slice(gi * gd, (gi + 1) * gd)
        acc = h[:, sl]
        for back in range(1, w):
            acc = acc + hist_ref[nhist - back][:, sl]
        mixed = acc / float(w) - h[:, sl]
        outs.append(_bdot(mixed, w_ref[gi]))
    o_ref[...] = x_ref[...] + jnp.concatenate(outs, axis=-1) * sc_ref[...]


def _pool_sample_call(hist, h, x, pool_w, pool_scale):
    b, d = h.shape
    full2 = pl.BlockSpec((b, d), lambda i: (0, 0))
    return pl.pallas_call(
        _pool_sample_kernel, grid=(1,),
        in_specs=[pl.BlockSpec(hist.shape, lambda i: (0, 0, 0)), full2, full2,
                  pl.BlockSpec(pool_w.shape, lambda i: (0, 0, 0)),
                  pl.BlockSpec((1, d), lambda i: (0, 0))],
        out_specs=full2, out_shape=jax.ShapeDtypeStruct((b, d), F32),
        compiler_params=_cparams(("arbitrary",)), name="pool_sample")(
            hist, h, x, pool_w, pool_scale.reshape(1, d))


def _time_block(t, pref):
    for c in range(min(pref, t), 0, -1):
        if t % c == 0 and (c % 8 == 0 or c == t):
            return c
    return t


def kernel(x_prompt, x_sample, state_wkv, state_shift, state_pool, meta_tokens, norm_mix, norm_ffn, norm_final, rwkv_mix, rwkv_w_rkv, rwkv_decay_w1, rwkv_decay_w2, rwkv_decay_b, rwkv_iclr_w1, rwkv_iclr_w2, rwkv_iclr_b, rwkv_gate_w1, rwkv_gate_w2, rwkv_k_k, rwkv_k_a, rwkv_r_k, rwkv_ln_g, rwkv_ln_b, rwkv_w_o, pool_w, pool_scale, peer_wq, peer_keys, peer_u, peer_v):
    bp, seq, d = x_prompt.shape
    bs = x_sample.shape[0]
    tp = N_META + seq
    n_p = bp * tp
    heads = d // HEAD_DIM
    assert x_sample.shape[1] == 1 and bp == 8 and bs % 8 == 0 and heads * 8 == LANES
    assert rwkv_w_rkv.shape[0] == 1 and pool_w.shape[0] == 1 and norm_mix.shape[0] == 2

    blockdiag = jnp.asarray(np.kron(np.eye(LANES // HEAD_DIM), np.ones((HEAD_DIM, HEAD_DIM))), BF16)
    proj_params = _rwkv_proj_params(
        norm_mix[0], rwkv_mix[0], rwkv_w_rkv[0], rwkv_decay_w1[0], rwkv_decay_w2[0], rwkv_decay_b[0], rwkv_iclr_w1[0],
        rwkv_iclr_w2[0], rwkv_iclr_b[0], rwkv_gate_w1[0], rwkv_gate_w2[0], rwkv_k_k[0], rwkv_k_a[0],
        rwkv_r_k[0], blockdiag)
    w_o = rwkv_w_o[0].astype(BF16)
    peer_params = [_peer_params(norm_ffn[i], peer_wq[i], peer_keys[i], peer_u[i], peer_v[i]) for i in range(2)]
    pool_wb = pool_w[0].astype(BF16)
    meta = jnp.broadcast_to(meta_tokens[:, None], (N_META, bp, d))
    x0p = jnp.concatenate([meta, x_prompt.transpose(1, 0, 2)], axis=0).reshape(n_p, d)
    x0s = x_sample.reshape(bs, d)

    def rwkv_layer(x0, prev, s0, time_groups):
        *scan_inputs, g, bonus = _rwkv_proj_call(x0, prev, proj_params, time_groups)
        if time_groups:
            y, s_out = _wkv_group(scan_inputs, s0, _time_block(tp, 48))
        else:
            y, s_out = _wkv_step_call(scan_inputs, s0)
        x1 = _rwkv_out_call(y, bonus, g, x0, rwkv_ln_g[0], rwkv_ln_b[0], w_o, blockdiag, time_groups)
        return s_out, x1

    wkv_p, x1p = rwkv_layer(x0p, x0p, jnp.zeros((bp, heads, HEAD_DIM, HEAD_DIM), F32), True)
    wkv_s, x1s = rwkv_layer(x0s, state_shift[0], state_wkv[0], False)
    f0 = _peer(x1p, x1s, peer_params[0])

    x2s, h1s = _add_norm_call(x1s, f0, norm_mix[1], col0=n_p)
    x3p = _pool_prompt_call(x1p, f0, norm_mix[1], pool_wb, pool_scale[0], bp)
    x3s = _pool_sample_call(state_pool[0].transpose(1, 0, 2), h1s, x2s, pool_wb, pool_scale[0])
    f1 = _peer(x3p, x3s, peer_params[1])
    y_prompt = _final_norm_prompt_call(x3p, f1, norm_final, bp, N_META)
    _, yfs = _add_norm_call(x3s, f1, norm_final, col0=n_p)

    nbuf = state_pool.shape[2]
    y_sample = yfs.reshape(bs, 1, d)
    shift_p = _norm_call(x0p[n_p - bp:], norm_mix[0])[None]
    shift_s = _norm_call(x0s, norm_mix[0])[None]
    keep = min(tp, nbuf) * bp
    last = -(-keep // LANES) * LANES
    tail = _add_norm_call(x1p[n_p - last:], f0[:, n_p - last:n_p], norm_mix[1])[1][last - keep:]
    tail = jnp.concatenate([jnp.zeros((nbuf * bp - keep, d), F32), tail], axis=0)
    pool_p = tail.reshape(nbuf, bp, d).transpose(1, 0, 2)[None]
    pool_s = jnp.concatenate([state_pool[0][:, 1:], h1s[:, None, :]], axis=1)[None]
    return (y_prompt, y_sample, wkv_p[None], shift_p, pool_p, wkv_s[None], shift_s, pool_s)
```

```python
import functools

import jax
import jax.numpy as jnp
import numpy as np
from jax import lax
from jax.experimental import pallas as pl
from jax.experimental.pallas import tpu as pltpu

F32 = jnp.float32
BF16 = jnp.bfloat16

N_META = 16
RMS_EPS = 1e-6
GN_EPS = 64e-5
HEAD_DIM = 64
LANES = 128
POOL_WINDOWS = (2, 4, 8, 16)
POOL_HALO = 16
PEER_HEADS = 8
PEER_NKEYS = 128
PEER_TOPK = 16
PEER_CHUNK = 2048
PEER_TOKENS = 512
MASK_DTYPE = jnp.bfloat16
MASK_ROWS = 16
ROW_TILE = 256
STREAM_TILE = 512
VMEM_LIMIT = 56 * 1024 * 1024

NEG_INF = float("-inf")


def _cparams(sem):
    return pltpu.CompilerParams(dimension_semantics=sem, vmem_limit_bytes=VMEM_LIMIT)


def _row_tile(n, tile=None):
    return min(ROW_TILE if tile is None else tile, n)


def _rms(x, g):
    return x * lax.rsqrt(jnp.mean(x * x, axis=-1, keepdims=True) + RMS_EPS) * g


def _bdot(a, b):
    return jnp.dot(a.astype(BF16), b.astype(BF16), preferred_element_type=F32)


def _head_sum(x, blockdiag):
    outs = []
    for blk in range(x.shape[-1] // LANES):
        xb = x[:, blk * LANES:(blk + 1) * LANES]
        hi = xb.astype(BF16)
        lo = (xb - hi.astype(F32)).astype(BF16)
        outs.append(jnp.dot(hi, blockdiag, preferred_element_type=F32)
                    + jnp.dot(lo, blockdiag, preferred_element_type=F32))
    return jnp.concatenate(outs, axis=-1)


def _norm_kernel(x_ref, g_ref, h_ref):
    h_ref[...] = _rms(x_ref[...], g_ref[...])


def _norm_call(x, g):
    n, d = x.shape
    tm = _row_tile(n)
    row = pl.BlockSpec((tm, d), lambda i: (i, 0))
    return pl.pallas_call(
        _norm_kernel, grid=(pl.cdiv(n, tm),),
        in_specs=[row, pl.BlockSpec((1, d), lambda i: (0, 0))],
        out_specs=row, out_shape=jax.ShapeDtypeStruct((n, d), F32),
        compiler_params=_cparams(("parallel",)), name="rmsnorm")(x, g.reshape(1, d))


def _add_norm_kernel(a_ref, bt_ref, g_ref, x_ref, h_ref):
    x = a_ref[...] + bt_ref[...].T
    x_ref[...] = x
    h_ref[...] = _rms(x, g_ref[...])


def _add_norm_call(a, bt, g, col0=0):
    n, d = a.shape
    tm = _row_tile(n)
    assert col0 % tm == 0
    off = col0 // tm
    row = pl.BlockSpec((tm, d), lambda i: (i, 0))
    return pl.pallas_call(
        _add_norm_kernel, grid=(pl.cdiv(n, tm),),
        in_specs=[row, pl.BlockSpec((d, tm), lambda i: (0, i + off)), pl.BlockSpec((1, d), lambda i: (0, 0))],
        out_specs=[row, row],
        out_shape=[jax.ShapeDtypeStruct((n, d), F32)] * 2,
        compiler_params=_cparams(("parallel",)), name="add_rmsnorm")(a, bt, g.reshape(1, d))


def _final_norm_prompt_kernel(*refs, batch, parts):
    a_refs, bt_refs = refs[:parts], refs[parts:2 * parts]
    g_ref, o_ref, h_ref = refs[2 * parts:]
    tm = a_refs[0].shape[0]
    for p in range(parts):
        h = _rms(a_refs[p][...] + bt_refs[p][...].T, g_ref[...])
        for c in range(h_ref.shape[0]):
            h_ref[c, p * tm:(p + 1) * tm, :] = h[:, c * LANES:(c + 1) * LANES]
    steps = h_ref.shape[1] // batch
    for b in range(batch):
        for c in range(h_ref.shape[0]):
            o_ref[b, :, c * LANES:(c + 1) * LANES] = h_ref[c, pl.ds(b, steps, stride=batch), :]


def _final_norm_prompt_call(a, bt, g, batch, meta_steps):
    n, d = a.shape
    tm = meta_steps * batch
    assert n % tm == 0 and tm % LANES == 0
    blocks = n // tm - 1
    parts = 2 if blocks % 2 == 0 else 1
    rows = [pl.BlockSpec((tm, d), lambda i, p=p: (parts * i + p + 1, 0)) for p in range(parts)]
    cols = [pl.BlockSpec((d, tm), lambda i, p=p: (0, parts * i + p + 1)) for p in range(parts)]
    return pl.pallas_call(
        functools.partial(_final_norm_prompt_kernel, batch=batch, parts=parts), grid=(blocks // parts,),
        in_specs=rows + cols + [pl.BlockSpec((1, d), lambda i: (0, 0))],
        out_specs=pl.BlockSpec((batch, parts * meta_steps, d), lambda i: (0, i, 0)),
        out_shape=jax.ShapeDtypeStruct((batch, n // batch - meta_steps, d), F32),
        scratch_shapes=[pltpu.VMEM((d // LANES, parts * tm, LANES), F32)],
        compiler_params=_cparams(("parallel",)), name="final_norm_prompt")(
            *([a] * parts), *([bt] * parts), g.reshape(1, d))


def _rows_to_scan(x8):
    pairs = x8.shape[-1] // LANES
    a = jnp.concatenate([x8[:, p * LANES:(p + 1) * LANES] for p in range(pairs)], axis=0)
    at = a.T
    return jnp.concatenate([at[:HEAD_DIM], at[HEAD_DIM:]], axis=1)


def _scan_to_rows(tile):
    at = jnp.concatenate([tile[:, :HEAD_DIM], tile[:, HEAD_DIM:]], axis=0)
    a = at.T
    return jnp.concatenate([a[p * 8:(p + 1) * 8, :] for p in range(a.shape[0] // 8)], axis=1)


def _store_scan(ref, x, time_groups):
    if not time_groups:
        ref[...] = x
        return
    for j in range(x.shape[0] // 8):
        ref[j] = _rows_to_scan(x[j * 8:(j + 1) * 8, :])


def _load_scan(ref, time_groups):
    if not time_groups:
        return ref[...]
    return jnp.concatenate([_scan_to_rows(ref[j]) for j in range(ref.shape[0])], axis=0)


def _scan_spec(tm, d, time_groups):
    if time_groups:
        return pl.BlockSpec((tm // 8, HEAD_DIM, LANES), lambda i: (i, 0, 0))
    return pl.BlockSpec((tm, d), lambda i: (i, 0))


def _scan_shape(n, d, time_groups):
    return (n // 8, HEAD_DIM, LANES) if time_groups else (n, d)


def _state_to_scan(s):
    b, heads, nv, nk = s.shape
    s = s.reshape(b // 8, 8, heads // 2, 2, nv, nk).transpose(5, 4, 0, 3, 2, 1)
    return s.reshape(nk, nv, b * heads)


def _state_from_scan(s, b):
    nk, nv, lanes = s.shape
    heads = lanes // b
    return s.reshape(nk, nv, b // 8, 2, heads // 2, 8).transpose(2, 5, 4, 3, 1, 0).reshape(b, heads, nv, nk)


def _rwkv_proj_kernel(x_ref, prev_ref, gn_ref, mix_ref, wrkv_ref, dw1_ref, dw2_ref, db_ref,
                      aw1_ref, aw2_ref, ab_ref, gw1_ref, gw2_ref, kk_ref, ka_ref, rk_ref, bd_ref,
                      r_ref, dec_ref, k_ref, v_ref, aa_ref, bb_ref, g_ref, bonus_ref, *, time_groups):
    h = _rms(x_ref[...], gn_ref[...])
    if time_groups:
        before = jnp.where(pl.program_id(0) == 0, 0.0, _rms(prev_ref[...], gn_ref[...]))
        prev = jnp.concatenate([before, h[:h.shape[0] - 8]], axis=0)
    else:
        prev = prev_ref[...]
    xx = prev - h

    def xs(c):
        return h + xx * mix_ref[c:c + 1, :]

    r = _bdot(xs(0), wrkv_ref[0])
    k = _bdot(xs(1), wrkv_ref[1])
    v = _bdot(xs(2), wrkv_ref[2])
    w = -jax.nn.softplus(-(db_ref[...] + _bdot(jnp.tanh(_bdot(xs(3), dw1_ref[...])), dw2_ref[...]))) - 0.5
    a = jax.nn.sigmoid(ab_ref[...] + _bdot(_bdot(xs(4), aw1_ref[...]), aw2_ref[...]))
    g = _bdot(jax.nn.sigmoid(_bdot(xs(5), gw1_ref[...])), gw2_ref[...])
    kk = k * kk_ref[...]
    nrm = jnp.sqrt(_head_sum(kk * kk, bd_ref[...]))
    kk = kk / jnp.maximum(nrm, 1e-12)
    kf = k * (1.0 + (a - 1.0) * ka_ref[...])
    _store_scan(r_ref, r, time_groups)
    _store_scan(dec_ref, jnp.exp(-jnp.exp(w)), time_groups)
    _store_scan(k_ref, kf, time_groups)
    _store_scan(v_ref, v, time_groups)
    _store_scan(aa_ref, -kk, time_groups)
    _store_scan(bb_ref, kk * a, time_groups)
    g_ref[...] = g
    bonus_ref[...] = _head_sum(r * kf * rk_ref[...], bd_ref[...]) * v


def _rwkv_proj_params(gn, mix, w_rkv, dw1, dw2, db, aw1, aw2, ab, gw1, gw2, k_k, k_a, r_k, blockdiag):
    d = mix.shape[-1]
    return [gn.reshape(1, d), mix, w_rkv.astype(BF16), dw1.astype(BF16), dw2.astype(BF16), db.reshape(1, d),
            aw1.astype(BF16), aw2.astype(BF16), ab.reshape(1, d), gw1.astype(BF16),
            gw2.astype(BF16), k_k.reshape(1, d), k_a.reshape(1, d), r_k.reshape(1, d), blockdiag]


def _rwkv_proj_call(x, prev, params, time_groups):
    n, d = x.shape
    tm = _row_tile(n)
    assert d == 8 * LANES and tm % 8 == 0
    row = pl.BlockSpec((tm, d), lambda i: (i, 0))
    scan = _scan_spec(tm, d, time_groups)
    prev_spec = pl.BlockSpec((8, d), lambda i: (jnp.maximum(i * (tm // 8) - 1, 0), 0)) if time_groups else row

    def full(x):
        nd = x.ndim
        return pl.BlockSpec(x.shape, lambda i: (0,) * nd)

    return pl.pallas_call(
        functools.partial(_rwkv_proj_kernel, time_groups=time_groups), grid=(pl.cdiv(n, tm),),
        in_specs=[row, prev_spec] + [full(p) for p in params],
        out_specs=[scan] * 6 + [row] * 2,
        out_shape=[jax.ShapeDtypeStruct(_scan_shape(n, d, time_groups), F32)] * 6
        + [jax.ShapeDtypeStruct((n, d), F32)] * 2,
        compiler_params=_cparams(("parallel",)), name="rwkv_proj")(x, prev, *params)


def _wkv_kernel(r_ref, d_ref, k_ref, v_ref, a_ref, b_ref, anext_ref, s0_ref, y_ref, s_ref, sa_ref, *, steps):
    nk = s_ref.shape[0]

    @pl.when(pl.program_id(1) == 0)
    def _():
        s_ref[...] = s0_ref[...]
        sa0 = None
        for kk in range(nk):
            term = s0_ref[kk] * a_ref[0, kk:kk + 1, :]
            sa0 = term if sa0 is None else sa0 + term
        sa_ref[...] = sa0

    def step(t, sa, next_row):
        vv = v_ref[t]
        y = None
        sa_next = None
        for kk in range(nk):
            sk = (s_ref[kk] * d_ref[t, kk:kk + 1, :] + sa * b_ref[t, kk:kk + 1, :]
                  + vv * k_ref[t, kk:kk + 1, :])
            s_ref[kk] = sk
            ty = sk * r_ref[t, kk:kk + 1, :]
            ta = sk * next_row(kk)
            y = ty if y is None else y + ty
            sa_next = ta if sa_next is None else sa_next + ta
        y_ref[t] = y
        return sa_next

    sa = lax.fori_loop(0, steps - 1, lambda t, sa: step(t, sa, lambda kk: a_ref[t + 1, kk:kk + 1, :]),
                       sa_ref[...])
    sa_ref[...] = step(steps - 1, sa, lambda kk: anext_ref[0, kk:kk + 1, :])


def _wkv_call(r, dec, k, v, aa, bb, s0, steps):
    t_len, hd, lanes = r.shape
    seq = pl.BlockSpec((steps, hd, LANES), lambda l, t: (t, 0, l))
    nxt = pl.BlockSpec((1, hd, LANES), lambda l, t: (jnp.minimum((t + 1) * steps, t_len - 1), 0, l))
    st = pl.BlockSpec((hd, hd, LANES), lambda l, t: (0, 0, l))
    return pl.pallas_call(
        functools.partial(_wkv_kernel, steps=steps),
        grid=(lanes // LANES, t_len // steps),
        in_specs=[seq] * 6 + [nxt, st],
        out_specs=[seq, st],
        out_shape=[jax.ShapeDtypeStruct((t_len, hd, lanes), F32),
                   jax.ShapeDtypeStruct((hd, hd, lanes), F32)],
        scratch_shapes=[pltpu.VMEM((hd, LANES), F32)],
        compiler_params=_cparams(("parallel", "arbitrary")), name="wkv_scan")(r, dec, k, v, aa, bb, aa, s0)


def _wkv_group(scan_inputs, s0, steps):
    y, s_l = _wkv_call(*scan_inputs, _state_to_scan(s0), steps)
    return y, _state_from_scan(s_l, s0.shape[0])


def _wkv_step_kernel(r_ref, d_ref, k_ref, v_ref, a_ref, b_ref, s0_ref, y_ref, s_ref):
    nb, heads, nv, nk = s0_ref.shape
    eye = (lax.broadcasted_iota(jnp.int32, (nv, nk), 0)
           == lax.broadcasted_iota(jnp.int32, (nv, nk), 1)).astype(F32)
    group = 8
    for b in range(nb):
        for h0 in range(0, heads, group):
            hs = range(h0, min(h0 + group, heads))
            sa = [jnp.sum(s0_ref[b, h] * a_ref[b, h:h + 1, :], axis=-1, keepdims=True) for h in hs]
            v_col = [jnp.sum(eye * v_ref[b, h:h + 1, :], axis=-1, keepdims=True) for h in hs]
            y_col = []
            for i, h in enumerate(hs):
                s_new = (s0_ref[b, h] * d_ref[b, h:h + 1, :] + sa[i] * b_ref[b, h:h + 1, :]
                         + v_col[i] * k_ref[b, h:h + 1, :])
                s_ref[b, h] = s_new
                y_col.append(jnp.sum(s_new * r_ref[b, h:h + 1, :], axis=-1, keepdims=True))
            for i, h in enumerate(hs):
                y_ref[b, h:h + 1, :] = jnp.sum(eye * y_col[i], axis=0, keepdims=True)


def _wkv_step_call(rows, s0):
    b, heads, nv, nk = s0.shape
    assert nv == nk
    nb = 8
    vec = pl.BlockSpec((nb, heads, nk), lambda i: (i, 0, 0))
    st = pl.BlockSpec((nb, heads, nv, nk), lambda i: (i, 0, 0, 0))
    y, s_out = pl.pallas_call(
        _wkv_step_kernel, grid=(b // nb,),
        in_specs=[vec] * 6 + [st], out_specs=[vec, st],
        out_shape=[jax.ShapeDtypeStruct((b, heads, nv), F32), jax.ShapeDtypeStruct(s0.shape, F32)],
        compiler_params=_cparams(("parallel",)), name="wkv_step")(
            *[x.reshape(b, heads, nk) for x in rows], s0)
    return y.reshape(b, heads * nv), s_out


def _rwkv_out_kernel(y_ref, bonus_ref, g_ref, x_ref, lng_ref, lnb_ref, wo_ref, bd_ref, o_ref, *, time_groups):
    bd = bd_ref[...]
    y = _load_scan(y_ref, time_groups)
    inv_n = 1.0 / HEAD_DIM
    mu = _head_sum(y, bd) * inv_n
    yc = y - mu
    var = _head_sum(yc * yc, bd) * inv_n
    yn = yc * lax.rsqrt(var + GN_EPS) * lng_ref[...] + lnb_ref[...]
    o_ref[...] = x_ref[...] + _bdot((yn + bonus_ref[...]) * g_ref[...], wo_ref[...])


def _rwkv_out_call(y, bonus, g, x, ln_g, ln_b, w_o, blockdiag, time_groups):
    n, d = x.shape
    tm = _row_tile(n, STREAM_TILE)
    row = pl.BlockSpec((tm, d), lambda i: (i, 0))
    vec = pl.BlockSpec((1, d), lambda i: (0, 0))
    return pl.pallas_call(
        functools.partial(_rwkv_out_kernel, time_groups=time_groups), grid=(pl.cdiv(n, tm),),
        in_specs=[_scan_spec(tm, d, time_groups), row, row, row, vec, vec,
                  pl.BlockSpec((d, d), lambda i: (0, 0)), pl.BlockSpec((LANES, LANES), lambda i: (0, 0))],
        out_specs=row, out_shape=jax.ShapeDtypeStruct((n, d), F32),
        compiler_params=_cparams(("parallel",)), name="rwkv_out")(
            y, bonus, g, x, ln_g.reshape(1, d), ln_b.reshape(1, d), w_o, blockdiag)


def _peer_scores_kernel(x_ref, g_ref, wq_ref, keys_ref, s_ref, ht_ref):
    h = _rms(x_ref[...], g_ref[...])
    ht_ref[...] = h.T.astype(BF16)
    q = _bdot(h, wq_ref[...]).astype(BF16)
    for hc in range(keys_ref.shape[0]):
        s_ref[hc] = lax.dot_general(keys_ref[hc], q[:, hc * LANES:(hc + 1) * LANES],
                                    (((1,), (1,)), ((), ())), preferred_element_type=F32)


def _peer_scores_into_kernel(x_ref, g_ref, wq_ref, keys_ref, s_in_ref, ht_in_ref, s_ref, ht_ref):
    del s_in_ref, ht_in_ref
    _peer_scores_kernel(x_ref, g_ref, wq_ref, keys_ref, s_ref, ht_ref)


def _peer_scores_call(x, g, wq, keys, n_total, col0, into=None):
    n, d = x.shape
    tm = _row_tile(n, STREAM_TILE)
    nhc = keys.shape[0]
    assert col0 % tm == 0
    off = col0 // tm
    in_specs = [pl.BlockSpec((tm, d), lambda i: (i, 0)), pl.BlockSpec((1, d), lambda i: (0, 0)),
                pl.BlockSpec(wq.shape, lambda i: (0, 0)), pl.BlockSpec(keys.shape, lambda i: (0, 0, 0))]
    args = [x, g.reshape(1, d), wq, keys]
    body, aliases = _peer_scores_kernel, {}
    if into is not None:
        in_specs += [pl.BlockSpec(memory_space=pl.ANY)] * 2
        args += list(into)
        body, aliases = _peer_scores_into_kernel, {4: 0, 5: 1}
    return pl.pallas_call(
        body, grid=(pl.cdiv(n, tm),), in_specs=in_specs,
        out_specs=[pl.BlockSpec((nhc, PEER_NKEYS, tm), lambda i: (0, 0, i + off)),
                   pl.BlockSpec((d, tm), lambda i: (0, i + off))],
        out_shape=[jax.ShapeDtypeStruct((nhc, PEER_NKEYS, n_total), F32),
                   jax.ShapeDtypeStruct((d, n_total), BF16)],
        input_output_aliases=aliases,
        compiler_params=_cparams(("parallel",)), name="peer_scores")(*args)


def _top16(s, iota, exact_ties, want_rank=True):
    keys = s
    rank = jnp.full(s.shape, float(PEER_TOPK), F32) if want_rank else None
    vals = []
    for a in range(PEER_TOPK):
        m = jnp.max(s, axis=0, keepdims=True)
        hit = s == m
        if exact_ties:
            idx = jnp.min(jnp.where(hit, iota, float(PEER_NKEYS)), axis=0, keepdims=True)
            hit = iota == idx
        if want_rank:
            rank = jnp.where(hit, float(a), rank)
        s = jnp.where(hit, NEG_INF, s)
        vals.append(m)
    taken = (rank < float(PEER_TOPK)) if want_rank else (keys >= vals[-1])
    return rank, vals, jnp.sum(jnp.where(taken, 1.0, 0.0), axis=0, keepdims=True)


def _dup_mask_bits(x):
    bits = lax.bitcast_convert_type(x.astype(MASK_DTYPE).astype(F32), jnp.uint32)
    return bits | (bits >> 16)


def _peer_select_kernel(s_ref, l_ref, p1_ref, r2_ref, p2_ref):
    miscount = _peer_select_body(s_ref, l_ref, p1_ref, r2_ref, p2_ref, exact_ties=False)

    @pl.when(jnp.max(miscount) > 0.0)
    def _():
        _peer_select_body(s_ref, l_ref, p1_ref, r2_ref, p2_ref, exact_ties=True)


def _peer_select_body(s_ref, l_ref, p1_ref, r2_ref, p2_ref, exact_ties):
    lanes = s_ref.shape[-1]
    iota = lax.broadcasted_iota(jnp.int32, (PEER_NKEYS, lanes), 0).astype(F32)
    iota16 = lax.broadcasted_iota(jnp.int32, (PEER_TOPK, lanes), 0).astype(F32)
    miscount = jnp.zeros((1, lanes), F32)
    for h in range(PEER_HEADS):
        s1 = s_ref[2 * h]
        s2 = s_ref[2 * h + 1]
        rank1, v1, n1 = _top16(s1, iota, exact_ties, want_rank=exact_ties)
        rank2, v2, n2 = _top16(s2, iota, exact_ties)
        miscount = miscount + jnp.abs(n1 - float(PEER_TOPK)) + jnp.abs(n2 - float(PEER_TOPK))
        v1a = jnp.concatenate(v1, axis=0)
        v2a = jnp.concatenate(v2, axis=0)
        top = v1[0] + v2[0]
        taken = jnp.zeros((PEER_TOPK, lanes), F32)
        front = v1a + v2[0]
        zsum = jnp.zeros((1, lanes), F32)
        for _ in range(PEER_TOPK):
            m = jnp.max(front, axis=0, keepdims=True)
            a_star = jnp.min(jnp.where(front == m, iota16, float(PEER_TOPK)), axis=0, keepdims=True)
            hit = iota16 == a_star
            zsum = zsum + jnp.exp(m - top)
            taken = jnp.where(hit, taken + 1.0, taken)
            cnt = jnp.max(jnp.where(hit, taken, -1.0), axis=0, keepdims=True)
            nxt = jnp.max(jnp.where(iota16 == cnt, v2a, NEG_INF), axis=0, keepdims=True)
            front = jnp.where(hit, v1a + nxt, front)
        lim = jnp.zeros((PEER_NKEYS, lanes), F32)
        for a in range(PEER_TOPK):
            is_a = (rank1 == float(a)) if exact_ties else (s1 == v1[a])
            lim = jnp.where(is_a, taken[a:a + 1, :], lim)
        l_ref[h] = _dup_mask_bits(lim)
        p1_ref[h] = _dup_mask_bits(jnp.exp(s1 - v1[0]))
        p2 = jnp.exp(s2 - v2[0]) / zsum
        for r in range(PEER_NKEYS // MASK_ROWS):
            rows = slice(r * MASK_ROWS, (r + 1) * MASK_ROWS)
            r2_ref[h, r] = rank2[rows].astype(MASK_DTYPE)
            p2_ref[h, r] = p2[rows].astype(MASK_DTYPE)
    return miscount


def _peer_select_call(scores):
    nhc, nk, n = scores.shape
    tl = LANES
    groups = nk // MASK_ROWS
    words = pl.BlockSpec((PEER_HEADS, nk, tl), lambda i: (0, 0, i))
    packed = pl.BlockSpec((PEER_HEADS, groups, MASK_ROWS, tl), lambda i: (0, 0, 0, i))
    return pl.pallas_call(
        _peer_select_kernel, grid=(pl.cdiv(n, tl),),
        in_specs=[pl.BlockSpec((nhc, nk, tl), lambda i: (0, 0, i))],
        out_specs=[words, words, packed, packed],
        out_shape=[jax.ShapeDtypeStruct((PEER_HEADS, nk, n), jnp.uint32)] * 2
        + [jax.ShapeDtypeStruct((PEER_HEADS, groups, MASK_ROWS, n), MASK_DTYPE)] * 2,
        compiler_params=_cparams(("parallel",)), name="peer_select")(scores)


def _gate_weights(cb, ii, heads, w, l_ref, p1_ref, r2_ref, p2_ref):
    blocks = PEER_CHUNK // PEER_NKEYS
    base = pl.multiple_of(cb * blocks + (ii // 8) * 8, 8)
    sub = ii % 8
    w = list(w)
    for h in heads:
        lw = l_ref[h, pl.ds(base, 8), :]
        pw = p1_ref[h, pl.ds(base, 8), :]
        lim = pltpu.bitcast(jnp.broadcast_to(lw[sub:sub + 1, :], lw.shape), MASK_DTYPE)
        p1 = pltpu.bitcast(jnp.broadcast_to(pw[sub:sub + 1, :], pw.shape), MASK_DTYPE)
        for r in range(len(w)):
            term = jnp.where(r2_ref[h, r] < lim, p2_ref[h, r], jnp.zeros_like(p1)) * p1
            w[r] = term if w[r] is None else w[r] + term
    return w


def _gate_store(ii, w, z_ref, g_ref):
    for r in range(len(w)):
        rows = slice(ii * PEER_NKEYS + r * MASK_ROWS, ii * PEER_NKEYS + (r + 1) * MASK_ROWS)
        z = z_ref[rows, :]
        act = 0.5 * z * (1.0 + lax.erf(z * np.float32(np.sqrt(0.5))))
        g_ref[rows, :] = (act.astype(MASK_DTYPE) * w[r]).astype(g_ref.dtype)


def _peer_dense_kernel(ht_ref, u_ref, v_ref, l_ref, p1_ref, r2_ref, p2_ref, o_ref,
                       z0_ref, z1_ref, g0_ref, g1_ref, acc_ref, *, items, chunks):
    s = pl.program_id(0)

    @pl.when(s == 0)
    def _():
        for ref in (z0_ref, z1_ref, g0_ref, g1_ref):
            ref[...] = jnp.zeros_like(ref)

    gate_chunk = jnp.clip(s - 1, 0, items - 1) % chunks

    def stages(z_new, z_old, g_new, g_old):
        blocks = PEER_CHUNK // PEER_NKEYS
        sel = (l_ref, p1_ref, r2_ref, p2_ref)
        half = PEER_HEADS // 2
        per_v = blocks // (o_ref.shape[0] // PEER_NKEYS)
        for ii in range(blocks):
            er = slice(ii * PEER_NKEYS, (ii + 1) * PEER_NKEYS)
            w = _gate_weights(gate_chunk, ii, range(half), [None] * (PEER_NKEYS // MASK_ROWS), *sel)
            z_new[er, :] = jnp.dot(u_ref[er, :], ht_ref[...], preferred_element_type=F32)
            w = _gate_weights(gate_chunk, ii, range(half, PEER_HEADS), w, *sel)
            _gate_store(ii, w, z_old, g_new)
            if ii % per_v == per_v - 1:
                dr = slice((ii // per_v) * PEER_NKEYS, (ii // per_v + 1) * PEER_NKEYS)
                acc_ref[dr, :] = lax.dot_general(v_ref[:, dr], g_old[...], (((0,), (0,)), ((), ())),
                                                 preferred_element_type=F32)

    @pl.when(s % 2 == 0)
    def _():
        stages(z0_ref, z1_ref, g1_ref, g0_ref)

    @pl.when(s % 2 == 1)
    def _():
        stages(z1_ref, z0_ref, g0_ref, g1_ref)

    first = jnp.clip(s - 2, 0, items - 1) % chunks == 0

    @pl.when(first)
    def _():
        o_ref[...] = acc_ref[...]

    @pl.when(jnp.logical_not(first))
    def _():
        o_ref[...] += acc_ref[...]


def _peer_dense_call(ht, u, v, lim, p1, r2, p2):
    d, n = ht.shape
    chunks = u.shape[0] // PEER_CHUNK
    tt = min(PEER_TOKENS, n)
    items = pl.cdiv(n, tt) * chunks

    def item(s, lag):
        return jnp.clip(s - lag, 0, items - 1)

    words = pl.BlockSpec((PEER_HEADS, PEER_NKEYS, tt), lambda s: (0, 0, item(s, 1) // chunks))
    packed = pl.BlockSpec(r2.shape[:3] + (tt,), lambda s: (0, 0, 0, item(s, 1) // chunks))
    return pl.pallas_call(
        functools.partial(_peer_dense_kernel, items=items, chunks=chunks),
        grid=(items + 2,),
        in_specs=[pl.BlockSpec((d, tt), lambda s: (0, item(s, 0) // chunks)),
                  pl.BlockSpec((PEER_CHUNK, d), lambda s: (item(s, 0) % chunks, 0)),
                  pl.BlockSpec((PEER_CHUNK, d), lambda s: (item(s, 2) % chunks, 0)),
                  words, words, packed, packed],
        out_specs=pl.BlockSpec((d, tt), lambda s: (0, item(s, 2) // chunks)),
        out_shape=jax.ShapeDtypeStruct((d, n), F32),
        scratch_shapes=[pltpu.VMEM((PEER_CHUNK, tt), F32)] * 2 + [pltpu.VMEM((PEER_CHUNK, tt), BF16)] * 2
        + [pltpu.VMEM((d, tt), F32)],
        compiler_params=_cparams(("arbitrary",)), name="peer_dense")(ht, u, v, lim, p1, r2, p2)


def _peer_params(g, wq, keys, u, v):
    nh, _, nk, half = keys.shape
    return g, wq.astype(BF16), keys.reshape(nh * 2, nk, half).astype(BF16), u.astype(BF16), v.astype(BF16)


def _peer(x_a, x_b, params):
    g, wq, keys, u, v = params
    n_a, n_b = x_a.shape[0], x_b.shape[0]
    first = _peer_scores_call(x_a, g, wq, keys, n_a + n_b, 0)
    scores, ht = _peer_scores_call(x_b, g, wq, keys, n_a + n_b, n_a, into=first)
    lim, p1, r2, p2 = _peer_select_call(scores)
    return _peer_dense_call(ht, u, v, lim, p1, r2, p2)


def _pool_prompt_kernel(xh_ref, fth_ref, x_ref, ft_ref, gn_ref, w_ref, sc_ref, o_ref, *, batch):
    tm = x_ref.shape[0]
    hr = xh_ref.shape[0]
    x = x_ref[...] + ft_ref[...].T
    cur = _rms(x, gn_ref[...])
    halo = jnp.where(pl.program_id(0) == 0, 0.0,
                     _rms(xh_ref[...] + fth_ref[...].T, gn_ref[...]))
    ext = jnp.concatenate([halo, cur], axis=0)
    row = (pl.program_id(0) * tm + lax.broadcasted_iota(jnp.int32, (tm, 1), 0)).astype(F32)
    pos = jnp.floor(row / float(batch))
    gd = cur.shape[-1] // len(POOL_WINDOWS)
    outs = []
    for gi, w in enumerate(POOL_WINDOWS):
        acc = ext[:, gi * gd:(gi + 1) * gd]
        dropped = 0
        shift = 1
        while shift < w:
            k = shift * batch
            acc = acc[k:] + acc[:-k]
            dropped += k
            shift *= 2
        cnt = jnp.minimum(float(w), pos + 1.0)
        mixed = acc[hr - dropped:] / cnt - cur[:, gi * gd:(gi + 1) * gd]
        outs.append(_bdot(mixed, w_ref[gi]))
    o_ref[...] = x + jnp.concatenate(outs, axis=-1) * sc_ref[...]


def _pool_prompt_call(x, ft, gn, pool_w, pool_scale, batch):
    n, d = x.shape
    tm = _row_tile(n, STREAM_TILE)
    hr = POOL_HALO * batch
    assert tm % hr == 0 and hr % LANES == 0

    def before(i):
        return jnp.maximum(i * (tm // hr) - 1, 0)

    vec = pl.BlockSpec((1, d), lambda i: (0, 0))
    return pl.pallas_call(
        functools.partial(_pool_prompt_kernel, batch=batch), grid=(pl.cdiv(n, tm),),
        in_specs=[pl.BlockSpec((hr, d), lambda i: (before(i), 0)), pl.BlockSpec((d, hr), lambda i: (0, before(i))),
                  pl.BlockSpec((tm, d), lambda i: (i, 0)), pl.BlockSpec((d, tm), lambda i: (0, i)), vec,
                  pl.BlockSpec(pool_w.shape, lambda i: (0, 0, 0)), vec],
        out_specs=pl.BlockSpec((tm, d), lambda i: (i, 0)), out_shape=jax.ShapeDtypeStruct((n, d), F32),
        compiler_params=_cparams(("parallel",)), name="pool_prompt")(
            x, ft, x, ft, gn.reshape(1, d), pool_w, pool_scale.reshape(1, d))


def _pool_sample_kernel(hist_ref, h_ref, x_ref, w_ref, sc_ref, o_ref):
    h = h_ref[...]
    nhist = hist_ref.shape[0]
    gd = h.shape[-1] // len(POOL_WINDOWS)
    outs = []
    for gi, w in enumerate(POOL_WINDOWS):
        sl = slice(gi * gd, (gi + 1) * gd)
        acc = h[:, sl]
        for back in range(1, w):
            acc = acc + hist_ref[nhist - back][:, sl]
        mixed = acc / float(w) - h[:, sl]
        outs.append(_bdot(mixed, w_ref[gi]))
    o_ref[...] = x_ref[...] + jnp.concatenate(outs, axis=-1) * sc_ref[...]


def _pool_sample_call(hist, h, x, pool_w, pool_scale):
    b, d = h.shape
    full2 = pl.BlockSpec((b, d), lambda i: (0, 0))
    return pl.pallas_call(
        _pool_sample_kernel, grid=(1,),
        in_specs=[pl.BlockSpec(hist.shape, lambda i: (0, 0, 0)), full2, full2,
                  pl.BlockSpec(pool_w.shape, lambda i: (0, 0, 0)),
                  pl.BlockSpec((1, d), lambda i: (0, 0))],
        out_specs=full2, out_shape=jax.ShapeDtypeStruct((b, d), F32),
        compiler_params=_cparams(("arbitrary",)), name="pool_sample")(
            hist, h, x, pool_w, pool_scale.reshape(1, d))


def _time_block(t, pref):
    for c in range(min(pref, t), 0, -1):
        if t % c == 0 and (c % 8 == 0 or c == t):
            return c
    return t


def kernel(x_prompt, x_sample, state_wkv, state_shift, state_pool, meta_tokens, norm_mix, norm_ffn, norm_final, rwkv_mix, rwkv_w_rkv, rwkv_decay_w1, rwkv_decay_w2, rwkv_decay_b, rwkv_iclr_w1, rwkv_iclr_w2, rwkv_iclr_b, rwkv_gate_w1, rwkv_gate_w2, rwkv_k_k, rwkv_k_a, rwkv_r_k, rwkv_ln_g, rwkv_ln_b, rwkv_w_o, pool_w, pool_scale, peer_wq, peer_keys, peer_u, peer_v):
    bp, seq, d = x_prompt.shape
    bs = x_sample.shape[0]
    tp = N_META + seq
    n_p = bp * tp
    heads = d // HEAD_DIM
    assert x_sample.shape[1] == 1 and bp == 8 and bs % 8 == 0 and heads * 8 == LANES
    assert rwkv_w_rkv.shape[0] == 1 and pool_w.shape[0] == 1 and norm_mix.shape[0] == 2

    blockdiag = jnp.asarray(np.kron(np.eye(LANES // HEAD_DIM), np.ones((HEAD_DIM, HEAD_DIM))), BF16)
    proj_params = _rwkv_proj_params(
        norm_mix[0], rwkv_mix[0], rwkv_w_rkv[0], rwkv_decay_w1[0], rwkv_decay_w2[0], rwkv_decay_b[0], rwkv_iclr_w1[0],
        rwkv_iclr_w2[0], rwkv_iclr_b[0], rwkv_gate_w1[0], rwkv_gate_w2[0], rwkv_k_k[0], rwkv_k_a[0],
        rwkv_r_k[0], blockdiag)
    w_o = rwkv_w_o[0].astype(BF16)
    peer_params = [_peer_params(norm_ffn[i], peer_wq[i], peer_keys[i], peer_u[i], peer_v[i]) for i in range(2)]
    pool_wb = pool_w[0].astype(BF16)
    meta = jnp.broadcast_to(meta_tokens[:, None], (N_META, bp, d))
    x0p = jnp.concatenate([meta, x_prompt.transpose(1, 0, 2)], axis=0).reshape(n_p, d)
    x0s = x_sample.reshape(bs, d)

    def rwkv_layer(x0, prev, s0, time_groups):
        *scan_inputs, g, bonus = _rwkv_proj_call(x0, prev, proj_params, time_groups)
        if time_groups:
            y, s_out = _wkv_group(scan_inputs, s0, _time_block(tp, 48))
        else:
            y, s_out = _wkv_step_call(scan_inputs, s0)
        x1 = _rwkv_out_call(y, bonus, g, x0, rwkv_ln_g[0], rwkv_ln_b[0], w_o, blockdiag, time_groups)
        return s_out, x1

    wkv_p, x1p = rwkv_layer(x0p, x0p, jnp.zeros((bp, heads, HEAD_DIM, HEAD_DIM), F32), True)
    wkv_s, x1s = rwkv_layer(x0s, state_shift[0], state_wkv[0], False)
    f0 = _peer(x1p, x1s, peer_params[0])

    x2s, h1s = _add_norm_call(x1s, f0, norm_mix[1], col0=n_p)
    x3p = _pool_prompt_call(x1p, f0, norm_mix[1], pool_wb, pool_scale[0], bp)
    x3s = _pool_sample_call(state_pool[0].transpose(1, 0, 2), h1s, x2s, pool_wb, pool_scale[0])
    f1 = _peer(x3p, x3s, peer_params[1])
    y_prompt = _final_norm_prompt_call(x3p, f1, norm_final, bp, N_META)
    _, yfs = _add_norm_call(x3s, f1, norm_final, col0=n_p)

    nbuf = state_pool.shape[2]
    y_sample = yfs.reshape(bs, 1, d)
    shift_p = _norm_call(x0p[n_p - bp:], norm_mix[0])[None]
    shift_s = _norm_call(x0s, norm_mix[0])[None]
    keep = min(tp, nbuf) * bp
    last = -(-keep // LANES) * LANES
    tail = _add_norm_call(x1p[n_p - last:], f0[:, n_p - last:n_p], norm_mix[1])[1][last - keep:]
    tail = jnp.concatenate([jnp.zeros((nbuf * bp - keep, d), F32), tail], axis=0)
    pool_p = tail.reshape(nbuf, bp, d).transpose(1, 0, 2)[None]
    pool_s = jnp.concatenate([state_pool[0][:, 1:], h1s[:, None, :]], axis=1)[None]
    return (y_prompt, y_sample, wkv_p[None], shift_p, pool_p, wkv_s[None], shift_s, pool_s)
```
